```python
import math
import jax, jax.numpy as jnp
from jax import lax
import numpy as np

D_MODEL = 1024
BATCH = 4
SEQ = 4096
DEPTH = 4
DEC_BATCH = 32
DEC_SEQ = 4
PAST_LEN = 8192
PAGE_SIZE = 128

N_META = 16
FOX_HEADS = 8
FOX_HEAD_DIM = 64
FOX_WIDTH = FOX_HEADS * FOX_HEAD_DIM
FOX_Q_BLOCK = 128
S5_GROUP_SIZE = 16
S5_GROUPS = 32
S5_WIDTH = S5_GROUPS * S5_GROUP_SIZE
S5_STATE = 64
GDN_HEADS = 4
GDN_HEAD_DIM = 128
GDN_WIDTH = GDN_HEADS * GDN_HEAD_DIM
GDN_CONV = 4
GDN_CHUNK = 64
N_BRANCH = 3
D_FF = 2816
RMS_EPS = 1e-6
IN_SIZES = (FOX_WIDTH, FOX_WIDTH, FOX_WIDTH, FOX_HEADS,
            S5_WIDTH,
            3 * GDN_WIDTH, GDN_HEADS, GDN_HEADS, GDN_WIDTH,
            N_BRANCH * D_MODEL)
D_IN = sum(IN_SIZES)

kernel_name = 'hybrid_fox_s5_gdn_macaron_step'


def _rmsnorm(x, w):
    xf = x.astype(jnp.float32)
    y = xf * lax.rsqrt(jnp.mean(xf * xf, axis=-1, keepdims=True) + RMS_EPS)
    return (y * w.astype(jnp.float32)).astype(x.dtype)


def _swiglu(h, w_in, w_out):
    gate, up = jnp.split(h @ w_in, 2, axis=-1)
    return (jax.nn.silu(gate) * up) @ w_out


def _split_cols(proj):
    pieces, start = [], 0
    for size in IN_SIZES:
        pieces.append(proj[..., start:start + size])
        start += size
    return pieces


def _fox_attend(q, k, v, fq, fk, q_pos):
    s = jnp.einsum('bqhd,bkhd->bhqk', q, k).astype(jnp.float32) * (FOX_HEAD_DIM ** -0.5)
    s = s + jnp.transpose(fq, (0, 2, 1))[:, :, :, None] - jnp.transpose(fk, (0, 2, 1))[:, :, None, :]
    mask = jnp.arange(k.shape[1])[None, :] <= q_pos[:, None]
    s = jnp.where(mask, s, -jnp.inf)
    p = jax.nn.softmax(s, axis=-1).astype(v.dtype)
    return jnp.einsum('bhqk,bkhd->bqhd', p, v)


def _fox_prompt(q, k, v, f_cum):
    b, length = q.shape[0], q.shape[1]
    n_blocks = (length - N_META) // FOX_Q_BLOCK
    o_meta = _fox_attend(q[:, :N_META], k[:, :N_META], v[:, :N_META],
                         f_cum[:, :N_META], f_cum[:, :N_META], jnp.arange(N_META))
    qb = q[:, N_META:].reshape(b, n_blocks, FOX_Q_BLOCK, FOX_HEADS, FOX_HEAD_DIM).transpose(1, 0, 2, 3, 4)
    fb = f_cum[:, N_META:].reshape(b, n_blocks, FOX_Q_BLOCK, FOX_HEADS).transpose(1, 0, 2, 3)
    pb = N_META + jnp.arange(n_blocks * FOX_Q_BLOCK).reshape(n_blocks, FOX_Q_BLOCK)
    ob = lax.map(lambda a: _fox_attend(a[0], k, v, a[1], f_cum, a[2]), (qb, fb, pb))
    o_real = ob.transpose(1, 0, 2, 3, 4).reshape(b, n_blocks * FOX_Q_BLOCK, FOX_HEADS, FOX_HEAD_DIM)
    return jnp.concatenate([o_meta, o_real], axis=1)


def _complex_affine_combine(e1, e2):
    a1r, a1i, b1r, b1i = e1
    a2r, a2i, b2r, b2i = e2
    return (a2r * a1r - a2i * a1i, a2r * a1i + a2i * a1r,
            a2r * b1r - a2i * b1i + b2r, a2r * b1i + a2i * b1r + b2i)


def _s5_scan(u, x0_re, x0_im, a_re, a_im, b_re, b_im, c_re, c_im, d_skip, log_dt):
    f32 = jnp.float32
    bsz, length = u.shape[0], u.shape[1]
    ug = u.astype(f32).reshape(bsz, length, S5_GROUPS, S5_GROUP_SIZE)
    dt = jnp.exp(log_dt.astype(f32))[:, None]
    lam_re, lam_im = a_re.astype(f32), a_im.astype(f32)
    mag = jnp.exp(lam_re * dt)
    ang = lam_im * dt
    abar_re, abar_im = mag * jnp.cos(ang), mag * jnp.sin(ang)
    den = lam_re * lam_re + lam_im * lam_im
    coef_re = ((abar_re - 1.0) * lam_re + abar_im * lam_im) / den
    coef_im = (abar_im * lam_re - (abar_re - 1.0) * lam_im) / den
    br, bi = b_re.astype(f32), b_im.astype(f32)
    bbar_re = coef_re[..., None] * br - coef_im[..., None] * bi
    bbar_im = coef_re[..., None] * bi + coef_im[..., None] * br
    bu_re = jnp.einsum('blgc,gpc->blgp', ug, bbar_re)
    bu_im = jnp.einsum('blgc,gpc->blgp', ug, bbar_im)
    x0r, x0i = x0_re.astype(f32), x0_im.astype(f32)
    bu_re = bu_re.at[:, 0].add(abar_re * x0r - abar_im * x0i)
    bu_im = bu_im.at[:, 0].add(abar_re * x0i + abar_im * x0r)
    elems = (jnp.broadcast_to(abar_re, bu_re.shape), jnp.broadcast_to(abar_im, bu_im.shape), bu_re, bu_im)
    _, _, xr, xi = lax.associative_scan(_complex_affine_combine, elems, axis=1)
    y = (jnp.einsum('blgp,gcp->blgc', xr, c_re.astype(f32))
         - jnp.einsum('blgp,gcp->blgc', xi, c_im.astype(f32))
         + d_skip.astype(f32) * ug)
    return y.reshape(bsz, length, S5_WIDTH), xr[:, -1], xi[:, -1]


def _short_conv(x, buf, w):
    xc = jnp.concatenate([buf.astype(x.dtype), x], axis=1)
    length = x.shape[1]
    y = xc[:, 0:length] * w[0]
    for j in range(1, GDN_CONV):
        y = y + xc[:, j:j + length] * w[j]
    return jax.nn.silu(y), xc[:, -(GDN_CONV - 1):]


def _l2norm(x):
    return x * lax.rsqrt(jnp.sum(x * x, axis=-1, keepdims=True) + 1e-6)


def _gdn_chunk(s, q, k, v, g, beta):
    q, k, v = (jnp.transpose(t, (0, 2, 1, 3)) for t in (q, k, v))
    g, beta = jnp.transpose(g, (0, 2, 1)), jnp.transpose(beta, (0, 2, 1))
    cum = jnp.cumsum(g, axis=-1)
    c = q.shape[2]
    idx = jnp.arange(c)
    incl = idx[:, None] >= idx[None, :]
    strict = idx[:, None] > idx[None, :]
    diff = cum[..., :, None] - cum[..., None, :]
    decay = jnp.where(incl, jnp.exp(jnp.where(incl, diff, 0.0)), 0.0)
    kb = k * beta[..., None]
    lower = jnp.where(strict, jnp.einsum('bhid,bhjd->bhij', kb, k) * decay, 0.0)
    rhs = jnp.concatenate([v * beta[..., None], kb * jnp.exp(cum)[..., None]], axis=-1)
    sol = lax.linalg.triangular_solve(lower, rhs, left_side=True, lower=True, unit_diagonal=True)
    dv = v.shape[-1]
    value, kcum = sol[..., :dv], sol[..., dv:]
    v_new = value - jnp.einsum('bhcd,bhde->bhce', kcum, s)
    o = (jnp.einsum('bhcd,bhde->bhce', q * jnp.exp(cum)[..., None], s)
         + jnp.einsum('bhij,bhje->bhie', jnp.einsum('bhid,bhjd->bhij', q, k) * decay, v_new))
    g_last = cum[..., -1]
    s_new = (s * jnp.exp(g_last)[..., None, None]
             + jnp.einsum('bhcd,bhce->bhde', k * jnp.exp(g_last[..., None] - cum)[..., None], v_new))
    return s_new, jnp.transpose(o, (0, 2, 1, 3))


def _gdn_prompt(s0, q, k, v, g, beta):
    b, length = q.shape[0], q.shape[1]
    s1, o_meta = _gdn_chunk(s0, q[:, :N_META], k[:, :N_META], v[:, :N_META], g[:, :N_META], beta[:, :N_META])
    n_chunks = (length - N_META) // GDN_CHUNK

    def to_chunks(t):
        t = t[:, N_META:]
        return jnp.moveaxis(t.reshape((b, n_chunks, GDN_CHUNK) + t.shape[2:]), 1, 0)

    s_fin, o_c = lax.scan(lambda st, a: _gdn_chunk(st, *a), s1,
                          (to_chunks(q), to_chunks(k), to_chunks(v), to_chunks(g), to_chunks(beta)))
    o_real = jnp.moveaxis(o_c, 0, 1).reshape((b, n_chunks * GDN_CHUNK) + o_c.shape[3:])
    return jnp.concatenate([o_meta, o_real], axis=1), s_fin


def _token_mixer(h, lp, fox_past, s5_x0, gdn_s0, conv_buf):
    f32 = jnp.float32
    b, length = h.shape[0], h.shape[1]
    (fq, fk, fv, f_logit, s5_u, gdn_qkv, gdn_a, gdn_b, gdn_z, gate_logits) = _split_cols(h @ lp['w_in'])

    q = fq.reshape(b, length, FOX_HEADS, FOX_HEAD_DIM)
    k = fk.reshape(b, length, FOX_HEADS, FOX_HEAD_DIM)
    v = fv.reshape(b, length, FOX_HEADS, FOX_HEAD_DIM)
    logf = jax.nn.log_sigmoid(f_logit.astype(f32) + lp['fox_b_f'].astype(f32))
    if fox_past is None:
        fox_o = _fox_prompt(q, k, v, jnp.cumsum(logf, axis=1))
    else:
        pk, pv, plogf = fox_past
        past_len = pk.shape[1]
        k_all = jnp.concatenate([pk, k], axis=1)
        v_all = jnp.concatenate([pv, v], axis=1)
        f_all = jnp.cumsum(jnp.concatenate([plogf.astype(f32), logf], axis=1), axis=1)
        fox_o = _fox_attend(q, k_all, v_all, f_all[:, past_len:], f_all, past_len + jnp.arange(length))
    fox_br = fox_o.reshape(b, length, FOX_WIDTH).astype(h.dtype) @ lp['w_fox_br']

    s5_y, s5_re, s5_im = _s5_scan(s5_u, s5_x0[0], s5_x0[1], lp['s5_A_re'], lp['s5_A_im'],
                                  lp['s5_B_re'], lp['s5_B_im'], lp['s5_C_re'], lp['s5_C_im'],
                                  lp['s5_D'], lp['s5_log_dt'])
    z5 = jax.nn.gelu(s5_y).astype(h.dtype)
    glu_a, glu_b = jnp.split(z5 @ lp['w_s5_glu'], 2, axis=-1)
    s5_br = glu_a * jax.nn.sigmoid(glu_b)

    qkv_c, conv_new = _short_conv(gdn_qkv, conv_buf, lp['gdn_conv_w'])
    gq, gk, gv = jnp.split(qkv_c.astype(f32), 3, axis=-1)
    gq = _l2norm(gq.reshape(b, length, GDN_HEADS, GDN_HEAD_DIM)) * (GDN_HEAD_DIM ** -0.5)
    gk = _l2norm(gk.reshape(b, length, GDN_HEADS, GDN_HEAD_DIM))
    gv = gv.reshape(b, length, GDN_HEADS, GDN_HEAD_DIM)
    g = -jnp.exp(lp['gdn_A_log'].astype(f32)) * jax.nn.softplus(gdn_a.astype(f32) + lp['gdn_dt_bias'].astype(f32))
    beta = jax.nn.sigmoid(gdn_b.astype(f32))
    s0 = gdn_s0.astype(f32)
    if fox_past is None:
        o_g, gdn_s = _gdn_prompt(s0, gq, gk, gv, g, beta)
    else:
        gdn_s, o_g = _gdn_chunk(s0, gq, gk, gv, g, beta)
    o_g = o_g * lax.rsqrt(jnp.mean(o_g * o_g, axis=-1, keepdims=True) + RMS_EPS) * lp['gdn_norm'].astype(f32)
    o_g = o_g * jax.nn.silu(gdn_z.astype(f32).reshape(b, length, GDN_HEADS, GDN_HEAD_DIM))
    gdn_br = o_g.reshape(b, length, GDN_WIDTH).astype(h.dtype) @ lp['w_gdn_br']

    gate = jax.nn.sigmoid(gate_logits.astype(f32)).astype(h.dtype).reshape(b, length, N_BRANCH, D_MODEL)
    merged = gate[:, :, 0] * fox_br + gate[:, :, 1] * s5_br + gate[:, :, 2] * gdn_br
    out = merged @ lp['w_out']
    return out, (k, v, logf, s5_re, s5_im, gdn_s, conv_new)


def setup_inputs(seed: int = 0) -> dict:
    key = jax.random.key(seed)
    ks = iter(jax.random.split(key, 48))
    f32 = jnp.float32

    def nrm(shape, scale=1.0):
        return scale * jax.random.normal(next(ks), shape, f32)

    def unif(shape, lo, hi):
        return jax.random.uniform(next(ks), shape, f32, lo, hi)

    n_pages = PAST_LEN // PAGE_SIZE
    n_used = DEC_BATCH * n_pages
    n_pool = n_used + n_used // 4
    page_table = jax.random.permutation(next(ks), n_pool)[:n_used].reshape(DEC_BATCH, n_pages).astype(jnp.int32)

    x_prompt = nrm((BATCH, SEQ, D_MODEL))
    x_sample = nrm((DEC_BATCH, DEC_SEQ, D_MODEL))
    cache_fox_k = nrm((DEPTH, n_pool, PAGE_SIZE, FOX_HEADS, FOX_HEAD_DIM))
    cache_fox_v = nrm((DEPTH, n_pool, PAGE_SIZE, FOX_HEADS, FOX_HEAD_DIM))
    cache_fox_logf = jax.nn.log_sigmoid(4.0 + nrm((DEPTH, n_pool, PAGE_SIZE, FOX_HEADS)))
    state_s5_re = nrm((DEPTH, DEC_BATCH, S5_GROUPS, S5_STATE), 0.5)
    state_s5_im = nrm((DEPTH, DEC_BATCH, S5_GROUPS, S5_STATE), 0.5)
    state_gdn = nrm((DEPTH, DEC_BATCH, GDN_HEADS, GDN_HEAD_DIM, GDN_HEAD_DIM), 0.1)
    state_gdn_conv = nrm((DEPTH, DEC_BATCH, GDN_CONV - 1, 3 * GDN_WIDTH))

    meta_tokens = nrm((N_META, D_MODEL))
    norm_ffn1 = 1.0 + nrm((DEPTH, D_MODEL), 0.02)
    w_ffn1_in = nrm((DEPTH, D_MODEL, 2 * D_FF), D_MODEL ** -0.5)
    w_ffn1_out = nrm((DEPTH, D_FF, D_MODEL), D_FF ** -0.5)
    norm_mix = 1.0 + nrm((DEPTH, D_MODEL), 0.02)
    w_in = nrm((DEPTH, D_MODEL, D_IN), D_MODEL ** -0.5)
    fox_b_f = 4.0 + nrm((DEPTH, FOX_HEADS), 1.0)
    w_fox_br = nrm((DEPTH, FOX_WIDTH, D_MODEL), FOX_WIDTH ** -0.5)
    s5_A_re = -0.5 + nrm((DEPTH, S5_GROUPS, S5_STATE), 0.01)
    s5_A_im = jnp.broadcast_to(jnp.pi * jnp.arange(S5_STATE, dtype=f32), (DEPTH, S5_GROUPS, S5_STATE))
    s5_B_re = nrm((DEPTH, S5_GROUPS, S5_STATE, S5_GROUP_SIZE), (2 * S5_GROUP_SIZE) ** -0.5)
    s5_B_im = nrm((DEPTH, S5_GROUPS, S5_STATE, S5_GROUP_SIZE), (2 * S5_GROUP_SIZE) ** -0.5)
    s5_C_re = nrm((DEPTH, S5_GROUPS, S5_GROUP_SIZE, S5_STATE), (2 * S5_STATE) ** -0.5)
    s5_C_im = nrm((DEPTH, S5_GROUPS, S5_GROUP_SIZE, S5_STATE), (2 * S5_STATE) ** -0.5)
    s5_D = nrm((DEPTH, S5_GROUPS, S5_GROUP_SIZE), 0.5)
    s5_log_dt = unif((DEPTH, S5_GROUPS), math.log(0.001), math.log(0.1))
    w_s5_glu = nrm((DEPTH, S5_WIDTH, 2 * D_MODEL), S5_WIDTH ** -0.5)
    gdn_conv_w = nrm((DEPTH, GDN_CONV, 3 * GDN_WIDTH), GDN_CONV ** -0.5)
    gdn_A_log = jnp.log(unif((DEPTH, GDN_HEADS), 1.0, 16.0))
    dt = jnp.exp(unif((DEPTH, GDN_HEADS), math.log(0.001), math.log(0.1)))
    gdn_dt_bias = dt + jnp.log(-jnp.expm1(-dt))
    gdn_norm = 1.0 + nrm((DEPTH, GDN_HEAD_DIM), 0.02)
    w_gdn_br = nrm((DEPTH, GDN_WIDTH, D_MODEL), GDN_WIDTH ** -0.5)
    w_out = nrm((DEPTH, D_MODEL, D_MODEL), D_MODEL ** -0.5)
    norm_ffn2 = 1.0 + nrm((DEPTH, D_MODEL), 0.02)
    w_ffn2_in = nrm((DEPTH, D_MODEL, 2 * D_FF), D_MODEL ** -0.5)
    w_ffn2_out = nrm((DEPTH, D_FF, D_MODEL), D_FF ** -0.5)
    norm_final = 1.0 + nrm((D_MODEL,), 0.02)

    return {'x_prompt': x_prompt, 'x_sample': x_sample,
            'cache_fox_k': cache_fox_k, 'cache_fox_v': cache_fox_v, 'cache_fox_logf': cache_fox_logf,
            'state_s5_re': state_s5_re, 'state_s5_im': state_s5_im,
            'state_gdn': state_gdn, 'state_gdn_conv': state_gdn_conv, 'page_table': page_table,
            'meta_tokens': meta_tokens, 'norm_ffn1': norm_ffn1, 'w_ffn1_in': w_ffn1_in, 'w_ffn1_out': w_ffn1_out,
            'norm_mix': norm_mix, 'w_in': w_in, 'fox_b_f': fox_b_f, 'w_fox_br': w_fox_br,
            's5_A_re': s5_A_re, 's5_A_im': s5_A_im, 's5_B_re': s5_B_re, 's5_B_im': s5_B_im,
            's5_C_re': s5_C_re, 's5_C_im': s5_C_im, 's5_D': s5_D, 's5_log_dt': s5_log_dt, 'w_s5_glu': w_s5_glu,
            'gdn_conv_w': gdn_conv_w, 'gdn_A_log': gdn_A_log, 'gdn_dt_bias': gdn_dt_bias, 'gdn_norm': gdn_norm,
            'w_gdn_br': w_gdn_br, 'w_out': w_out, 'norm_ffn2': norm_ffn2, 'w_ffn2_in': w_ffn2_in,
            'w_ffn2_out': w_ffn2_out, 'norm_final': norm_final}


def reference(x_prompt, x_sample, cache_fox_k, cache_fox_v, cache_fox_logf, state_s5_re, state_s5_im,
              state_gdn, state_gdn_conv, page_table, meta_tokens, norm_ffn1, w_ffn1_in, w_ffn1_out,
              norm_mix, w_in, fox_b_f, w_fox_br, s5_A_re, s5_A_im, s5_B_re, s5_B_im, s5_C_re, s5_C_im,
              s5_D, s5_log_dt, w_s5_glu, gdn_conv_w, gdn_A_log, gdn_dt_bias, gdn_norm, w_gdn_br, w_out,
              norm_ffn2, w_ffn2_in, w_ffn2_out, norm_final):
    f32 = jnp.float32
    b_p = x_prompt.shape[0]
    b_s = x_sample.shape[0]
    meta = jnp.broadcast_to(meta_tokens[None].astype(x_prompt.dtype), (b_p, N_META, D_MODEL))
    xp = jnp.concatenate([meta, x_prompt], axis=1)
    xs = x_sample
    zeros_s5 = jnp.zeros((b_p, S5_GROUPS, S5_STATE), f32)
    zeros_gdn = jnp.zeros((b_p, GDN_HEADS, GDN_HEAD_DIM, GDN_HEAD_DIM), f32)
    zeros_conv = jnp.zeros((b_p, GDN_CONV - 1, 3 * GDN_WIDTH), x_prompt.dtype)
    new_p = [[] for _ in range(7)]
    new_s = [[] for _ in range(7)]

    for l in range(DEPTH):
        lp = {'w_in': w_in[l], 'fox_b_f': fox_b_f[l], 'w_fox_br': w_fox_br[l],
              's5_A_re': s5_A_re[l], 's5_A_im': s5_A_im[l], 's5_B_re': s5_B_re[l], 's5_B_im': s5_B_im[l],
              's5_C_re': s5_C_re[l], 's5_C_im': s5_C_im[l], 's5_D': s5_D[l], 's5_log_dt': s5_log_dt[l],
              'w_s5_glu': w_s5_glu[l], 'gdn_conv_w': gdn_conv_w[l], 'gdn_A_log': gdn_A_log[l],
              'gdn_dt_bias': gdn_dt_bias[l], 'gdn_norm': gdn_norm[l], 'w_gdn_br': w_gdn_br[l], 'w_out': w_out[l]}
        xp = xp + 0.5 * _swiglu(_rmsnorm(xp, norm_ffn1[l]), w_ffn1_in[l], w_ffn1_out[l])
        xs = xs + 0.5 * _swiglu(_rmsnorm(xs, norm_ffn1[l]), w_ffn1_in[l], w_ffn1_out[l])
        mp, st_p = _token_mixer(_rmsnorm(xp, norm_mix[l]), lp, None, (zeros_s5, zeros_s5), zeros_gdn, zeros_conv)
        fox_past = (cache_fox_k[l][page_table].reshape((b_s, -1) + cache_fox_k.shape[3:]),
                    cache_fox_v[l][page_table].reshape((b_s, -1) + cache_fox_v.shape[3:]),
                    cache_fox_logf[l][page_table].reshape((b_s, -1) + cache_fox_logf.shape[3:]))
        ms, st_s = _token_mixer(_rmsnorm(xs, norm_mix[l]), lp, fox_past,
                                (state_s5_re[l], state_s5_im[l]), state_gdn[l], state_gdn_conv[l])
        xp = xp + mp
        xs = xs + ms
        xp = xp + 0.5 * _swiglu(_rmsnorm(xp, norm_ffn2[l]), w_ffn2_in[l], w_ffn2_out[l])
        xs = xs + 0.5 * _swiglu(_rmsnorm(xs, norm_ffn2[l]), w_ffn2_in[l], w_ffn2_out[l])
        for i in range(7):
            new_p[i].append(st_p[i])
            new_s[i].append(st_s[i])

    y_prompt = _rmsnorm(xp, norm_final)[:, N_META:]
    y_sample = _rmsnorm(xs, norm_final)
    return (y_prompt, y_sample,
            jnp.stack(new_p[0]), jnp.stack(new_p[1]), jnp.stack(new_p[2]),
            jnp.stack(new_s[0]), jnp.stack(new_s[1]), jnp.stack(new_s[2]),
            jnp.stack(new_p[3]), jnp.stack(new_p[4]), jnp.stack(new_s[3]), jnp.stack(new_s[4]),
            jnp.stack(new_p[5]), jnp.stack(new_s[5]),
            jnp.stack(new_p[6]), jnp.stack(new_s[6]))
```

```python
import functools
import math

import jax
import jax.numpy as jnp
from jax import lax
from jax.experimental import pallas as pl
from jax.experimental.pallas import tpu as pltpu

F32 = jnp.float32
BF16 = jnp.bfloat16
RMS_EPS = 1e-6
NEG_BIG = -1e30
LANES = 128
VMEM_LIMIT = 56 * 1024 * 1024
S5_CHUNK = 16
GDN_CHUNK = 128
HI = lax.Precision.HIGHEST


def _cparams(sem):
    return pltpu.CompilerParams(dimension_semantics=sem, vmem_limit_bytes=VMEM_LIMIT)


def _resident(shape, index_map):
    return pl.BlockSpec(shape, index_map, pipeline_mode=pl.Buffered(1))


def _dot(a, b):
    return jnp.dot(a, b, preferred_element_type=F32)


def _dot_hi(a, b):
    return jnp.dot(a, b, preferred_element_type=F32, precision=HI)


def _dot_nt(a, b):
    return lax.dot_general(a, b, (((1,), (1,)), ((), ())), preferred_element_type=F32)


def _rms_bf16(x, w):
    ms = jnp.mean(x * x, axis=-1, keepdims=True)
    return (x * lax.rsqrt(ms + RMS_EPS) * w).astype(BF16)


def _sigmoid(x):
    return 1.0 / (1.0 + jnp.exp(-x))


def _row_tile(t, cap=1024):
    best = 8
    for tm in range(8, min(t, cap) + 1, 8):
        if t % tm == 0:
            best = tm
    return best


def _ffn_body(x_ref, nw_ref, wi_ref, wo_ref, o_ref, act_ref, *, d_ff, fc):
    x = x_ref[...]
    h = _rms_bf16(x, nw_ref[...])
    for c in range(d_ff // fc):
        g = _dot(h, wi_ref[:, c * fc:(c + 1) * fc])
        u = _dot(h, wi_ref[:, d_ff + c * fc:d_ff + (c + 1) * fc])
        act_ref[:, c * fc:(c + 1) * fc] = (g * _sigmoid(g) * u).astype(BF16)
    o_ref[...] = x + 0.5 * _dot(act_ref[...], wo_ref[...])


def _ffn(x, nw, wi, wo, layer, tm):
    t, d = x.shape
    d_ff = wo.shape[1]
    fc = 256 if d_ff % 256 == 0 else d_ff
    return pl.pallas_call(
        functools.partial(_ffn_body, d_ff=d_ff, fc=fc),
        grid=(t // tm,),
        in_specs=[pl.BlockSpec((tm, d), lambda i: (i, 0)),
                  _resident((None, 1, d), lambda i: (layer, 0, 0)),
                  _resident((None, d, 2 * d_ff), lambda i: (layer, 0, 0)),
                  _resident((None, d_ff, d), lambda i: (layer, 0, 0))],
        out_specs=pl.BlockSpec((tm, d), lambda i: (i, 0)),
        out_shape=jax.ShapeDtypeStruct((t, d), F32),
        scratch_shapes=[pltpu.VMEM((tm, d_ff), BF16)],
        compiler_params=_cparams(("arbitrary",)),
        name="ffn",
    )(x, nw, wi, wo)


def _inproj_body(x_ref, nw_ref, w_ref, h_ref, q_ref, k_ref, v_ref, kv_ref, u_ref, gq_ref, z_ref, sm_ref,
                 *, fw, sw, gw, q_scale):
    h = _rms_bf16(x_ref[...], nw_ref[...])
    h_ref[...] = h
    o = 0
    q = _dot(h, w_ref[:, o:o + fw]); o += fw
    q_ref[...] = (q * q_scale).astype(BF16)
    kv = _dot(h, w_ref[:, o:o + 2 * fw]); o += 2 * fw
    kv_ref[...] = kv
    k_ref[...] = kv[:, :fw].astype(BF16)
    v_ref[...] = kv[:, fw:].astype(BF16)
    u_ref[...] = _dot(h, w_ref[:, o:o + sw]); o += sw
    gq_ref[...] = _dot(h, w_ref[:, o:o + 3 * gw]); o += 3 * gw
    z_ref[...] = _dot(h, w_ref[:, o:o + gw]); o += gw
    sm_ref[...] = _dot(h, w_ref[:, o:o + LANES])


def _inproj(x, nw, w, layer, tm, fw, sw, gw, q_scale):
    t, d = x.shape
    n = w.shape[2]
    row = lambda width: pl.BlockSpec((tm, width), lambda i: (i, 0))
    widths = (d, fw, fw, fw, 2 * fw, sw, 3 * gw, gw, LANES)
    dts = (BF16, BF16, BF16, BF16, F32, F32, F32, F32, F32)
    return pl.pallas_call(
        functools.partial(_inproj_body, fw=fw, sw=sw, gw=gw, q_scale=q_scale),
        grid=(t // tm,),
        in_specs=[row(d), _resident((None, 1, d), lambda i: (layer, 0, 0)),
                  _resident((None, d, n), lambda i: (layer, 0, 0))],
        out_specs=[row(wd) for wd in widths],
        out_shape=[jax.ShapeDtypeStruct((t, wd), dt) for wd, dt in zip(widths, dts)],
        compiler_params=_cparams(("arbitrary",)),
        name="inproj",
    )(x, nw, w)


def _log_sigmoid(x):
    return jnp.minimum(x, 0.0) - jnp.log(1.0 + jnp.exp(-jnp.abs(x)))


def _fcum_body(lg_ref, b_ref, lf_ref, cum_ref, *, nblk):
    r = lax.broadcasted_iota(jnp.int32, (LANES, LANES), 0)
    c = lax.broadcasted_iota(jnp.int32, (LANES, LANES), 1)
    upper = (r <= c).astype(F32)
    carry = jnp.zeros((lg_ref.shape[0], 1), F32)
    for i in range(nblk):
        sl = slice(i * LANES, (i + 1) * LANES)
        lf = _log_sigmoid(lg_ref[:, sl] + b_ref[...])
        lf_ref[:, sl] = lf
        cs = jnp.dot(lf, upper, preferred_element_type=F32, precision=HI) + carry
        cum_ref[:, sl] = cs
        carry = cs[:, LANES - 1:LANES]


def _fcum(logits, bias):
    ns, nh, length = logits.shape
    blk = pl.BlockSpec((None, nh, length), lambda s: (s, 0, 0))
    return pl.pallas_call(
        functools.partial(_fcum_body, nblk=length // LANES),
        grid=(ns,),
        in_specs=[blk, pl.BlockSpec((nh, 1), lambda s: (0, 0))],
        out_specs=[blk, blk],
        out_shape=[jax.ShapeDtypeStruct(logits.shape, F32)] * 2,
        compiler_params=_cparams(("arbitrary",)),
        name="fcum",
    )(logits, bias)


def _fox_prompt_body(q_ref, k_ref, v_ref, fq_ref, fk_ref, o_ref, m_sc, l_sc, acc_sc, *, tq, tk, pad, hd, n_heads):
    qi = pl.program_id(1)
    row = qi * tq + lax.broadcasted_iota(jnp.int32, (tq, 1), 0)
    lane = lax.broadcasted_iota(jnp.int32, (1, LANES), 1)
    lo = lane < hd
    n_kv = (qi * tq + tq + tk - 1) // tk
    for p in range(n_heads * hd // LANES):
        qp = q_ref[:, p * LANES:(p + 1) * LANES]
        qm = (jnp.where(lo, qp, jnp.zeros_like(qp)), jnp.where(lo, jnp.zeros_like(qp), qp))
        m_sc[...] = jnp.full(m_sc.shape, NEG_BIG, F32)
        l_sc[...] = jnp.zeros(l_sc.shape, F32)
        acc_sc[...] = jnp.zeros(acc_sc.shape, F32)

        def kv_step(j, _, p=p, qm=qm):
            start = pl.multiple_of(j * tk, tk)
            kb = k_ref[pl.ds(start, tk), p * LANES:(p + 1) * LANES]
            vb = v_ref[pl.ds(start, tk), p * LANES:(p + 1) * LANES]
            fkb = fk_ref[:, pl.ds(start, tk)]
            col = start + lax.broadcasted_iota(jnp.int32, (1, tk), 1)
            valid = (col <= row) & ((col >= pad) | (row < pad))
            pv, alpha = [], []
            for hh in range(2):
                h = 2 * p + hh
                s = _dot_nt(qm[hh], kb) + (fq_ref[:, h:h + 1] - fkb[h:h + 1, :])
                s = jnp.where(valid, s, NEG_BIG)
                m_old = m_sc[hh]
                m_new = jnp.maximum(m_old, jnp.max(s, axis=-1, keepdims=True))
                a = jnp.exp(m_old - m_new)
                e = jnp.exp(s - m_new)
                l_sc[hh] = a * l_sc[hh] + jnp.sum(e, axis=-1, keepdims=True)
                m_sc[hh] = m_new
                pv.append(_dot(e.astype(BF16), vb))
                alpha.append(a)
            acc_sc[...] = jnp.where(lo, alpha[0] * acc_sc[...] + pv[0], alpha[1] * acc_sc[...] + pv[1])
            return 0

        lax.fori_loop(0, n_kv, kv_step, 0)
        inv = jnp.where(lo, 1.0 / l_sc[0], 1.0 / l_sc[1])
        o_ref[:, p * LANES:(p + 1) * LANES] = (acc_sc[...] * inv).astype(BF16)


def _fox_prompt(q, k, v, fcum_t, fcum, nb, lp, pad, n_heads, hd, tq, tk):
    fw = n_heads * hd
    nq = lp // tq
    return pl.pallas_call(
        functools.partial(_fox_prompt_body, tq=tq, tk=tk, pad=pad, hd=hd, n_heads=n_heads),
        grid=(nb, nq),
        in_specs=[pl.BlockSpec((tq, fw), lambda b, i: (b * nq + i, 0)),
                  pl.BlockSpec((lp, fw), lambda b, i: (b, 0)),
                  pl.BlockSpec((lp, fw), lambda b, i: (b, 0)),
                  pl.BlockSpec((tq, n_heads), lambda b, i: (b * nq + i, 0)),
                  pl.BlockSpec((None, n_heads, lp), lambda b, i: (b, 0, 0))],
        out_specs=pl.BlockSpec((tq, fw), lambda b, i: (b * nq + i, 0)),
        out_shape=jax.ShapeDtypeStruct((nb * lp, fw), BF16),
        scratch_shapes=[pltpu.VMEM((2, tq, 1), F32), pltpu.VMEM((2, tq, 1), F32), pltpu.VMEM((tq, LANES), F32)],
        compiler_params=_cparams(("arbitrary", "arbitrary")),
        name="fox_prompt",
    )(q, k, v, fcum_t, fcum)


def _fox_sample_body(pt_ref, qbd_ref, cn_ref, kn_ref, vn_ref, *rest, g_pages, n_q, n_heads, hd, page):
    kt_refs = rest[:g_pages]
    vt_refs = rest[g_pages:2 * g_pages]
    lf_refs = rest[2 * g_pages:3 * g_pages]
    o_ref, m_sc, l_sc, acc_sc, car_sc = rest[3 * g_pages:]
    del pt_ref
    g = pl.program_id(1)
    nr = n_q * n_heads
    qbd = qbd_ref[...]
    rr = lax.broadcasted_iota(jnp.int32, (nr, 1), 0)
    qidx = rr // n_heads
    lane = lax.broadcasted_iota(jnp.int32, (1, LANES), 1)
    col_new = jnp.concatenate([cn_ref[...]] * n_q, axis=0)
    row_new = jnp.sum(jnp.where(lane == qidx, col_new, 0.0), axis=-1, keepdims=True)

    def update(s, pv_of):
        m_old = m_sc[...]
        m_new = jnp.maximum(m_old, jnp.max(s, axis=-1, keepdims=True))
        a = jnp.exp(m_old - m_new)
        e = jnp.exp(s - m_new)
        l_sc[...] = a * l_sc[...] + jnp.sum(e, axis=-1, keepdims=True)
        m_sc[...] = m_new
        acc_sc[...] = a * acc_sc[...] + pv_of(e.astype(BF16))

    @pl.when(g == 0)
    def _():
        m_sc[...] = jnp.full(m_sc.shape, NEG_BIG, F32)
        l_sc[...] = jnp.zeros(l_sc.shape, F32)
        acc_sc[...] = jnp.zeros(acc_sc.shape, F32)
        car_sc[...] = jnp.zeros(car_sc.shape, F32)
        s = _dot_nt(qbd, kn_ref[...]) + (row_new - col_new)
        s = jnp.where((lane <= qidx) & (lane < n_q), s, NEG_BIG)
        update(s, lambda e: _dot(e, vn_ref[...]))

    r = lax.broadcasted_iota(jnp.int32, (page, page), 0)
    c = lax.broadcasted_iota(jnp.int32, (page, page), 1)
    later = (r > c).astype(F32)
    for i in range(g_pages):
        lf = lf_refs[i][...]
        suffix = jnp.dot(lf, later, preferred_element_type=F32, precision=HI) + car_sc[...]
        bias = jnp.concatenate([suffix] * n_q, axis=0) + row_new
        kt = kt_refs[i][...].reshape(n_heads * hd, page).astype(BF16)
        vt = vt_refs[i][...].reshape(n_heads * hd, page).astype(BF16)
        update(_dot(qbd, kt) + bias, lambda e, vt=vt: _dot_nt(e, vt))
        car_sc[...] = car_sc[...] + jnp.sum(lf, axis=-1, keepdims=True)

    @pl.when(g == pl.num_programs(1) - 1)
    def _():
        o = acc_sc[...] / l_sc[...]
        lane_w = lax.broadcasted_iota(jnp.int32, (1, n_heads * hd), 1)
        o = jnp.where(lane_w // hd == rr % n_heads, o, 0.0)
        o_ref[...] = jnp.sum(o.reshape(n_q, n_heads, n_heads * hd), axis=1).astype(BF16)


def _fox_sample(page_table, qbd, cum_new, k_new, v_new, kt, vt, lft, layer, g_pages, n_q):
    nb, n_pages = page_table.shape
    _, _, n_heads, hd, page = kt.shape
    fw = n_heads * hd
    nr = n_q * n_heads
    n_groups = n_pages // g_pages

    def page_map(i):
        return lambda b, g, pt: (layer, pt[b, n_pages - 1 - (g * g_pages + i)], 0, 0, 0)

    def lf_map(i):
        return lambda b, g, pt: (layer, pt[b, n_pages - 1 - (g * g_pages + i)], 0, 0)

    per_b = lambda shape: pl.BlockSpec((None,) + shape, lambda b, g, pt: (b, 0, 0))
    in_specs = ([per_b((nr, fw)), per_b((n_heads, LANES)), per_b((LANES, fw)), per_b((LANES, fw))]
                + [pl.BlockSpec((None, None, n_heads, hd, page), page_map(i)) for i in range(g_pages)]
                + [pl.BlockSpec((None, None, n_heads, hd, page), page_map(i)) for i in range(g_pages)]
                + [pl.BlockSpec((None, None, n_heads, page), lf_map(i)) for i in range(g_pages)])
    return pl.pallas_call(
        functools.partial(_fox_sample_body, g_pages=g_pages, n_q=n_q, n_heads=n_heads, hd=hd, page=page),
        grid_spec=pltpu.PrefetchScalarGridSpec(
            num_scalar_prefetch=1, grid=(nb, n_groups), in_specs=in_specs,
            out_specs=per_b((n_q, fw)),
            scratch_shapes=[pltpu.VMEM((nr, 1), F32), pltpu.VMEM((nr, 1), F32), pltpu.VMEM((nr, fw), F32),
                            pltpu.VMEM((n_heads, 1), F32)]),
        out_shape=jax.ShapeDtypeStruct((nb, n_q, fw), BF16),
        compiler_params=_cparams(("arbitrary", "arbitrary")),
        name="fox_sample",
    )(page_table, qbd, cum_new, k_new, v_new, *([kt] * g_pages), *([vt] * g_pages), *([lft] * g_pages))


def _gelu_tanh(x):
    return 0.5 * x * (1.0 + jnp.tanh(math.sqrt(2.0 / math.pi) * (x + 0.044715 * x * x * x)))


def _s5_tables(a_re, a_im, b_re, b_im, c_re, c_im, d_skip, log_dt, n_last):
    ck = S5_CHUNK
    g, p, gs = b_re.shape
    dt = jnp.exp(log_dt)[:, None]
    mag_l, ang = a_re * dt, a_im * dt

    def apow_fn(n):
        m = jnp.exp(mag_l * n)
        return m * jnp.cos(ang * n), m * jnp.sin(ang * n)

    ab_re, ab_im = apow_fn(1.0)
    den = a_re * a_re + a_im * a_im
    cf_re = ((ab_re - 1.0) * a_re + ab_im * a_im) / den
    cf_im = (ab_im * a_re - (ab_re - 1.0) * a_im) / den
    bb_re = cf_re[..., None] * b_re - cf_im[..., None] * b_im
    bb_im = cf_re[..., None] * b_im + cf_im[..., None] * b_re
    lags = jnp.arange(ck + 1, dtype=F32)
    pw_re, pw_im = jax.vmap(apow_fn)(lags)
    cab_re = c_re[None] * pw_re[:, :, None, :] - c_im[None] * pw_im[:, :, None, :]
    cab_im = c_re[None] * pw_im[:, :, None, :] + c_im[None] * pw_re[:, :, None, :]
    w = (jnp.einsum('lgcp,gpd->lgcd', cab_re[:ck], bb_re, precision=HI)
         - jnp.einsum('lgcp,gpd->lgcd', cab_im[:ck], bb_im, precision=HI))
    s_i = jnp.arange(ck)[:, None]
    t_i = jnp.arange(ck)[None, :]
    lag = jnp.clip(t_i - s_i, 0, ck - 1)
    kfull = jnp.where((t_i >= s_i)[None, :, None, :, None],
                      jnp.transpose(w[lag], (2, 0, 4, 1, 3)), 0.0)
    eye = (s_i == t_i)[None, :, None, :, None] & (jnp.arange(gs)[:, None, None] == jnp.arange(gs)[None, None, :])[None, None]
    kfull = kfull + jnp.where(eye, d_skip[:, None, None, None, :], 0.0)
    kfull = kfull.reshape(g, ck * gs, ck * gs)

    def state_map(n_steps):
        e = jnp.clip(n_steps - 1 - jnp.arange(ck), 0, ck)
        live = (jnp.arange(ck) < n_steps)[:, None, None, None]
        f_re = jnp.where(live, pw_re[e][..., None] * bb_re[None] - pw_im[e][..., None] * bb_im[None], 0.0)
        f_im = jnp.where(live, pw_re[e][..., None] * bb_im[None] + pw_im[e][..., None] * bb_re[None], 0.0)
        to_rows = lambda f: jnp.transpose(f, (1, 0, 3, 2)).reshape(g, ck * gs, p)
        return to_rows(f_re), to_rows(f_im)

    e_re = jnp.transpose(cab_re[1:], (1, 3, 0, 2)).reshape(g, p, ck * gs)
    e_im = -jnp.transpose(cab_im[1:], (1, 3, 0, 2)).reshape(g, p, ck * gs)

    npair = g // 2
    z = jnp.zeros_like

    def pair_bd(m):
        m = m.reshape(npair, 2, m.shape[1], m.shape[2])
        top = jnp.concatenate([m[:, 0], z(m[:, 0])], axis=2)
        bot = jnp.concatenate([z(m[:, 1]), m[:, 1]], axis=2)
        return jnp.concatenate([top, bot], axis=1)

    def pair_f(f_re, f_im):
        fr = f_re.reshape(npair, 2, f_re.shape[1], p)
        fi = f_im.reshape(npair, 2, f_im.shape[1], p)
        top = jnp.concatenate([fr[:, 0], z(fr[:, 0]), fi[:, 0], z(fi[:, 0])], axis=2)
        bot = jnp.concatenate([z(fr[:, 1]), fr[:, 1], z(fi[:, 1]), fi[:, 1]], axis=2)
        return jnp.concatenate([top, bot], axis=1)

    def pair_e(m_re, m_im):
        er = m_re.reshape(npair, 2, p, m_re.shape[2])
        ei = m_im.reshape(npair, 2, p, m_im.shape[2])
        rows = [jnp.concatenate([er[:, 0], z(er[:, 0])], axis=2), jnp.concatenate([z(er[:, 1]), er[:, 1]], axis=2),
                jnp.concatenate([ei[:, 0], z(ei[:, 0])], axis=2), jnp.concatenate([z(ei[:, 1]), ei[:, 1]], axis=2)]
        return jnp.concatenate(rows, axis=1)

    pair_vec = lambda v: v.reshape(npair, 2 * p)
    nlev = 16
    lev = [apow_fn(float(ck * 2 ** i)) for i in range(nlev)]
    apow = jnp.stack([jnp.stack([pair_vec(r), pair_vec(i)], axis=1) for r, i in lev], axis=1)
    al_re, al_im = apow_fn(float(n_last))
    alast = jnp.stack([pair_vec(al_re), pair_vec(al_im)], axis=1)
    return (pair_bd(kfull).astype(BF16), pair_f(*state_map(ck)).astype(BF16), pair_f(*state_map(n_last)).astype(BF16),
            pair_e(e_re, e_im).astype(BF16), apow, alast)


def _s5_local_body(u_ref, k_ref, f_ref, y_ref, z_ref):
    u = u_ref[...]
    y_ref[...] = _dot(u, k_ref[...])
    z_ref[...] = _dot(u, f_ref[...])


def _s5_local(u, kmat, fmat):
    npair, n, w = u.shape
    sw = fmat.shape[2]
    return pl.pallas_call(
        _s5_local_body,
        grid=(npair,),
        in_specs=[pl.BlockSpec((None, n, w), lambda i: (i, 0, 0)),
                  pl.BlockSpec((None, w, w), lambda i: (i, 0, 0)),
                  pl.BlockSpec((None, w, sw), lambda i: (i, 0, 0))],
        out_specs=[pl.BlockSpec((None, n, w), lambda i: (i, 0, 0)), pl.BlockSpec((None, n, sw), lambda i: (i, 0, 0))],
        out_shape=[jax.ShapeDtypeStruct((npair, n, w), F32), jax.ShapeDtypeStruct((npair, n, sw), F32)],
        compiler_params=_cparams(("arbitrary",)),
        name="s5_local",
    )(u, kmat, fmat)


def _s5_scan_body(z_ref, ap_ref, xs_ref, xf_ref, *, nc, half):
    xr = z_ref[:, :half]
    xi = z_ref[:, half:]
    row = lax.broadcasted_iota(jnp.int32, (nc, 1), 0)
    lev, s = 0, 1
    while s < nc:
        ar = ap_ref[lev, 0:1, :]
        ai = ap_ref[lev, 1:2, :]
        keep = row >= s
        sr = jnp.where(keep, pltpu.roll(xr, s, 0), 0.0)
        si = jnp.where(keep, pltpu.roll(xi, s, 0), 0.0)
        xr, xi = xr + ar * sr - ai * si, xi + ar * si + ai * sr
        lev, s = lev + 1, s * 2
    xf_ref[:, :half] = xr[nc - 1:nc, :]
    xf_ref[:, half:] = xi[nc - 1:nc, :]
    xs_ref[:, :half] = jnp.where(row >= 1, pltpu.roll(xr, 1, 0), 0.0)
    xs_ref[:, half:] = jnp.where(row >= 1, pltpu.roll(xi, 1, 0), 0.0)


def _s5_scan(z, apow, nb):
    npair, n, sw = z.shape
    nc = n // nb
    nlev = apow.shape[1]
    return pl.pallas_call(
        functools.partial(_s5_scan_body, nc=nc, half=sw // 2),
        grid=(npair, nb),
        in_specs=[pl.BlockSpec((None, nc, sw), lambda i, b: (i, b, 0)),
                  pl.BlockSpec((None, nlev, 2, sw // 2), lambda i, b: (i, 0, 0, 0))],
        out_specs=[pl.BlockSpec((None, nc, sw), lambda i, b: (i, b, 0)),
                   pl.BlockSpec((None, None, 1, sw), lambda i, b: (i, b, 0, 0))],
        out_shape=[jax.ShapeDtypeStruct((npair, n, sw), F32), jax.ShapeDtypeStruct((npair, nb, 1, sw), F32)],
        compiler_params=_cparams(("arbitrary", "arbitrary")),
        name="s5_scan",
    )(z, apow)


def _s5_out_body(y_ref, xs_ref, e_ref, o_ref):
    y = y_ref[...] + _dot(xs_ref[...].astype(BF16), e_ref[...])
    o_ref[...] = _gelu_tanh(y).astype(BF16)


def _s5_out(ylocal, xstart, emat):
    npair, n, w = ylocal.shape
    sw = xstart.shape[2]
    return pl.pallas_call(
        _s5_out_body,
        grid=(npair,),
        in_specs=[pl.BlockSpec((None, n, w), lambda i: (i, 0, 0)),
                  pl.BlockSpec((None, n, sw), lambda i: (i, 0, 0)),
                  pl.BlockSpec((None, sw, w), lambda i: (i, 0, 0))],
        out_specs=pl.BlockSpec((None, n, w), lambda i: (i, 0, 0)),
        out_shape=jax.ShapeDtypeStruct((npair, n, w), BF16),
        compiler_params=_cparams(("arbitrary",)),
        name="s5_out",
    )(ylocal, xstart, emat)


def _s5_sample_body(u_ref, x0_ref, k_ref, f_ref, e_ref, al_ref, o_ref, xf_ref, *, half):
    u = u_ref[...]
    x0 = x0_ref[...]
    y = _dot(u, k_ref[...]) + _dot(x0.astype(BF16), e_ref[...])
    o_ref[...] = _gelu_tanh(y).astype(BF16)
    z = _dot(u, f_ref[...])
    ar, ai = al_ref[0:1, :], al_ref[1:2, :]
    xr, xi = x0[:, :half], x0[:, half:]
    xf_ref[:, :half] = z[:, :half] + ar * xr - ai * xi
    xf_ref[:, half:] = z[:, half:] + ar * xi + ai * xr


def _s5_sample(u, x0, kmat, flast, emat, alast):
    npair, n, w = u.shape
    sw = x0.shape[2]
    blk = lambda r, c: pl.BlockSpec((None, r, c), lambda i: (i, 0, 0))
    return pl.pallas_call(
        functools.partial(_s5_sample_body, half=sw // 2),
        grid=(npair,),
        in_specs=[blk(n, w), blk(n, sw), blk(w, w), blk(w, sw), blk(sw, w), blk(2, sw // 2)],
        out_specs=[blk(n, w), blk(n, sw)],
        out_shape=[jax.ShapeDtypeStruct((npair, n, w), BF16), jax.ShapeDtypeStruct((npair, n, sw), F32)],
        compiler_params=_cparams(("arbitrary",)),
        name="s5_sample",
    )(u, x0, kmat, flast, emat, alast)


def _gdn_body(x_ref, z_ref, sm_ref, ci_ref, s0_ref, cw_ref, na_ref, db_ref, nw_ref, o_ref, sf_ref, s_sc, car_sc,
              *, c, n_valid, n_heads, hd, kconv, a_lane, b_lane):
    ci = pl.program_id(1)
    gw = n_heads * hd

    @pl.when(ci == 0)
    def _():
        s_sc[...] = s0_ref[...]
        car_sc[...] = ci_ref[...]

    x = x_ref[...]
    xc = jnp.concatenate([car_sc[...], x], axis=0)
    hist = car_sc.shape[0]
    y = None
    for j in range(kconv):
        off = hist - (kconv - 1) + j
        term = xc[off:off + c, :] * cw_ref[j:j + 1, :]
        y = term if y is None else y + term
    car_sc[...] = x[c - hist:, :]
    qkv = y * _sigmoid(y)

    sm = sm_ref[...]
    sp_in = sm + db_ref[...]
    softplus = jnp.maximum(sp_in, 0.0) + jnp.log(1.0 + jnp.exp(-jnp.abs(sp_in)))
    gfull = na_ref[...] * softplus
    beta_full = _sigmoid(sm)
    row = ci * c + lax.broadcasted_iota(jnp.int32, (c, 1), 0)
    if n_valid is not None:
        live = row < n_valid
        gfull = jnp.where(live, gfull, 0.0)
        beta_full = jnp.where(live, beta_full, 0.0)
        qkv = jnp.where(live, qkv, 0.0)
    r = lax.broadcasted_iota(jnp.int32, (c, c), 0)
    cc = lax.broadcasted_iota(jnp.int32, (c, c), 1)
    incl = r >= cc
    strict = r > cc
    eye = (r == cc).astype(F32)
    cum_full = jnp.dot(incl.astype(F32), gfull, preferred_element_type=F32, precision=HI)
    cum_t = cum_full.T
    for h in range(n_heads):
        q = qkv[:, h * hd:(h + 1) * hd]
        k = qkv[:, gw + h * hd:gw + (h + 1) * hd]
        v = qkv[:, 2 * gw + h * hd:2 * gw + (h + 1) * hd]
        q = q * lax.rsqrt(jnp.sum(q * q, axis=-1, keepdims=True) + 1e-6) * (hd ** -0.5)
        k = k * lax.rsqrt(jnp.sum(k * k, axis=-1, keepdims=True) + 1e-6)
        cum = cum_full[:, a_lane + h:a_lane + h + 1]
        cum_r = cum_t[a_lane + h:a_lane + h + 1, :]
        beta = beta_full[:, b_lane + h:b_lane + h + 1]
        decay = jnp.where(incl, jnp.exp(jnp.where(incl, cum - cum_r, 0.0)), 0.0)
        kb = k * beta
        k16 = k.astype(BF16)
        lower = jnp.where(strict, _dot_nt(kb.astype(BF16), k16) * decay, 0.0)
        t_inv = eye - lower
        pw = _dot_hi(lower, lower)
        span = 2
        while span < c:
            t_inv = t_inv + _dot_hi(t_inv, pw)
            span *= 2
            if span < c:
                pw = _dot_hi(pw, pw)
        ecum = jnp.exp(cum)
        rhs = jnp.concatenate([v * beta, kb * ecum], axis=-1)
        sol = _dot_hi(t_inv, rhs)
        s_h = s_sc[h]
        s16 = s_h.astype(BF16)
        v_new = sol[:, :hd] - _dot(sol[:, hd:].astype(BF16), s16)
        vn16 = v_new.astype(BF16)
        aqk = jnp.where(incl, _dot_nt(q.astype(BF16), k16) * decay, 0.0)
        o = _dot((q * ecum).astype(BF16), s16) + _dot(aqk.astype(BF16), vn16)
        g_last = cum[c - 1:c, :]
        kd = (k * jnp.exp(g_last - cum)).T.astype(BF16)
        s_sc[h] = s_h * jnp.exp(g_last) + _dot(kd, vn16)
        o = o * lax.rsqrt(jnp.mean(o * o, axis=-1, keepdims=True) + RMS_EPS) * nw_ref[...]
        zz = z_ref[:, h * hd:(h + 1) * hd]
        o_ref[:, h * hd:(h + 1) * hd] = (o * (zz * _sigmoid(zz))).astype(BF16)

    @pl.when(ci == pl.num_programs(1) - 1)
    def _():
        sf_ref[...] = s_sc[...]


def _gdn(x, z, sm, conv_init, s0, conv_w, neg_a, dt_b, norm_w, ns, n_valid, a_lane, b_lane):
    rows, w3 = x.shape
    n_heads, hd = s0.shape[1], s0.shape[2]
    gw = n_heads * hd
    c = GDN_CHUNK
    nchunk = rows // ns // c
    kconv = conv_w.shape[0]
    hist = conv_init.shape[1]
    rowblk = lambda width: pl.BlockSpec((c, width), lambda s, i: (s * nchunk + i, 0))
    const = lambda shape: pl.BlockSpec(shape, lambda s, i: (0,) * len(shape))
    return pl.pallas_call(
        functools.partial(_gdn_body, c=c, n_valid=n_valid, n_heads=n_heads, hd=hd, kconv=kconv,
                          a_lane=a_lane, b_lane=b_lane),
        grid=(ns, nchunk),
        in_specs=[rowblk(w3), rowblk(gw), rowblk(LANES),
                  pl.BlockSpec((None, hist, w3), lambda s, i: (s, 0, 0)),
                  pl.BlockSpec((None, n_heads, hd, hd), lambda s, i: (s, 0, 0, 0)),
                  const((kconv, w3)), const((1, LANES)), const((1, LANES)), const((1, hd))],
        out_specs=[rowblk(gw), pl.BlockSpec((None, n_heads, hd, hd), lambda s, i: (s, 0, 0, 0))],
        out_shape=[jax.ShapeDtypeStruct((rows, gw), BF16), jax.ShapeDtypeStruct(s0.shape, F32)],
        scratch_shapes=[pltpu.VMEM((n_heads, hd, hd), F32), pltpu.VMEM((hist, w3), F32)],
        compiler_params=_cparams(("arbitrary", "arbitrary")),
        name="gdn",
    )(x, z, sm, conv_init, s0, conv_w, neg_a, dt_b, norm_w)


def _merge_body(x_ref, h_ref, of_ref, z5_ref, og_ref, wf_ref, w5_ref, wg_ref, wgate_ref, wo_ref, o_ref, *, d):
    fox_br = _dot(of_ref[...], wf_ref[...])
    glu = _dot(z5_ref[...], w5_ref[...])
    s5_br = glu[:, :d] * _sigmoid(glu[:, d:])
    gdn_br = _dot(og_ref[...], wg_ref[...])
    h = h_ref[...]
    merged = (_sigmoid(_dot(h, wgate_ref[:, :d])) * fox_br
              + _sigmoid(_dot(h, wgate_ref[:, d:2 * d])) * s5_br
              + _sigmoid(_dot(h, wgate_ref[:, 2 * d:])) * gdn_br)
    o_ref[...] = x_ref[...] + _dot(merged.astype(BF16), wo_ref[...])


def _merge(x, h, o_fox, z5, o_gdn, wf, w5, wg, wgate, wo, layer, tm):
    t, d = x.shape
    row = lambda a: pl.BlockSpec((tm, a.shape[1]), lambda i: (i, 0))
    res = lambda a: _resident((None,) + a.shape[1:], lambda i: (layer, 0, 0))
    return pl.pallas_call(
        functools.partial(_merge_body, d=d),
        grid=(t // tm,),
        in_specs=[row(x), row(h), row(o_fox), row(z5), row(o_gdn), res(wf), res(w5), res(wg), res(wgate), res(wo)],
        out_specs=pl.BlockSpec((tm, d), lambda i: (i, 0)),
        out_shape=jax.ShapeDtypeStruct((t, d), F32),
        compiler_params=_cparams(("arbitrary",)),
        name="merge",
    )(x, h, o_fox, z5, o_gdn, wf, w5, wg, wgate, wo)


def _final_norm_body(x_ref, w_ref, o_ref):
    x = x_ref[...]
    ms = jnp.mean(x * x, axis=-1, keepdims=True)
    o_ref[...] = x * lax.rsqrt(ms + RMS_EPS) * w_ref[...]


def _final_norm(x, w, tm):
    t, d = x.shape
    return pl.pallas_call(
        _final_norm_body,
        grid=(t // tm,),
        in_specs=[pl.BlockSpec((tm, d), lambda i: (i, 0)), pl.BlockSpec((1, d), lambda i: (0, 0))],
        out_specs=pl.BlockSpec((tm, d), lambda i: (i, 0)),
        out_shape=jax.ShapeDtypeStruct((t, d), F32),
        compiler_params=_cparams(("arbitrary",)),
        name="final_norm",
    )(x, w)


def kernel(x_prompt, x_sample, cache_fox_k, cache_fox_v, cache_fox_logf, state_s5_re, state_s5_im, state_gdn, state_gdn_conv, page_table, meta_tokens, norm_ffn1, w_ffn1_in, w_ffn1_out, norm_mix, w_in, fox_b_f, w_fox_br, s5_A_re, s5_A_im, s5_B_re, s5_B_im, s5_C_re, s5_C_im, s5_D, s5_log_dt, w_s5_glu, gdn_conv_w, gdn_A_log, gdn_dt_bias, gdn_norm, w_gdn_br, w_out, norm_ffn2, w_ffn2_in, w_ffn2_out, norm_final):
    nb, seq, d = x_prompt.shape
    ds_b, ds_t, _ = x_sample.shape
    depth = w_in.shape[0]
    n_meta = meta_tokens.shape[0]
    fh, fhd = cache_fox_k.shape[3], cache_fox_k.shape[4]
    fw = fh * fhd
    page = cache_fox_k.shape[2]
    sg, sp, sgs = s5_B_re.shape[1:]
    sw = sg * sgs
    gh, ghd = state_gdn.shape[2], state_gdn.shape[3]
    gw = gh * ghd
    kconv = gdn_conv_w.shape[1]
    ck = S5_CHUNK
    assert fw % LANES == 0 and 2 * fhd == LANES and ghd == LANES and sg % 2 == 0
    assert 2 * sp == LANES and ds_t <= ck and ds_t >= kconv - 1
    assert page == LANES and ds_t <= LANES and fh + 2 * gh <= LANES

    lreal = n_meta + seq
    pad = (-lreal) % LANES
    lp = lreal + pad
    tp = nb * lp
    ts = ds_b * ds_t
    t = tp + ts
    tm = _row_tile(t)
    tq = max(c for c in (128, 256, 384, 512) if lp % c == 0)
    n_pages = page_table.shape[1]
    g_pages = max(c for c in (1, 2, 4, 8) if n_pages % c == 0)
    hist = 8

    sizes = (fw, fw, fw, fh, sw, 3 * gw, gh, gh, gw, 3 * d)
    offs = [0]
    for s in sizes:
        offs.append(offs[-1] + s)
    col = lambda i: w_in[:, :, offs[i]:offs[i + 1]]
    small = jnp.concatenate([col(3), col(6), col(7),
                             jnp.zeros((depth, d, LANES - fh - 2 * gh), F32)], axis=2)
    a_lane, b_lane = fh, fh + gh
    w_proj = jnp.concatenate([col(0), col(1), col(2), col(4), col(5), col(8), small], axis=2).astype(BF16)
    w_gate = col(9).astype(BF16)
    wi1, wo1 = w_ffn1_in.astype(BF16), w_ffn1_out.astype(BF16)
    wi2, wo2 = w_ffn2_in.astype(BF16), w_ffn2_out.astype(BF16)
    wf16, w516, wg16, wo16 = (w.astype(BF16) for w in (w_fox_br, w_s5_glu, w_gdn_br, w_out))
    nrm = lambda w: w.reshape(depth, 1, d)
    n1, nm, n2 = nrm(norm_ffn1), nrm(norm_mix), nrm(norm_ffn2)
    lane_vec = lambda v, at: jnp.zeros((depth, 1, LANES), F32).at[:, 0, at:at + v.shape[1]].set(v)
    neg_a = lane_vec(-jnp.exp(gdn_A_log), a_lane)
    dt_b = lane_vec(gdn_dt_bias, a_lane)
    s5_tab = [_s5_tables(s5_A_re[l], s5_A_im[l], s5_B_re[l], s5_B_im[l], s5_C_re[l], s5_C_im[l], s5_D[l],
                         s5_log_dt[l], ds_t) for l in range(depth)]

    kt_cache = jnp.transpose(cache_fox_k, (0, 1, 3, 4, 2))
    vt_cache = jnp.transpose(cache_fox_v, (0, 1, 3, 4, 2))
    lf_cache = jnp.transpose(cache_fox_logf, (0, 1, 3, 2))

    meta = jnp.broadcast_to(meta_tokens[None].astype(F32), (nb, n_meta, d))
    xp = jnp.concatenate([jnp.zeros((nb, pad, d), F32), meta, x_prompt], axis=1).reshape(tp, d)
    x = jnp.concatenate([xp, x_sample.reshape(ts, d)], axis=0)

    zeros_conv = jnp.zeros((nb, hist, 3 * gw), F32)
    zeros_gdn = jnp.zeros((nb, gh, ghd, ghd), F32)
    npair = sg // 2
    nc = lp // ck
    outs = [[] for _ in range(14)]

    for l in range(depth):
        x = _ffn(x, n1, wi1, wo1, l, tm)
        h, q, k, v, kv32, u, gq, z, sm = _inproj(x, nm, w_proj, l, tm, fw, sw, gw, fhd ** -0.5)

        lg_p = jnp.transpose(sm[:tp, :fh].reshape(nb, lp, fh), (0, 2, 1))
        lf_p, cum_p = _fcum(lg_p, fox_b_f[l].reshape(fh, 1))
        cum_pt = jnp.transpose(cum_p, (0, 2, 1)).reshape(tp, fh)
        o_fox_p = _fox_prompt(q, k, v, cum_pt, cum_p, nb, lp, pad, fh, fhd, tq, tq)
        lg_s = jnp.transpose(sm[tp:, :fh].reshape(ds_b, ds_t, fh), (0, 2, 1))
        lg_s = jnp.pad(lg_s, ((0, 0), (0, 0), (0, LANES - ds_t)))
        lf_s, cum_s = _fcum(lg_s, fox_b_f[l].reshape(fh, 1))
        q_s = q[tp:].reshape(ds_b, ds_t, 1, fh, fhd)
        head_eye = jnp.eye(fh, dtype=BF16).reshape(1, 1, fh, fh, 1)
        qbd = (q_s * head_eye).reshape(ds_b, ds_t * fh, fw)
        rows_pad = ((0, 0), (0, LANES - ds_t), (0, 0))
        k_new = jnp.pad(k[tp:].reshape(ds_b, ds_t, fw), rows_pad)
        v_new = jnp.pad(v[tp:].reshape(ds_b, ds_t, fw), rows_pad)
        o_fox_s = _fox_sample(page_table, qbd, cum_s, k_new, v_new, kt_cache, vt_cache, lf_cache, l, g_pages, ds_t)
        o_fox = jnp.concatenate([o_fox_p, o_fox_s.reshape(ts, fw)], axis=0)

        kmat, fmat, flast, emat, apow, alast = s5_tab[l]
        u_p = u[:tp].astype(BF16).reshape(nb, nc, ck, npair, 2, sgs)
        u_p = jnp.transpose(u_p, (3, 0, 1, 4, 2, 5)).reshape(npair, nb * nc, 2 * ck * sgs)
        y_loc, z_loc = _s5_local(u_p, kmat, fmat)
        x_start, x_fin = _s5_scan(z_loc, apow, nb)
        z5_p = _s5_out(y_loc, x_start, emat)
        z5_p = jnp.transpose(z5_p.reshape(npair, nb, nc, 2, ck, sgs), (1, 2, 4, 0, 3, 5)).reshape(tp, sw)
        u_s = jnp.pad(u[tp:].astype(BF16).reshape(ds_b, ds_t, npair, 2, sgs), ((0, 0), (0, ck - ds_t), (0, 0), (0, 0), (0, 0)))
        u_s = jnp.transpose(u_s, (2, 0, 3, 1, 4)).reshape(npair, ds_b, 2 * ck * sgs)
        to_pair = lambda st: jnp.transpose(st.reshape(ds_b, npair, 2 * sp), (1, 0, 2))
        x0 = jnp.concatenate([to_pair(state_s5_re[l]), to_pair(state_s5_im[l])], axis=2)
        z5_s, xf_s = _s5_sample(u_s, x0, kmat, flast, emat, alast)
        z5_s = jnp.transpose(z5_s.reshape(npair, ds_b, 2, ck, sgs)[:, :, :, :ds_t], (1, 3, 0, 2, 4)).reshape(ts, sw)
        z5 = jnp.concatenate([z5_p, z5_s], axis=0)

        cw = gdn_conv_w[l]
        o_gdn_p, s_p = _gdn(gq[:tp], z[:tp], sm[:tp], zeros_conv, zeros_gdn, cw, neg_a[l], dt_b[l],
                            gdn_norm[l].reshape(1, ghd), nb, None, a_lane, b_lane)
        seq_pad = lambda a: jnp.pad(a.reshape(ds_b, ds_t, a.shape[1]), ((0, 0), (0, GDN_CHUNK - ds_t), (0, 0))
                                    ).reshape(ds_b * GDN_CHUNK, a.shape[1])
        conv_s = jnp.pad(state_gdn_conv[l], ((0, 0), (hist - (kconv - 1), 0), (0, 0)))
        o_gdn_s, s_s = _gdn(seq_pad(gq[tp:]), seq_pad(z[tp:]), seq_pad(sm[tp:]), conv_s, state_gdn[l], cw, neg_a[l],
                            dt_b[l], gdn_norm[l].reshape(1, ghd), ds_b, ds_t, a_lane, b_lane)
        o_gdn = jnp.concatenate([o_gdn_p, o_gdn_s.reshape(ds_b, GDN_CHUNK, gw)[:, :ds_t].reshape(ts, gw)], axis=0)

        x = _merge(x, h, o_fox, z5, o_gdn, wf16, w516, wg16, w_gate, wo16, l, tm)
        x = _ffn(x, n2, wi2, wo2, l, tm)

        kv_p = kv32[:tp].reshape(nb, lp, 2, fh, fhd)[:, pad:]
        kv_s = kv32[tp:].reshape(ds_b, ds_t, 2, fh, fhd)
        from_pair = lambda a, n: jnp.transpose(a.reshape(npair, n, 2, sp), (1, 0, 2, 3)).reshape(n, sg, sp)
        xf_p = x_fin.reshape(npair, nb, 4 * sp)
        gq_p = gq[:tp].reshape(nb, lp, 3 * gw)
        gq_s = jnp.concatenate([state_gdn_conv[l], gq[tp:].reshape(ds_b, ds_t, 3 * gw)], axis=1)
        new = (kv_p[:, :, 0], kv_p[:, :, 1], jnp.transpose(lf_p[:, :, pad:], (0, 2, 1)),
               kv_s[:, :, 0], kv_s[:, :, 1], jnp.transpose(lf_s[:, :, :ds_t], (0, 2, 1)),
               from_pair(xf_p[:, :, :2 * sp], nb), from_pair(xf_p[:, :, 2 * sp:], nb),
               from_pair(xf_s[:, :, :2 * sp], ds_b), from_pair(xf_s[:, :, 2 * sp:], ds_b),
               s_p, s_s, gq_p[:, lp - (kconv - 1):], gq_s[:, -(kconv - 1):])
        for i, a in enumerate(new):
            outs[i].append(a)

    y = _final_norm(x, norm_final.reshape(1, d), tm)
    y_prompt = y[:tp].reshape(nb, lp, d)[:, pad + n_meta:]
    y_sample = y[tp:].reshape(ds_b, ds_t, d)
    return (y_prompt, y_sample) + tuple(jnp.stack(o) for o in outs)
```

```python
import functools
import math

import jax
import jax.numpy as jnp
from jax import lax
from jax.experimental import pallas as pl
from jax.experimental.pallas import tpu as pltpu

F32 = jnp.float32
BF16 = jnp.bfloat16
RMS_EPS = 1e-6
NEG_BIG = -1e30
LANES = 128
VMEM_LIMIT = 56 * 1024 * 1024
S5_CHUNK = 16
GDN_CHUNK = 128
HI = lax.Precision.HIGHEST


def _cparams(sem):
    return pltpu.CompilerParams(dimension_semantics=sem, vmem_limit_bytes=VMEM_LIMIT)


def _resident(shape, index_map):
    return pl.BlockSpec(shape, index_map, pipeline_mode=pl.Buffered(1))


def _dot(a, b):
    return jnp.dot(a, b, preferred_element_type=F32)


def _dot_hi(a, b):
    return jnp.dot(a, b, preferred_element_type=F32, precision=HI)


def _dot_nt(a, b):
    return lax.dot_general(a, b, (((1,), (1,)), ((), ())), preferred_element_type=F32)


def _split(x):
    hi = x.astype(BF16)
    return hi, (x - hi.astype(F32)).astype(BF16)


def _dot3(a, b):
    a_hi, a_lo = _split(a)
    b_hi, b_lo = _split(b)
    return _dot(jnp.concatenate([a_hi, a_lo, a_hi], axis=1), jnp.concatenate([b_hi, b_hi, b_lo], axis=0))


def _rms_bf16(x, w):
    ms = jnp.mean(x * x, axis=-1, keepdims=True)
    return (x * lax.rsqrt(ms + RMS_EPS) * w).astype(BF16)


def _sigmoid(x):
    return 1.0 / (1.0 + jnp.exp(-x))


def _row_tile(t, cap=1024):
    best = 8
    for tm in range(8, min(t, cap) + 1, 8):
        if t % tm == 0:
            best = tm
    return best


def _ffn_body(x_ref, nw_ref, wi_ref, wo_ref, o_ref, act_ref, *, d_ff, fc):
    x = x_ref[...]
    h = _rms_bf16(x, nw_ref[...])
    for c in range(d_ff // fc):
        g = _dot(h, wi_ref[:, c * fc:(c + 1) * fc])
        u = _dot(h, wi_ref[:, d_ff + c * fc:d_ff + (c + 1) * fc])
        act_ref[:, c * fc:(c + 1) * fc] = (g * _sigmoid(g) * u).astype(BF16)
    o_ref[...] = x + 0.5 * _dot(act_ref[...], wo_ref[...])


def _ffn(x, nw, wi, wo, layer):
    t, d = x.shape
    tm = _row_tile(t)
    d_ff = wo.shape[1]
    fc = 256 if d_ff % 256 == 0 else d_ff
    return pl.pallas_call(
        functools.partial(_ffn_body, d_ff=d_ff, fc=fc),
        grid=(t // tm,),
        in_specs=[pl.BlockSpec((tm, d), lambda i: (i, 0)),
                  _resident((None, 1, d), lambda i: (layer, 0, 0)),
                  _resident((None, d, 2 * d_ff), lambda i: (layer, 0, 0)),
                  _resident((None, d_ff, d), lambda i: (layer, 0, 0))],
        out_specs=pl.BlockSpec((tm, d), lambda i: (i, 0)),
        out_shape=jax.ShapeDtypeStruct((t, d), F32),
        scratch_shapes=[pltpu.VMEM((tm, d_ff), BF16)],
        compiler_params=_cparams(("arbitrary",)),
        name="ffn",
    )(x, nw, wi, wo)


def _inproj_body(x_ref, nw_ref, w_ref, h_ref, q_ref, k_ref, v_ref, kv_ref, u_ref, gq_ref, z_ref, sm_ref,
                 *, fw, sw, gw, q_scale):
    h = _rms_bf16(x_ref[...], nw_ref[...])
    h_ref[...] = h
    o = 0
    q = _dot(h, w_ref[:, o:o + fw]); o += fw
    q_ref[...] = (q * q_scale).astype(BF16)
    kv = _dot(h, w_ref[:, o:o + 2 * fw]); o += 2 * fw
    kv_ref[...] = kv
    k_ref[...] = kv[:, :fw].astype(BF16)
    v_ref[...] = kv[:, fw:].astype(BF16)
    u_ref[...] = _dot(h, w_ref[:, o:o + sw]); o += sw
    gq_ref[...] = _dot(h, w_ref[:, o:o + 3 * gw]); o += 3 * gw
    z_ref[...] = _dot(h, w_ref[:, o:o + gw]); o += gw
    sm_ref[...] = _dot(h, w_ref[:, o:o + LANES])


def _inproj(x, nw, w, layer, fw, sw, gw, q_scale):
    t, d = x.shape
    tm = _row_tile(t)
    n = w.shape[2]
    row = lambda width: pl.BlockSpec((tm, width), lambda i: (i, 0))
    widths = (d, fw, fw, fw, 2 * fw, sw, 3 * gw, gw, LANES)
    dts = (BF16, BF16, BF16, BF16, F32, F32, F32, F32, F32)
    return pl.pallas_call(
        functools.partial(_inproj_body, fw=fw, sw=sw, gw=gw, q_scale=q_scale),
        grid=(t // tm,),
        in_specs=[row(d), _resident((None, 1, d), lambda i: (layer, 0, 0)),
                  _resident((None, d, n), lambda i: (layer, 0, 0))],
        out_specs=[row(wd) for wd in widths],
        out_shape=[jax.ShapeDtypeStruct((t, wd), dt) for wd, dt in zip(widths, dts)],
        compiler_params=_cparams(("arbitrary",)),
        name="inproj",
    )(x, nw, w)


def _log_sigmoid(x):
    return jnp.minimum(x, 0.0) - jnp.log(1.0 + jnp.exp(-jnp.abs(x)))


def _fcum_body(lg_ref, b_ref, lf_ref, cum_ref, *, nblk, pad):
    r = lax.broadcasted_iota(jnp.int32, (LANES, LANES), 0)
    c = lax.broadcasted_iota(jnp.int32, (LANES, LANES), 1)
    upper = (r <= c).astype(F32)
    lane = lax.broadcasted_iota(jnp.int32, (1, LANES), 1)
    carry = jnp.zeros((lg_ref.shape[0], 1), F32)
    for i in range(nblk):
        sl = slice(i * LANES, (i + 1) * LANES)
        lf = _log_sigmoid(lg_ref[:, sl] + b_ref[...])
        lf_ref[:, sl] = lf
        cs = _dot_hi(lf, upper) + carry
        carry = cs[:, LANES - 1:LANES]
        cum_ref[:, sl] = jnp.where(lane + i * LANES < pad, -NEG_BIG, cs) if i * LANES < pad else cs


def _fcum(logits, bias, pad):
    ns, nh, length = logits.shape
    blk = pl.BlockSpec((None, nh, length), lambda s: (s, 0, 0))
    return pl.pallas_call(
        functools.partial(_fcum_body, nblk=length // LANES, pad=pad),
        grid=(ns,),
        in_specs=[blk, pl.BlockSpec((nh, 1), lambda s: (0, 0))],
        out_specs=[blk, blk],
        out_shape=[jax.ShapeDtypeStruct(logits.shape, F32)] * 2,
        compiler_params=_cparams(("arbitrary",)),
        name="fcum",
    )(logits, bias)


def _fox_prompt_body(q_ref, k_ref, v_ref, fk_ref, o_ref, m_sc, l_sc, acc_sc, *, tq, hd, n_heads):
    qi = pl.program_id(1)
    lane = lax.broadcasted_iota(jnp.int32, (1, LANES), 1)
    lo = lane < hd
    reps = tq // LANES
    causal = lax.broadcasted_iota(jnp.int32, (tq, tq), 1) <= lax.broadcasted_iota(jnp.int32, (tq, tq), 0)
    for p in range(n_heads * hd // LANES):
        qp = q_ref[:, p * LANES:(p + 1) * LANES]
        qm = (jnp.where(lo, qp, jnp.zeros_like(qp)), jnp.where(lo, jnp.zeros_like(qp), qp))
        m_sc[...] = jnp.full(m_sc.shape, NEG_BIG, F32)
        l_sc[...] = jnp.zeros(l_sc.shape, F32)
        acc_sc[...] = jnp.zeros(acc_sc.shape, F32)

        def kv_step(j, masked, p=p, qm=qm):
            start = pl.multiple_of(j * tq, tq)
            kb = k_ref[pl.ds(start, tq), p * LANES:(p + 1) * LANES]
            vb = v_ref[pl.ds(start, tq), p * LANES:(p + 1) * LANES]
            fkb = fk_ref[:, pl.ds(start, tq)]
            s = [_dot_nt(qm[hh], kb) - fkb[2 * p + hh:2 * p + hh + 1, :] for hh in range(2)]
            if masked:
                s = [jnp.where(causal, sh, 2.0 * NEG_BIG) for sh in s]
            m_old = [m_sc[hh] for hh in range(2)]
            m_new = [jnp.maximum(m_old[hh], jnp.max(s[hh], axis=-1, keepdims=True)) for hh in range(2)]
            alpha = [jnp.exp(m_old[hh] - m_new[hh]) for hh in range(2)]
            e = [jnp.exp(s[hh] - pltpu.repeat(m_new[hh], reps, 1)) for hh in range(2)]
            pv = [_dot(e[hh].astype(BF16), vb) for hh in range(2)]
            for hh in range(2):
                l_sc[hh] = alpha[hh] * l_sc[hh] + jnp.sum(e[hh], axis=-1, keepdims=True)
                m_sc[hh] = m_new[hh]
            acc_sc[...] = jnp.where(lo, alpha[0], alpha[1]) * acc_sc[...] + jnp.where(lo, pv[0], pv[1])

        def body(j, carry):
            kv_step(j, False)
            return carry

        lax.fori_loop(0, qi, body, 0)
        kv_step(qi, True)
        inv = 1.0 / jnp.where(lo, l_sc[0], l_sc[1])
        o_ref[:, p * LANES:(p + 1) * LANES] = (acc_sc[...] * inv).astype(BF16)


def _fox_prompt(q, k, v, fcum, nb, lp, n_heads, hd, tq):
    fw = n_heads * hd
    nq = lp // tq
    return pl.pallas_call(
        functools.partial(_fox_prompt_body, tq=tq, hd=hd, n_heads=n_heads),
        grid=(nb, nq),
        in_specs=[pl.BlockSpec((tq, fw), lambda b, i: (b * nq + i, 0)),
                  pl.BlockSpec((lp, fw), lambda b, i: (b, 0)),
                  pl.BlockSpec((lp, fw), lambda b, i: (b, 0)),
                  pl.BlockSpec((None, n_heads, lp), lambda b, i: (b, 0, 0))],
        out_specs=pl.BlockSpec((tq, fw), lambda b, i: (b * nq + i, 0)),
        out_shape=jax.ShapeDtypeStruct((nb * lp, fw), BF16),
        scratch_shapes=[pltpu.VMEM((2, tq, LANES), F32), pltpu.VMEM((2, tq, LANES), F32),
                        pltpu.VMEM((tq, LANES), F32)],
        compiler_params=_cparams(("arbitrary", "arbitrary")),
        name="fox_prompt",
    )(q, k, v, fcum)


def _fox_sample_body(pt_ref, qbd_ref, cn_ref, kn_ref, vn_ref, *rest, g_pages, n_q, n_heads, hd, page):
    kt_refs = rest[:g_pages]
    vt_refs = rest[g_pages:2 * g_pages]
    lf_refs = rest[2 * g_pages:3 * g_pages]
    o_ref, m_sc, l_sc, acc_sc, car_sc = rest[3 * g_pages:]
    del pt_ref
    g = pl.program_id(1)
    nr = n_q * n_heads
    fw = n_heads * hd
    qbd = qbd_ref[...]
    rr = lax.broadcasted_iota(jnp.int32, (nr, 1), 0)
    qidx = rr // n_heads
    lane = lax.broadcasted_iota(jnp.int32, (1, LANES), 1)
    col_new = jnp.concatenate([cn_ref[...]] * n_q, axis=0)
    row_new = jnp.sum(jnp.where(lane == qidx, col_new, 0.0), axis=-1, keepdims=True)

    def update(s, pv_of):
        m_old = m_sc[...]
        m_new = jnp.maximum(m_old, jnp.max(s, axis=-1, keepdims=True))
        a = jnp.exp(m_old - m_new)
        e = jnp.exp(s - m_new)
        l_sc[...] = a * l_sc[...] + jnp.sum(e, axis=-1, keepdims=True)
        m_sc[...] = m_new
        acc_sc[...] = a * acc_sc[...] + pv_of(e.astype(BF16))

    @pl.when(g == 0)
    def _():
        m_sc[...] = jnp.full(m_sc.shape, NEG_BIG, F32)
        l_sc[...] = jnp.zeros(l_sc.shape, F32)
        acc_sc[...] = jnp.zeros(acc_sc.shape, F32)
        car_sc[...] = jnp.zeros(car_sc.shape, F32)
        s = _dot_nt(qbd, kn_ref[...]) + (row_new - col_new)
        s = jnp.where((lane <= qidx) & (lane < n_q), s, NEG_BIG)
        update(s, lambda e: _dot(e, vn_ref[...]))

    r = lax.broadcasted_iota(jnp.int32, (page, page), 0)
    c = lax.broadcasted_iota(jnp.int32, (page, page), 1)
    later = (r > c).astype(F32)
    lfs = [lf_refs[i][...] for i in range(g_pages)]
    suffix = _dot_hi(jnp.concatenate(lfs, axis=0), later)
    carry = car_sc[...]
    scores = []
    for i in range(g_pages):
        bias = jnp.concatenate([suffix[i * n_heads:(i + 1) * n_heads] + carry] * n_q, axis=0) + row_new
        kt = kt_refs[i][...].reshape(fw, page).astype(BF16)
        scores.append(_dot(qbd, kt) + bias)
        carry = carry + jnp.sum(lfs[i], axis=-1, keepdims=True)
    car_sc[...] = carry

    def pv_all(e):
        out = None
        for i in range(g_pages):
            vt = vt_refs[i][...].reshape(fw, page).astype(BF16)
            t = _dot_nt(e[:, i * page:(i + 1) * page], vt)
            out = t if out is None else out + t
        return out

    update(jnp.concatenate(scores, axis=1), pv_all)

    @pl.when(g == pl.num_programs(1) - 1)
    def _():
        o = acc_sc[...] / l_sc[...]
        lane_w = lax.broadcasted_iota(jnp.int32, (1, fw), 1)
        o = jnp.where(lane_w // hd == rr % n_heads, o, 0.0)
        o_ref[...] = jnp.sum(o.reshape(n_q, n_heads, fw), axis=1).astype(BF16)


def _fox_sample(page_table, qbd, cum_new, k_new, v_new, kt, vt, lft, layer, g_pages, n_q):
    nb, n_pages = page_table.shape
    _, _, n_heads, hd, page = kt.shape
    fw = n_heads * hd
    nr = n_q * n_heads
    n_groups = n_pages // g_pages

    def page_map(i):
        return lambda b, g, pt: (layer, pt[b, n_pages - 1 - (g * g_pages + i)], 0, 0, 0)

    def lf_map(i):
        return lambda b, g, pt: (layer, pt[b, n_pages - 1 - (g * g_pages + i)], 0, 0)

    per_b = lambda shape: pl.BlockSpec((None,) + shape, lambda b, g, pt: (b, 0, 0))
    in_specs = ([per_b((nr, fw)), per_b((n_heads, LANES)), per_b((LANES, fw)), per_b((LANES, fw))]
                + [pl.BlockSpec((None, None, n_heads, hd, page), page_map(i)) for i in range(g_pages)]
                + [pl.BlockSpec((None, None, n_heads, hd, page), page_map(i)) for i in range(g_pages)]
                + [pl.BlockSpec((None, None, n_heads, page), lf_map(i)) for i in range(g_pages)])
    return pl.pallas_call(
        functools.partial(_fox_sample_body, g_pages=g_pages, n_q=n_q, n_heads=n_heads, hd=hd, page=page),
        grid_spec=pltpu.PrefetchScalarGridSpec(
            num_scalar_prefetch=1, grid=(nb, n_groups), in_specs=in_specs,
            out_specs=per_b((n_q, fw)),
            scratch_shapes=[pltpu.VMEM((nr, 1), F32), pltpu.VMEM((nr, 1), F32), pltpu.VMEM((nr, fw), F32),
                            pltpu.VMEM((n_heads, 1), F32)]),
        out_shape=jax.ShapeDtypeStruct((nb, n_q, fw), BF16),
        compiler_params=_cparams(("arbitrary", "arbitrary")),
        name="fox_sample",
    )(page_table, qbd, cum_new, k_new, v_new, *([kt] * g_pages), *([vt] * g_pages), *([lft] * g_pages))


def _gelu_tanh(x):
    return 0.5 * x * (1.0 + jnp.tanh(math.sqrt(2.0 / math.pi) * (x + 0.044715 * x * x * x)))


def _s5_tables(a_re, a_im, b_re, b_im, c_re, c_im, d_skip, log_dt, n_last):
    ck = S5_CHUNK
    g, p, gs = b_re.shape
    gt = LANES // gs
    nt = g // gt
    dt = jnp.exp(log_dt)[:, None]
    mag_l, ang = a_re * dt, a_im * dt

    def apow_fn(n):
        m = jnp.exp(mag_l * n)
        return m * jnp.cos(ang * n), m * jnp.sin(ang * n)

    ab_re, ab_im = apow_fn(1.0)
    den = a_re * a_re + a_im * a_im
    cf_re = ((ab_re - 1.0) * a_re + ab_im * a_im) / den
    cf_im = (ab_im * a_re - (ab_re - 1.0) * a_im) / den
    bb_re = cf_re[..., None] * b_re - cf_im[..., None] * b_im
    bb_im = cf_re[..., None] * b_im + cf_im[..., None] * b_re
    pw_re, pw_im = jax.vmap(apow_fn)(jnp.arange(ck + 1, dtype=F32))
    cab_re = c_re[None] * pw_re[:, :, None, :] - c_im[None] * pw_im[:, :, None, :]
    cab_im = c_re[None] * pw_im[:, :, None, :] + c_im[None] * pw_re[:, :, None, :]
    w = (jnp.einsum('lgcp,gpd->lgcd', cab_re[:ck], bb_re, precision=HI)
         - jnp.einsum('lgcp,gpd->lgcd', cab_im[:ck], bb_im, precision=HI))
    w = w.at[0].add(jax.vmap(jnp.diag)(d_skip))
    eye = jnp.eye(gt, dtype=F32)
    wt = jnp.transpose(w, (1, 3, 0, 2)).reshape(nt, gt, gs, ck, gs)
    wcat = jnp.einsum('jgdlc,gh->jgdlhc', wt, eye).reshape(nt, gt * gs, ck * gt * gs)

    e_pow = ck - 1 - jnp.arange(ck)
    f_re = pw_re[e_pow][..., None] * bb_re[None] - pw_im[e_pow][..., None] * bb_im[None]
    f_im = pw_re[e_pow][..., None] * bb_im[None] + pw_im[e_pow][..., None] * bb_re[None]

    def f_tile(f):
        ft = jnp.transpose(f, (1, 0, 3, 2)).reshape(nt, gt, ck, gs, p)
        return jnp.einsum('jgsdp,gh->jsgdhp', ft, eye).reshape(nt, ck, gt * gs, gt * p)

    fmat = jnp.concatenate([f_tile(f_re), f_tile(f_im)], axis=-1)

    def e_tile(m):
        et = jnp.transpose(m, (1, 0, 3, 2)).reshape(nt, gt, ck, p, gs)
        return jnp.einsum('jgtpc,gh->jtgphc', et, eye).reshape(nt, ck, gt * p, gt * gs)

    emat = jnp.concatenate([e_tile(cab_re[1:]), e_tile(-cab_im[1:])], axis=2)

    tile_vec = lambda v: v.reshape(nt, gt * p)
    nlev = 16
    lev = [apow_fn(float(ck * 2 ** i)) for i in range(nlev)]
    apow = jnp.stack([jnp.stack([tile_vec(r), tile_vec(i)], axis=1) for r, i in lev], axis=1)
    al_re, al_im = apow_fn(float(n_last))
    alast = jnp.stack([tile_vec(al_re), tile_vec(al_im)], axis=1)
    return wcat.astype(BF16), fmat.astype(BF16), emat.astype(BF16), apow, alast


def _s5_local_body(u_ref, w_ref, f_ref, y_ref, z_ref, acc_ref, *, ck, n):
    accz = None
    for s in range(ck):
        us = u_ref[pl.ds(s, n, stride=ck), :].astype(BF16)
        y = _dot(us, w_ref[:, :(ck - s) * LANES])
        if s == 0:
            acc_ref[...] = y
        else:
            acc_ref[:, s * LANES:] += y
        z = _dot(us, f_ref[s])
        accz = z if accz is None else accz + z
    for t in range(ck):
        y_ref[pl.ds(t, n, stride=ck), :] = acc_ref[:, t * LANES:(t + 1) * LANES]
    z_ref[...] = accz


def _s5_local(u, wcat, fmat, rows_per_step):
    r, width = u.shape
    ck = S5_CHUNK
    nt = width // LANES
    sw = fmat.shape[3]
    n = rows_per_step // ck
    return pl.pallas_call(
        functools.partial(_s5_local_body, ck=ck, n=n),
        grid=(nt, r // rows_per_step),
        in_specs=[pl.BlockSpec((rows_per_step, LANES), lambda j, i: (i, j)),
                  pl.BlockSpec((None, LANES, ck * LANES), lambda j, i: (j, 0, 0)),
                  pl.BlockSpec((None, ck, LANES, sw), lambda j, i: (j, 0, 0, 0))],
        out_specs=[pl.BlockSpec((rows_per_step, LANES), lambda j, i: (i, j)),
                   pl.BlockSpec((None, n, sw), lambda j, i: (j, i, 0))],
        out_shape=[jax.ShapeDtypeStruct((r, width), F32), jax.ShapeDtypeStruct((nt, r // ck, sw), F32)],
        scratch_shapes=[pltpu.VMEM((n, ck * LANES), F32)],
        compiler_params=_cparams(("arbitrary", "arbitrary")),
        name="s5_local",
    )(u, wcat, fmat)


def _s5_scan_body(z_ref, ap_ref, xs_ref, xf_ref, *, nc, half):
    xr = z_ref[:, :half]
    xi = z_ref[:, half:]
    row = lax.broadcasted_iota(jnp.int32, (nc, 1), 0)
    lev, s = 0, 1
    while s < nc:
        ar = ap_ref[lev, 0:1, :]
        ai = ap_ref[lev, 1:2, :]
        keep = row >= s
        sr = jnp.where(keep, pltpu.roll(xr, s, 0), 0.0)
        si = jnp.where(keep, pltpu.roll(xi, s, 0), 0.0)
        xr, xi = xr + ar * sr - ai * si, xi + ar * si + ai * sr
        lev, s = lev + 1, s * 2
    xf_ref[:, :half] = xr[nc - 1:nc, :]
    xf_ref[:, half:] = xi[nc - 1:nc, :]
    xs_ref[:, :half] = jnp.where(row >= 1, pltpu.roll(xr, 1, 0), 0.0)
    xs_ref[:, half:] = jnp.where(row >= 1, pltpu.roll(xi, 1, 0), 0.0)


def _s5_scan(z, apow, nb):
    nt, n, sw = z.shape
    nc = n // nb
    nlev = apow.shape[1]
    return pl.pallas_call(
        functools.partial(_s5_scan_body, nc=nc, half=sw // 2),
        grid=(nt, nb),
        in_specs=[pl.BlockSpec((None, nc, sw), lambda j, b: (j, b, 0)),
                  pl.BlockSpec((None, nlev, 2, sw // 2), lambda j, b: (j, 0, 0, 0))],
        out_specs=[pl.BlockSpec((None, nc, sw), lambda j, b: (j, b, 0)),
                   pl.BlockSpec((None, None, 1, sw), lambda j, b: (j, b, 0, 0))],
        out_shape=[jax.ShapeDtypeStruct((nt, n, sw), F32), jax.ShapeDtypeStruct((nt, nb, 1, sw), F32)],
        compiler_params=_cparams(("arbitrary", "arbitrary")),
        name="s5_scan",
    )(z, apow)


def _s5_out_body(y_ref, xs_ref, e_ref, o_ref, act_ref, *, ck, n, first):
    xs = xs_ref[...].astype(BF16)
    for t in range(ck):
        rows = pl.ds(t, n, stride=ck)
        y = y_ref[rows, :]
        if t >= first:
            y = y + _dot(xs, e_ref[t - first])
        act_ref[rows, :] = _gelu_tanh(y)
    o_ref[...] = act_ref[...].astype(BF16)


def _s5_out(ylocal, xstart, emat, rows_per_step, first):
    r, width = ylocal.shape
    ck = S5_CHUNK
    nt = width // LANES
    sw = xstart.shape[2]
    n = rows_per_step // ck
    return pl.pallas_call(
        functools.partial(_s5_out_body, ck=ck, n=n, first=first),
        grid=(nt, r // rows_per_step),
        in_specs=[pl.BlockSpec((rows_per_step, LANES), lambda j, i: (i, j)),
                  pl.BlockSpec((None, n, sw), lambda j, i: (j, i, 0)),
                  pl.BlockSpec((None, ck, sw, LANES), lambda j, i: (j, 0, 0, 0))],
        out_specs=pl.BlockSpec((rows_per_step, LANES), lambda j, i: (i, j)),
        out_shape=jax.ShapeDtypeStruct((r, width), BF16),
        scratch_shapes=[pltpu.VMEM((rows_per_step, LANES), F32)],
        compiler_params=_cparams(("arbitrary", "arbitrary")),
        name="s5_out",
    )(ylocal, xstart, emat)


def _s5_step_body(z_ref, x0_ref, al_ref, xf_ref, *, half):
    ar, ai = al_ref[0:1, :], al_ref[1:2, :]
    xr, xi = x0_ref[:, :half], x0_ref[:, half:]
    xf_ref[:, :half] = z_ref[:, :half] + ar * xr - ai * xi
    xf_ref[:, half:] = z_ref[:, half:] + ar * xi + ai * xr


def _s5_step(z, x0, alast):
    nt, n, sw = z.shape
    blk = pl.BlockSpec((None, n, sw), lambda j: (j, 0, 0))
    return pl.pallas_call(
        functools.partial(_s5_step_body, half=sw // 2),
        grid=(nt,),
        in_specs=[blk, blk, pl.BlockSpec((None, 2, sw // 2), lambda j: (j, 0, 0))],
        out_specs=blk,
        out_shape=jax.ShapeDtypeStruct(z.shape, F32),
        compiler_params=_cparams(("arbitrary",)),
        name="s5_step",
    )(z, x0, alast)


def _gdn_body(x_ref, z_ref, sm_ref, ci_ref, s0_ref, cw_ref, na_ref, db_ref, nw_ref, o_ref, sf_ref, s_sc, car_sc,
              *, c, sb, n_valid, n_heads, hd, kconv, a_lane, b_lane):
    ci = pl.program_id(1)
    gw = n_heads * hd

    @pl.when(ci == 0)
    def _():
        s_sc[...] = s0_ref[...]
        car_sc[...] = ci_ref[...]

    r = lax.broadcasted_iota(jnp.int32, (c, c), 0)
    cc = lax.broadcasted_iota(jnp.int32, (c, c), 1)
    incl = r >= cc
    strict = r > cc
    eye = (r == cc).astype(F32)
    row = ci * c + lax.broadcasted_iota(jnp.int32, (c, 1), 0)
    hist = car_sc.shape[1]
    chains = []
    for b in range(sb):
        x = x_ref[b]
        xc = jnp.concatenate([car_sc[b], x], axis=0)
        y = None
        for j in range(kconv):
            off = hist - (kconv - 1) + j
            term = xc[off:off + c, :] * cw_ref[j:j + 1, :]
            y = term if y is None else y + term
        car_sc[b] = x[c - hist:, :]
        qkv = y * _sigmoid(y)

        sm = sm_ref[b]
        sp_in = sm + db_ref[...]
        softplus = jnp.maximum(sp_in, 0.0) + jnp.log(1.0 + jnp.exp(-jnp.abs(sp_in)))
        gfull = na_ref[...] * softplus
        beta_full = _sigmoid(sm)
        if n_valid is not None:
            live = row < n_valid
            gfull = jnp.where(live, gfull, 0.0)
            beta_full = jnp.where(live, beta_full, 0.0)
            qkv = jnp.where(live, qkv, 0.0)
        cum_full = _dot_hi(incl.astype(F32), gfull)
        cum_t = cum_full.T
        for h in range(n_heads):
            q = qkv[:, h * hd:(h + 1) * hd]
            k = qkv[:, gw + h * hd:gw + (h + 1) * hd]
            v = qkv[:, 2 * gw + h * hd:2 * gw + (h + 1) * hd]
            q = q * lax.rsqrt(jnp.sum(q * q, axis=-1, keepdims=True) + 1e-6) * (hd ** -0.5)
            k = k * lax.rsqrt(jnp.sum(k * k, axis=-1, keepdims=True) + 1e-6)
            cum = cum_full[:, a_lane + h:a_lane + h + 1]
            cum_r = cum_t[a_lane + h:a_lane + h + 1, :]
            beta = beta_full[:, b_lane + h:b_lane + h + 1]
            decay = jnp.where(incl, jnp.exp(jnp.where(incl, cum - cum_r, 0.0)), 0.0)
            kb = k * beta
            k16 = k.astype(BF16)
            ecum = jnp.exp(cum)
            g_last = cum[c - 1:c, :]
            chains.append(dict(
                b=b, h=h, k16=k16, decay=decay, qe16=(q * ecum).astype(BF16), q16=q.astype(BF16),
                rhs=jnp.concatenate([v * beta, kb * ecum], axis=-1), kb16=kb.astype(BF16),
                kd=k * jnp.exp(g_last - cum), s_scale=jnp.exp(g_last)))
    for ch in chains:
        ch['lower'] = jnp.where(strict, _dot_nt(ch['kb16'], ch['k16']) * ch['decay'], 0.0)
        ch['aqk16'] = jnp.where(incl, _dot_nt(ch['q16'], ch['k16']) * ch['decay'], 0.0).astype(BF16)
    for ch in chains:
        ch['t_inv'] = eye - ch['lower']
        ch['pw'] = _dot3(ch['lower'], ch['lower'])
    span = 2
    while span < c:
        span *= 2
        for ch in chains:
            if span < c:
                both = _dot3(ch['pw'], jnp.concatenate([ch['t_inv'], ch['pw']], axis=1))
                ch['t_inv'] = ch['t_inv'] + both[:, :c]
                ch['pw'] = both[:, c:]
            else:
                ch['t_inv'] = ch['t_inv'] + _dot3(ch['pw'], ch['t_inv'])
    for ch in chains:
        ch['sol'] = _dot3(ch['t_inv'], ch['rhs'])
    for ch in chains:
        s_h = s_sc[ch['b'], ch['h']]
        s16 = s_h.astype(BF16)
        v_new = ch['sol'][:, :hd] - _dot(ch['sol'][:, hd:].astype(BF16), s16)
        ch['vn16'] = v_new.astype(BF16)
        ch['o'] = _dot(ch['qe16'], s16)
        ch['s_dec'] = s_h * ch['s_scale']
    for ch in chains:
        b, h = ch['b'], ch['h']
        o = ch['o'] + _dot(ch['aqk16'], ch['vn16'])
        s_sc[b, h] = ch['s_dec'] + _dot(ch['kd'].T.astype(BF16), ch['vn16'])
        o = o * lax.rsqrt(jnp.mean(o * o, axis=-1, keepdims=True) + RMS_EPS) * nw_ref[...]
        zz = z_ref[b, :, h * hd:(h + 1) * hd]
        o_ref[b, :, h * hd:(h + 1) * hd] = (o * (zz * _sigmoid(zz))).astype(BF16)

    @pl.when(ci == pl.num_programs(1) - 1)
    def _():
        sf_ref[...] = s_sc[...]


def _gdn(x, z, sm, conv_init, s0, conv_w, neg_a, dt_b, norm_w, sb, n_valid, a_lane, b_lane):
    ns, length, w3 = x.shape
    n_heads, hd = s0.shape[1], s0.shape[2]
    gw = n_heads * hd
    c = GDN_CHUNK
    kconv = conv_w.shape[0]
    hist = conv_init.shape[1]
    rowblk = lambda width: pl.BlockSpec((sb, c, width), lambda s, i: (s, i, 0))
    const = lambda shape: pl.BlockSpec(shape, lambda s, i: (0,) * len(shape))
    return pl.pallas_call(
        functools.partial(_gdn_body, c=c, sb=sb, n_valid=n_valid, n_heads=n_heads, hd=hd, kconv=kconv,
                          a_lane=a_lane, b_lane=b_lane),
        grid=(ns // sb, length // c),
        in_specs=[rowblk(w3), rowblk(gw), rowblk(LANES),
                  pl.BlockSpec((sb, hist, w3), lambda s, i: (s, 0, 0)),
                  pl.BlockSpec((sb, n_heads, hd, hd), lambda s, i: (s, 0, 0, 0)),
                  const((kconv, w3)), const((1, LANES)), const((1, LANES)), const((1, hd))],
        out_specs=[rowblk(gw), pl.BlockSpec((sb, n_heads, hd, hd), lambda s, i: (s, 0, 0, 0))],
        out_shape=[jax.ShapeDtypeStruct((ns, length, gw), BF16), jax.ShapeDtypeStruct(s0.shape, F32)],
        scratch_shapes=[pltpu.VMEM((sb, n_heads, hd, hd), F32), pltpu.VMEM((sb, hist, w3), F32)],
        compiler_params=_cparams(("arbitrary", "arbitrary")),
        name="gdn",
    )(x, z, sm, conv_init, s0, conv_w, neg_a, dt_b, norm_w)


def _merge_body(x_ref, h_ref, of_ref, z5_ref, og_ref, wf_ref, w5_ref, wg_ref, wgate_ref, wo_ref, o_ref, *, d):
    fox_br = _dot(of_ref[...], wf_ref[...])
    glu = _dot(z5_ref[...], w5_ref[...])
    s5_br = glu[:, :d] * _sigmoid(glu[:, d:])
    gdn_br = _dot(og_ref[...], wg_ref[...])
    h = h_ref[...]
    merged = (_sigmoid(_dot(h, wgate_ref[:, :d])) * fox_br
              + _sigmoid(_dot(h, wgate_ref[:, d:2 * d])) * s5_br
              + _sigmoid(_dot(h, wgate_ref[:, 2 * d:])) * gdn_br)
    o_ref[...] = x_ref[...] + _dot(merged.astype(BF16), wo_ref[...])


def _merge(x, h, o_fox, z5, o_gdn, wf, w5, wg, wgate, wo, layer):
    t, d = x.shape
    tm = _row_tile(t)
    row = lambda a: pl.BlockSpec((tm, a.shape[1]), lambda i: (i, 0))
    res = lambda a: _resident((None,) + a.shape[1:], lambda i: (layer, 0, 0))
    return pl.pallas_call(
        functools.partial(_merge_body, d=d),
        grid=(t // tm,),
        in_specs=[row(x), row(h), row(o_fox), row(z5), row(o_gdn), res(wf), res(w5), res(wg), res(wgate), res(wo)],
        out_specs=pl.BlockSpec((tm, d), lambda i: (i, 0)),
        out_shape=jax.ShapeDtypeStruct((t, d), F32),
        compiler_params=_cparams(("arbitrary",)),
        name="merge",
    )(x, h, o_fox, z5, o_gdn, wf, w5, wg, wgate, wo)


def _final_norm_body(x_ref, w_ref, o_ref):
    x = x_ref[...]
    ms = jnp.mean(x * x, axis=-1, keepdims=True)
    o_ref[...] = x * lax.rsqrt(ms + RMS_EPS) * w_ref[...]


def _final_norm(x, w):
    t, d = x.shape
    tm = _row_tile(t)
    return pl.pallas_call(
        _final_norm_body,
        grid=(t // tm,),
        in_specs=[pl.BlockSpec((tm, d), lambda i: (i, 0)), pl.BlockSpec((1, d), lambda i: (0, 0))],
        out_specs=pl.BlockSpec((tm, d), lambda i: (i, 0)),
        out_shape=jax.ShapeDtypeStruct((t, d), F32),
        compiler_params=_cparams(("arbitrary",)),
        name="final_norm",
    )(x, w)


def kernel(x_prompt, x_sample, cache_fox_k, cache_fox_v, cache_fox_logf, state_s5_re, state_s5_im, state_gdn, state_gdn_conv, page_table, meta_tokens, norm_ffn1, w_ffn1_in, w_ffn1_out, norm_mix, w_in, fox_b_f, w_fox_br, s5_A_re, s5_A_im, s5_B_re, s5_B_im, s5_C_re, s5_C_im, s5_D, s5_log_dt, w_s5_glu, gdn_conv_w, gdn_A_log, gdn_dt_bias, gdn_norm, w_gdn_br, w_out, norm_ffn2, w_ffn2_in, w_ffn2_out, norm_final):
    nb, seq, d = x_prompt.shape
    ds_b, ds_t, _ = x_sample.shape
    depth = w_in.shape[0]
    n_meta = meta_tokens.shape[0]
    fh, fhd = cache_fox_k.shape[3], cache_fox_k.shape[4]
    fw = fh * fhd
    page = cache_fox_k.shape[2]
    sg, sp, sgs = s5_B_re.shape[1:]
    sw = sg * sgs
    gh, ghd = state_gdn.shape[2], state_gdn.shape[3]
    gw = gh * ghd
    kconv = gdn_conv_w.shape[1]
    ck = S5_CHUNK
    gc = GDN_CHUNK
    assert fw % LANES == 0 and 2 * fhd == LANES and ghd == LANES and sw % LANES == 0 and LANES % sgs == 0
    assert ds_t <= ck and ds_t >= kconv - 1 and ds_t <= gc
    assert page == LANES and fh + 2 * gh <= LANES

    lreal = n_meta + seq
    pad = (-lreal) % LANES
    lp = lreal + pad
    tp = nb * lp
    ts = ds_b * ds_t
    tq = max(c for c in (128, 256, 384, 512) if lp % c == 0)
    n_pages = page_table.shape[1]
    g_pages = max(c for c in (1, 2, 4, 8) if n_pages % c == 0)
    hist = 8
    nt = sw // LANES
    gt = LANES // sgs
    ssw = 2 * gt * sp

    sizes = (fw, fw, fw, fh, sw, 3 * gw, gh, gh, gw, 3 * d)
    offs = [0]
    for s in sizes:
        offs.append(offs[-1] + s)
    col = lambda i: w_in[:, :, offs[i]:offs[i + 1]]
    small = jnp.concatenate([col(3), col(6), col(7),
                             jnp.zeros((depth, d, LANES - fh - 2 * gh), F32)], axis=2)
    a_lane, b_lane = fh, fh + gh
    w_proj = jnp.concatenate([col(0), col(1), col(2), col(4), col(5), col(8), small], axis=2).astype(BF16)
    w_gate = col(9).astype(BF16)
    wi1, wo1 = w_ffn1_in.astype(BF16), w_ffn1_out.astype(BF16)
    wi2, wo2 = w_ffn2_in.astype(BF16), w_ffn2_out.astype(BF16)
    wf16, w516, wg16, wo16 = (w.astype(BF16) for w in (w_fox_br, w_s5_glu, w_gdn_br, w_out))
    nrm = lambda w: w.reshape(depth, 1, d)
    n1, nm, n2 = nrm(norm_ffn1), nrm(norm_mix), nrm(norm_ffn2)
    lane_vec = lambda v, at: jnp.zeros((depth, 1, LANES), F32).at[:, 0, at:at + v.shape[1]].set(v)
    neg_a = lane_vec(-jnp.exp(gdn_A_log), a_lane)
    dt_b = lane_vec(gdn_dt_bias, a_lane)
    s5_tab = [_s5_tables(s5_A_re[l], s5_A_im[l], s5_B_re[l], s5_B_im[l], s5_C_re[l], s5_C_im[l], s5_D[l],
                         s5_log_dt[l], ds_t) for l in range(depth)]

    kt_cache = jnp.transpose(cache_fox_k, (0, 1, 3, 4, 2))
    vt_cache = jnp.transpose(cache_fox_v, (0, 1, 3, 4, 2))
    lf_cache = jnp.transpose(cache_fox_logf, (0, 1, 3, 2))

    meta = jnp.broadcast_to(meta_tokens[None].astype(F32), (nb, n_meta, d))
    xp = jnp.concatenate([jnp.zeros((nb, pad, d), F32), meta, x_prompt], axis=1).reshape(tp, d)
    xs = x_sample.reshape(ts, d)

    zeros_conv = jnp.zeros((nb, hist, 3 * gw), F32)
    zeros_gdn = jnp.zeros((nb, gh, ghd, ghd), F32)
    sb_p = 2 if nb % 2 == 0 else 1
    sb_s = 2 if ds_b % 2 == 0 else 1
    head_eye = jnp.eye(fh, dtype=BF16).reshape(1, 1, fh, fh, 1)
    to_tile = lambda st: jnp.transpose(st.reshape(ds_b, nt, gt * sp), (1, 0, 2))
    from_tile = lambda a, n: jnp.transpose(a, (1, 0, 2)).reshape(n, sg, sp)
    outs = [[] for _ in range(14)]

    for l in range(depth):
        xp = _ffn(xp, n1, wi1, wo1, l)
        xs = _ffn(xs, n1, wi1, wo1, l)
        hp, qp, kp, vp, kvp, up, gqp, zp, smp = _inproj(xp, nm, w_proj, l, fw, sw, gw, fhd ** -0.5)
        hs, qs, ks, vs, kvs, us, gqs, zs, sms = _inproj(xs, nm, w_proj, l, fw, sw, gw, fhd ** -0.5)
        bias_f = fox_b_f[l].reshape(fh, 1)

        lg_p = jnp.transpose(smp[:, :fh].reshape(nb, lp, fh), (0, 2, 1))
        lf_p, cum_p = _fcum(lg_p, bias_f, pad)
        o_fox_p = _fox_prompt(qp, kp, vp, cum_p, nb, lp, fh, fhd, tq)
        lg_s = jnp.transpose(sms[:, :fh].reshape(ds_b, ds_t, fh), (0, 2, 1))
        lg_s = jnp.pad(lg_s, ((0, 0), (0, 0), (0, LANES - ds_t)))
        lf_s, cum_s = _fcum(lg_s, bias_f, 0)
        qbd = (qs.reshape(ds_b, ds_t, 1, fh, fhd) * head_eye).reshape(ds_b, ds_t * fh, fw)
        rows_pad = ((0, 0), (0, LANES - ds_t), (0, 0))
        k_new = jnp.pad(ks.reshape(ds_b, ds_t, fw), rows_pad)
        v_new = jnp.pad(vs.reshape(ds_b, ds_t, fw), rows_pad)
        o_fox_s = _fox_sample(page_table, qbd, cum_s, k_new, v_new, kt_cache, vt_cache, lf_cache, l, g_pages, ds_t)

        wcat, fmat, emat, apow, alast = s5_tab[l]
        y_loc, z_loc = _s5_local(up, wcat, fmat, lp)
        x_start, x_fin = _s5_scan(z_loc, apow, nb)
        z5_p = _s5_out(y_loc, x_start, emat, lp, 0)
        us_c = jnp.pad(us.reshape(ds_b, ds_t, sw), ((0, 0), (ck - ds_t, 0), (0, 0))).reshape(ds_b * ck, sw)
        x0 = jnp.concatenate([to_tile(state_s5_re[l]), to_tile(state_s5_im[l])], axis=2)
        y_loc_s, z_loc_s = _s5_local(us_c, wcat, fmat, ds_b * ck)
        z5_s = _s5_out(y_loc_s, x0, emat, ds_b * ck, ck - ds_t)
        z5_s = z5_s.reshape(ds_b, ck, sw)[:, ck - ds_t:].reshape(ts, sw)
        xf_s = _s5_step(z_loc_s, x0, alast)

        cw = gdn_conv_w[l]
        gnorm = gdn_norm[l].reshape(1, ghd)
        o_gdn_p, s_p = _gdn(gqp.reshape(nb, lp, 3 * gw), zp.reshape(nb, lp, gw), smp.reshape(nb, lp, LANES),
                            zeros_conv, zeros_gdn, cw, neg_a[l], dt_b[l], gnorm, sb_p, None, a_lane, b_lane)
        seq_pad = lambda a: jnp.pad(a.reshape(ds_b, ds_t, a.shape[1]), ((0, 0), (0, gc - ds_t), (0, 0)))
        conv_s = jnp.pad(state_gdn_conv[l], ((0, 0), (hist - (kconv - 1), 0), (0, 0)))
        o_gdn_s, s_s = _gdn(seq_pad(gqs), seq_pad(zs), seq_pad(sms), conv_s, state_gdn[l], cw, neg_a[l], dt_b[l],
                            gnorm, sb_s, ds_t, a_lane, b_lane)

        xp = _merge(xp, hp, o_fox_p, z5_p, o_gdn_p.reshape(tp, gw), wf16, w516, wg16, w_gate, wo16, l)
        xs = _merge(xs, hs, o_fox_s.reshape(ts, fw), z5_s, o_gdn_s[:, :ds_t].reshape(ts, gw),
                    wf16, w516, wg16, w_gate, wo16, l)
        xp = _ffn(xp, n2, wi2, wo2, l)
        xs = _ffn(xs, n2, wi2, wo2, l)

        kv_p = kvp.reshape(nb, lp, 2, fh, fhd)[:, pad:]
        kv_s = kvs.reshape(ds_b, ds_t, 2, fh, fhd)
        xf_p = x_fin.reshape(nt, nb, ssw)
        gq_s = jnp.concatenate([state_gdn_conv[l], gqs.reshape(ds_b, ds_t, 3 * gw)], axis=1)
        new = (kv_p[:, :, 0], kv_p[:, :, 1], jnp.transpose(lf_p[:, :, pad:], (0, 2, 1)),
               kv_s[:, :, 0], kv_s[:, :, 1], jnp.transpose(lf_s[:, :, :ds_t], (0, 2, 1)),
               from_tile(xf_p[:, :, :ssw // 2], nb), from_tile(xf_p[:, :, ssw // 2:], nb),
               from_tile(xf_s[:, :, :ssw // 2], ds_b), from_tile(xf_s[:, :, ssw // 2:], ds_b),
               s_p, s_s, gqp.reshape(nb, lp, 3 * gw)[:, lp - (kconv - 1):], gq_s[:, -(kconv - 1):])
        for i, a in enumerate(new):
            outs[i].append(a)

    y_prompt = _final_norm(xp, norm_final.reshape(1, d)).reshape(nb, lp, d)[:, pad + n_meta:]
    y_sample = _final_norm(xs, norm_final.reshape(1, d)).reshape(ds_b, ds_t, d)
    return (y_prompt, y_sample) + tuple(jnp.stack(o) for o in outs)
```

```python
import functools
import math

import jax
import jax.numpy as jnp
from jax import lax
from jax.experimental import pallas as pl
from jax.experimental.pallas import tpu as pltpu

F32 = jnp.float32
BF16 = jnp.bfloat16
RMS_EPS = 1e-6
NEG_BIG = -1e30
LANES = 128
VMEM_LIMIT = 56 * 1024 * 1024
S5_CHUNK = 16
GDN_CHUNK = 128
HI = lax.Precision.HIGHEST
LOG2E = math.log2(math.e)


def _cparams(sem):
    return pltpu.CompilerParams(dimension_semantics=sem, vmem_limit_bytes=VMEM_LIMIT)


def _resident(shape, index_map):
    return pl.BlockSpec(shape, index_map, pipeline_mode=pl.Buffered(1))


def _dot(a, b):
    return jnp.dot(a, b, preferred_element_type=F32)


def _dot_hi(a, b):
    return jnp.dot(a, b, preferred_element_type=F32, precision=HI)


def _dot_nt(a, b):
    return lax.dot_general(a, b, (((1,), (1,)), ((), ())), preferred_element_type=F32)


def _split(x):
    hi = x.astype(BF16)
    return hi, (x - hi.astype(F32)).astype(BF16)


def _dot3(a, b):
    a_hi, a_lo = _split(a)
    b_hi, b_lo = _split(b)
    return _dot(jnp.concatenate([a_hi, a_lo, a_hi], axis=1), jnp.concatenate([b_hi, b_hi, b_lo], axis=0))


def _rms_bf16(x, w):
    ms = jnp.mean(x * x, axis=-1, keepdims=True)
    return (x * lax.rsqrt(ms + RMS_EPS) * w).astype(BF16)


def _sigmoid(x):
    return 1.0 / (1.0 + jnp.exp(-x))


def _row_tile(t, cap=1024):
    best = 8
    for tm in range(8, min(t, cap) + 1, 8):
        if t % tm == 0:
            best = tm
    return best


def _ffn_body(x_ref, nw_ref, wi_ref, wo_ref, o_ref, act_ref, *, d_ff, fc):
    x = x_ref[...]
    h = _rms_bf16(x, nw_ref[...])
    for c in range(d_ff // fc):
        g = _dot(h, wi_ref[:, c * fc:(c + 1) * fc])
        u = _dot(h, wi_ref[:, d_ff + c * fc:d_ff + (c + 1) * fc])
        act_ref[:, c * fc:(c + 1) * fc] = (g * _sigmoid(g) * u).astype(BF16)
    o_ref[...] = x + 0.5 * _dot(act_ref[...], wo_ref[...])


def _ffn(x, nw, wi, wo, layer):
    t, d = x.shape
    tm = _row_tile(t)
    d_ff = wo.shape[1]
    fc = 256 if d_ff % 256 == 0 else d_ff
    return pl.pallas_call(
        functools.partial(_ffn_body, d_ff=d_ff, fc=fc),
        grid=(t // tm,),
        in_specs=[pl.BlockSpec((tm, d), lambda i: (i, 0)),
                  _resident((None, 1, d), lambda i: (layer, 0, 0)),
                  _resident((None, d, 2 * d_ff), lambda i: (layer, 0, 0)),
                  _resident((None, d_ff, d), lambda i: (layer, 0, 0))],
        out_specs=pl.BlockSpec((tm, d), lambda i: (i, 0)),
        out_shape=jax.ShapeDtypeStruct((t, d), F32),
        scratch_shapes=[pltpu.VMEM((tm, d_ff), BF16)],
        compiler_params=_cparams(("arbitrary",)),
        name="ffn",
    )(x, nw, wi, wo)


def _inproj_body(x_ref, nw_ref, w_ref, *rest, fw, sw, gw, q_scale, kv_t):
    if kv_t:
        wkvt_ref, h_ref, q_ref, k_ref, v_ref, kv_ref, u_ref, gq_ref, z_ref, sm_ref = rest
    else:
        h_ref, q_ref, k_ref, v_ref, kv_ref, u_ref, gq_ref, z_ref, sm_ref = rest
    h = _rms_bf16(x_ref[...], nw_ref[...])
    h_ref[...] = h
    o = 0
    q = _dot(h, w_ref[:, o:o + fw]); o += fw
    q_ref[...] = (q * q_scale).astype(BF16)
    kv = _dot(h, w_ref[:, o:o + 2 * fw]); o += 2 * fw
    if kv_t:
        kv_ref[...] = _dot_nt(wkvt_ref[...], h)
    else:
        kv_ref[...] = kv
    k_ref[...] = kv[:, :fw].astype(BF16)
    v_ref[...] = kv[:, fw:].astype(BF16)
    u_ref[...] = _dot(h, w_ref[:, o:o + sw]); o += sw
    gq_ref[...] = _dot(h, w_ref[:, o:o + 3 * gw]); o += 3 * gw
    z_ref[...] = _dot(h, w_ref[:, o:o + gw]); o += gw
    sm_ref[...] = _dot(h, w_ref[:, o:o + LANES])


def _inproj(x, nw, w, layer, fw, sw, gw, q_scale, w_kvt=None, seq_len=None):
    t, d = x.shape
    n = w.shape[2]
    kv_t = w_kvt is not None
    tm = max(c for c in (128, 256, 384, 512) if seq_len % c == 0) if kv_t else _row_tile(t)
    row = lambda width: pl.BlockSpec((tm, width), lambda i: (i, 0))
    widths = (d, fw, fw, fw, 2 * fw, sw, 3 * gw, gw, LANES)
    dts = (BF16, BF16, BF16, BF16, F32, F32, F32, F32, F32)
    in_specs = [row(d), _resident((None, 1, d), lambda i: (layer, 0, 0)),
                _resident((None, d, n), lambda i: (layer, 0, 0))]
    out_specs = [row(wd) for wd in widths]
    out_shape = [jax.ShapeDtypeStruct((t, wd), dt) for wd, dt in zip(widths, dts)]
    args = (x, nw, w)
    if kv_t:
        per_seq = seq_len // tm
        in_specs.append(_resident((None, 2 * fw, d), lambda i: (layer, 0, 0)))
        out_specs[4] = pl.BlockSpec((None, 2 * fw, tm), lambda i: (i // per_seq, 0, i % per_seq))
        out_shape[4] = jax.ShapeDtypeStruct((t // seq_len, 2 * fw, seq_len), F32)
        args = (x, nw, w, w_kvt)
    return pl.pallas_call(
        functools.partial(_inproj_body, fw=fw, sw=sw, gw=gw, q_scale=q_scale, kv_t=kv_t),
        grid=(t // tm,),
        in_specs=in_specs,
        out_specs=out_specs,
        out_shape=out_shape,
        compiler_params=_cparams(("arbitrary",)),
        name="inproj",
    )(*args)


def _log_sigmoid(x):
    return jnp.minimum(x, 0.0) - jnp.log(1.0 + jnp.exp(-jnp.abs(x)))


def _fcum_body(lg_ref, b_ref, lf_ref, cum_ref, *, nblk, pad):
    r = lax.broadcasted_iota(jnp.int32, (LANES, LANES), 0)
    c = lax.broadcasted_iota(jnp.int32, (LANES, LANES), 1)
    upper = (r <= c).astype(F32)
    lane = lax.broadcasted_iota(jnp.int32, (1, LANES), 1)
    carry = jnp.zeros((lg_ref.shape[0], 1), F32)
    for i in range(nblk):
        sl = slice(i * LANES, (i + 1) * LANES)
        lf = _log_sigmoid(lg_ref[:, sl] + b_ref[...])
        lf_ref[:, sl] = lf
        cs = _dot_hi(lf, upper) + carry
        carry = cs[:, LANES - 1:LANES]
        cum_ref[:, sl] = jnp.where(lane + i * LANES < pad, -NEG_BIG, cs) if i * LANES < pad else cs


def _fcum(logits, bias, pad):
    ns, nh, length = logits.shape
    blk = pl.BlockSpec((None, nh, length), lambda s: (s, 0, 0))
    return pl.pallas_call(
        functools.partial(_fcum_body, nblk=length // LANES, pad=pad),
        grid=(ns,),
        in_specs=[blk, pl.BlockSpec((nh, 1), lambda s: (0, 0))],
        out_specs=[blk, blk],
        out_shape=[jax.ShapeDtypeStruct(logits.shape, F32)] * 2,
        compiler_params=_cparams(("arbitrary",)),
        name="fcum",
    )(logits, bias)


def _fox_prompt_body(q_ref, k_ref, v_ref, fk_ref, o_ref, m_sc, l_sc, acc_sc, s_sc, *, tq, hd, n_heads):
    qi = pl.program_id(1)
    lane = lax.broadcasted_iota(jnp.int32, (1, LANES), 1)
    lo = lane < hd
    reps = tq // LANES
    causal = lax.broadcasted_iota(jnp.int32, (tq, tq), 1) <= lax.broadcasted_iota(jnp.int32, (tq, tq), 0)
    for p in range(n_heads * hd // LANES):
        qp = q_ref[:, p * LANES:(p + 1) * LANES]
        qm = (jnp.where(lo, qp, jnp.zeros_like(qp)), jnp.where(lo, jnp.zeros_like(qp), qp))
        m_sc[...] = jnp.full(m_sc.shape, 3.0 * NEG_BIG * LOG2E, F32)
        l_sc[...] = jnp.zeros(l_sc.shape, F32)
        acc_sc[...] = jnp.zeros(acc_sc.shape, F32)

        def scores(j, p=p, qm=qm):
            start = pl.multiple_of(j * tq, tq)
            kb = k_ref[pl.ds(start, tq), p * LANES:(p + 1) * LANES]
            fkb = fk_ref[:, pl.ds(start, tq)] * LOG2E
            return [_dot_nt(qm[hh], kb) - fkb[2 * p + hh:2 * p + hh + 1, :] for hh in range(2)]

        def consume(j, s, masked, p=p):
            vb = v_ref[pl.ds(pl.multiple_of(j * tq, tq), tq), p * LANES:(p + 1) * LANES]
            if masked:
                s = [jnp.where(causal, sh, 2.0 * NEG_BIG) for sh in s]
            m_old = [m_sc[hh] for hh in range(2)]
            m_new = [jnp.maximum(m_old[hh], jnp.max(s[hh], axis=-1, keepdims=True)) for hh in range(2)]
            alpha = [jnp.exp2(m_old[hh] - m_new[hh]) for hh in range(2)]
            e = [jnp.exp2(s[hh] - pltpu.repeat(m_new[hh], reps, 1)) for hh in range(2)]
            pv = [_dot(e[hh].astype(BF16), vb) for hh in range(2)]
            for hh in range(2):
                l_sc[hh] = alpha[hh] * l_sc[hh] + jnp.sum(e[hh], axis=-1, keepdims=True)
                m_sc[hh] = m_new[hh]
            acc_sc[...] = jnp.where(lo, alpha[0], alpha[1]) * acc_sc[...] + jnp.where(lo, pv[0], pv[1])

        s0 = scores(0)
        s_sc[0], s_sc[1] = s0[0], s0[1]

        def body(j, carry):
            s_cur = [s_sc[0], s_sc[1]]
            s_nxt = scores(j + 1)
            consume(j, s_cur, False)
            s_sc[0], s_sc[1] = s_nxt[0], s_nxt[1]
            return carry

        lax.fori_loop(0, qi, body, 0)
        consume(qi, [s_sc[0], s_sc[1]], True)
        inv = 1.0 / jnp.where(lo, l_sc[0], l_sc[1])
        o_ref[:, p * LANES:(p + 1) * LANES] = (acc_sc[...] * inv).astype(BF16)


def _fox_prompt(q, k, v, fcum, nb, lp, n_heads, hd, tq):
    fw = n_heads * hd
    nq = lp // tq
    return pl.pallas_call(
        functools.partial(_fox_prompt_body, tq=tq, hd=hd, n_heads=n_heads),
        grid=(nb, nq),
        in_specs=[pl.BlockSpec((tq, fw), lambda b, i: (b * nq + i, 0)),
                  pl.BlockSpec((lp, fw), lambda b, i: (b, 0)),
                  pl.BlockSpec((lp, fw), lambda b, i: (b, 0)),
                  pl.BlockSpec((None, n_heads, lp), lambda b, i: (b, 0, 0))],
        out_specs=pl.BlockSpec((tq, fw), lambda b, i: (b * nq + i, 0)),
        out_shape=jax.ShapeDtypeStruct((nb * lp, fw), BF16),
        scratch_shapes=[pltpu.VMEM((2, tq, LANES), F32), pltpu.VMEM((2, tq, LANES), F32),
                        pltpu.VMEM((tq, LANES), F32), pltpu.VMEM((2, tq, tq), F32)],
        compiler_params=_cparams(("arbitrary", "arbitrary")),
        name="fox_prompt",
    )(q, k, v, fcum)


def _fox_sample_body(pt_ref, qbd_ref, cn_ref, kn_ref, vn_ref, *rest, g_pages, n_q, n_heads, hd, page):
    kt_refs = rest[:g_pages]
    vt_refs = rest[g_pages:2 * g_pages]
    lf_ref, o_ref, m_sc, l_sc, acc_sc, car_sc = rest[2 * g_pages:]
    g = pl.program_id(1)
    n_pages = pl.num_programs(1) * g_pages
    lfs = [lf_ref[pt_ref[pl.program_id(0), n_pages - 1 - (g * g_pages + i)]] for i in range(g_pages)]
    nr = n_q * n_heads
    fw = n_heads * hd
    qbd = qbd_ref[...]
    rr = lax.broadcasted_iota(jnp.int32, (nr, 1), 0)
    qidx = rr // n_heads
    lane = lax.broadcasted_iota(jnp.int32, (1, LANES), 1)
    col_new = jnp.concatenate([cn_ref[...]] * n_q, axis=0)
    row_new = jnp.sum(jnp.where(lane == qidx, col_new, 0.0), axis=-1, keepdims=True)

    def update(slots, scores, pv_ofs):
        m_old = [m_sc[i] for i in slots]
        m_new = [jnp.maximum(mo, jnp.max(s, axis=-1, keepdims=True)) for mo, s in zip(m_old, scores)]
        alpha = [jnp.exp2(mo - mn) for mo, mn in zip(m_old, m_new)]
        e = [jnp.exp2(s - mn) for s, mn in zip(scores, m_new)]
        pv = [f(ee.astype(BF16)) for f, ee in zip(pv_ofs, e)]
        for n, i in enumerate(slots):
            l_sc[i] = alpha[n] * l_sc[i] + jnp.sum(e[n], axis=-1, keepdims=True)
            m_sc[i] = m_new[n]
            acc_sc[i] = alpha[n] * acc_sc[i] + pv[n]

    @pl.when(g == 0)
    def _():
        m_sc[...] = jnp.full(m_sc.shape, NEG_BIG, F32)
        l_sc[...] = jnp.zeros(l_sc.shape, F32)
        acc_sc[...] = jnp.zeros(acc_sc.shape, F32)
        car_sc[...] = jnp.zeros(car_sc.shape, F32)
        s = _dot_nt(qbd, kn_ref[...]) + (row_new - col_new) * LOG2E
        s = jnp.where((lane <= qidx) & (lane < n_q), s, NEG_BIG)
        update([0], [s], [lambda e: _dot(e, vn_ref[...])])

    r = lax.broadcasted_iota(jnp.int32, (page, page), 0)
    c = lax.broadcasted_iota(jnp.int32, (page, page), 1)
    later = (r > c).astype(F32)
    suffix = _dot_hi(jnp.concatenate(lfs, axis=0), later)
    carry = car_sc[...]
    scores, pv_ofs = [], []
    for i in range(g_pages):
        bias = jnp.concatenate([suffix[i * n_heads:(i + 1) * n_heads] + carry] * n_q, axis=0) + row_new
        kt = kt_refs[i][...].reshape(fw, page).astype(BF16)
        scores.append(_dot(qbd, kt) + bias * LOG2E)
        vt = vt_refs[i][...].reshape(fw, page).astype(BF16)
        pv_ofs.append(lambda e, vt=vt: _dot_nt(e, vt))
        carry = carry + jnp.sum(lfs[i], axis=-1, keepdims=True)
    car_sc[...] = carry
    update(list(range(g_pages)), scores, pv_ofs)

    @pl.when(g == pl.num_programs(1) - 1)
    def _():
        ms = [m_sc[i] for i in range(g_pages)]
        m_all = functools.reduce(jnp.maximum, ms)
        wts = [jnp.exp2(mi - m_all) for mi in ms]
        l_all = sum(w * l_sc[i] for i, w in enumerate(wts))
        o = sum(w * acc_sc[i] for i, w in enumerate(wts)) / l_all
        lane_w = lax.broadcasted_iota(jnp.int32, (1, fw), 1)
        o = jnp.where(lane_w // hd == rr % n_heads, o, 0.0)
        o_ref[...] = jnp.sum(o.reshape(n_q, n_heads, fw), axis=1).astype(BF16)


def _fox_sample(page_table, qbd, cum_new, k_new, v_new, kt, vt, lft, layer, g_pages, n_q):
    nb, n_pages = page_table.shape
    _, _, n_heads, hd, page = kt.shape
    fw = n_heads * hd
    nr = n_q * n_heads
    n_groups = n_pages // g_pages

    def page_map(i):
        return lambda b, g, pt: (layer, pt[b, n_pages - 1 - (g * g_pages + i)], 0, 0, 0)

    per_b = lambda shape: pl.BlockSpec((None,) + shape, lambda b, g, pt: (b, 0, 0))
    in_specs = ([per_b((nr, fw)), per_b((n_heads, LANES)), per_b((LANES, fw)), per_b((LANES, fw))]
                + [pl.BlockSpec((None, None, n_heads, hd, page), page_map(i)) for i in range(g_pages)]
                + [pl.BlockSpec((None, None, n_heads, hd, page), page_map(i)) for i in range(g_pages)]
                + [_resident((None,) + lft.shape[1:], lambda b, g, pt: (layer, 0, 0, 0))])
    return pl.pallas_call(
        functools.partial(_fox_sample_body, g_pages=g_pages, n_q=n_q, n_heads=n_heads, hd=hd, page=page),
        grid_spec=pltpu.PrefetchScalarGridSpec(
            num_scalar_prefetch=1, grid=(nb, n_groups), in_specs=in_specs,
            out_specs=per_b((n_q, fw)),
            scratch_shapes=[pltpu.VMEM((g_pages, nr, 1), F32), pltpu.VMEM((g_pages, nr, 1), F32),
                            pltpu.VMEM((g_pages, nr, fw), F32), pltpu.VMEM((n_heads, 1), F32)]),
        out_shape=jax.ShapeDtypeStruct((nb, n_q, fw), BF16),
        compiler_params=_cparams(("arbitrary", "arbitrary")),
        name="fox_sample",
    )(page_table, qbd, cum_new, k_new, v_new, *([kt] * g_pages), *([vt] * g_pages), lft)


def _gelu_tanh(x):
    return 0.5 * x * (1.0 + jnp.tanh(math.sqrt(2.0 / math.pi) * (x + 0.044715 * x * x * x)))


def _s5_tables(a_re, a_im, b_re, b_im, c_re, c_im, d_skip, log_dt, n_last):
    ck = S5_CHUNK
    g, p, gs = b_re.shape
    gt = LANES // gs
    nt = g // gt
    dt = jnp.exp(log_dt)[:, None]
    mag_l, ang = a_re * dt, a_im * dt

    def apow_fn(n):
        m = jnp.exp(mag_l * n)
        return m * jnp.cos(ang * n), m * jnp.sin(ang * n)

    ab_re, ab_im = apow_fn(1.0)
    den = a_re * a_re + a_im * a_im
    cf_re = ((ab_re - 1.0) * a_re + ab_im * a_im) / den
    cf_im = (ab_im * a_re - (ab_re - 1.0) * a_im) / den
    bb_re = cf_re[..., None] * b_re - cf_im[..., None] * b_im
    bb_im = cf_re[..., None] * b_im + cf_im[..., None] * b_re
    pw_re, pw_im = jax.vmap(apow_fn)(jnp.arange(ck + 1, dtype=F32))
    cab_re = c_re[None] * pw_re[:, :, None, :] - c_im[None] * pw_im[:, :, None, :]
    cab_im = c_re[None] * pw_im[:, :, None, :] + c_im[None] * pw_re[:, :, None, :]
    w = (jnp.einsum('lgcp,gpd->lgcd', cab_re[:ck], bb_re, precision=HI)
         - jnp.einsum('lgcp,gpd->lgcd', cab_im[:ck], bb_im, precision=HI))
    w = w.at[0].add(jax.vmap(jnp.diag)(d_skip))
    eye = jnp.eye(gt, dtype=F32)
    wt = jnp.transpose(w, (1, 3, 0, 2)).reshape(nt, gt, gs, ck, gs)
    wcat = jnp.einsum('jgdlc,gh->jgdlhc', wt, eye).reshape(nt, gt * gs, ck * gt * gs)

    e_pow = ck - 1 - jnp.arange(ck)
    f_re = pw_re[e_pow][..., None] * bb_re[None] - pw_im[e_pow][..., None] * bb_im[None]
    f_im = pw_re[e_pow][..., None] * bb_im[None] + pw_im[e_pow][..., None] * bb_re[None]

    def f_tile(f):
        ft = jnp.transpose(f, (1, 0, 3, 2)).reshape(nt, gt, ck, gs, p)
        return jnp.einsum('jgsdp,gh->jsgdhp', ft, eye).reshape(nt, ck, gt * gs, gt * p)

    fmat = jnp.concatenate([f_tile(f_re), f_tile(f_im)], axis=-1)

    def e_tile(m):
        et = jnp.transpose(m, (1, 0, 3, 2)).reshape(nt, gt, ck, p, gs)
        return jnp.einsum('jgtpc,gh->jtgphc', et, eye).reshape(nt, ck, gt * p, gt * gs)

    emat = jnp.concatenate([e_tile(cab_re[1:]), e_tile(-cab_im[1:])], axis=2)

    tile_vec = lambda v: v.reshape(nt, gt * p)
    nlev = 16
    lev = [apow_fn(float(ck * 2 ** i)) for i in range(nlev)]
    apow = jnp.stack([jnp.stack([tile_vec(r), tile_vec(i)], axis=1) for r, i in lev], axis=1)
    al_re, al_im = apow_fn(float(n_last))
    alast = jnp.stack([tile_vec(al_re), tile_vec(al_im)], axis=1)
    return wcat.astype(BF16), fmat.astype(BF16), emat.astype(BF16), apow, alast


def _s5_local_body(u_ref, w_ref, f_ref, y_ref, z_ref, acc_ref, *, ck, n):
    accz = None
    for s in range(ck):
        us = u_ref[pl.ds(s, n, stride=ck), :].astype(BF16)
        y = _dot(us, w_ref[:, :(ck - s) * LANES])
        if s == 0:
            acc_ref[...] = y
        else:
            acc_ref[:, s * LANES:] += y
        z = _dot(us, f_ref[s])
        accz = z if accz is None else accz + z
    for t in range(ck):
        y_ref[pl.ds(t, n, stride=ck), :] = acc_ref[:, t * LANES:(t + 1) * LANES]
    z_ref[...] = accz


def _s5_local(u, wcat, fmat, rows_per_step):
    r, width = u.shape
    ck = S5_CHUNK
    nt = width // LANES
    sw = fmat.shape[3]
    n = rows_per_step // ck
    return pl.pallas_call(
        functools.partial(_s5_local_body, ck=ck, n=n),
        grid=(nt, r // rows_per_step),
        in_specs=[pl.BlockSpec((rows_per_step, LANES), lambda j, i: (i, j)),
                  pl.BlockSpec((None, LANES, ck * LANES), lambda j, i: (j, 0, 0)),
                  pl.BlockSpec((None, ck, LANES, sw), lambda j, i: (j, 0, 0, 0))],
        out_specs=[pl.BlockSpec((rows_per_step, LANES), lambda j, i: (i, j)),
                   pl.BlockSpec((None, n, sw), lambda j, i: (j, i, 0))],
        out_shape=[jax.ShapeDtypeStruct((r, width), F32), jax.ShapeDtypeStruct((nt, r // ck, sw), F32)],
        scratch_shapes=[pltpu.VMEM((n, ck * LANES), F32)],
        compiler_params=_cparams(("arbitrary", "arbitrary")),
        name="s5_local",
    )(u, wcat, fmat)


def _s5_scan_body(z_ref, ap_ref, xs_ref, xf_ref, *, nc, half):
    xr = z_ref[:, :half]
    xi = z_ref[:, half:]
    row = lax.broadcasted_iota(jnp.int32, (nc, 1), 0)
    lev, s = 0, 1
    while s < nc:
        ar = ap_ref[lev, 0:1, :]
        ai = ap_ref[lev, 1:2, :]
        keep = row >= s
        sr = jnp.where(keep, pltpu.roll(xr, s, 0), 0.0)
        si = jnp.where(keep, pltpu.roll(xi, s, 0), 0.0)
        xr, xi = xr + ar * sr - ai * si, xi + ar * si + ai * sr
        lev, s = lev + 1, s * 2
    xf_ref[:, :half] = xr[nc - 1:nc, :]
    xf_ref[:, half:] = xi[nc - 1:nc, :]
    xs_ref[:, :half] = jnp.where(row >= 1, pltpu.roll(xr, 1, 0), 0.0)
    xs_ref[:, half:] = jnp.where(row >= 1, pltpu.roll(xi, 1, 0), 0.0)


def _s5_scan(z, apow, nb):
    nt, n, sw = z.shape
    nc = n // nb
    nlev = apow.shape[1]
    return pl.pallas_call(
        functools.partial(_s5_scan_body, nc=nc, half=sw // 2),
        grid=(nt, nb),
        in_specs=[pl.BlockSpec((None, nc, sw), lambda j, b: (j, b, 0)),
                  pl.BlockSpec((None, nlev, 2, sw // 2), lambda j, b: (j, 0, 0, 0))],
        out_specs=[pl.BlockSpec((None, nc, sw), lambda j, b: (j, b, 0)),
                   pl.BlockSpec((None, None, 1, sw), lambda j, b: (j, b, 0, 0))],
        out_shape=[jax.ShapeDtypeStruct((nt, n, sw), F32), jax.ShapeDtypeStruct((nt, nb, 1, sw), F32)],
        compiler_params=_cparams(("arbitrary", "arbitrary")),
        name="s5_scan",
    )(z, apow)


def _s5_out_body(y_ref, xs_ref, e_ref, o_ref, act_ref, *, ck, n, first):
    xs = xs_ref[...].astype(BF16)
    for t in range(ck):
        rows = pl.ds(t, n, stride=ck)
        y = y_ref[rows, :]
        if t >= first:
            y = y + _dot(xs, e_ref[t - first])
        act_ref[rows, :] = _gelu_tanh(y)
    o_ref[...] = act_ref[...].astype(BF16)


def _s5_out(ylocal, xstart, emat, rows_per_step, first):
    r, width = ylocal.shape
    ck = S5_CHUNK
    nt = width // LANES
    sw = xstart.shape[2]
    n = rows_per_step // ck
    return pl.pallas_call(
        functools.partial(_s5_out_body, ck=ck, n=n, first=first),
        grid=(nt, r // rows_per_step),
        in_specs=[pl.BlockSpec((rows_per_step, LANES), lambda j, i: (i, j)),
                  pl.BlockSpec((None, n, sw), lambda j, i: (j, i, 0)),
                  pl.BlockSpec((None, ck, sw, LANES), lambda j, i: (j, 0, 0, 0))],
        out_specs=pl.BlockSpec((rows_per_step, LANES), lambda j, i: (i, j)),
        out_shape=jax.ShapeDtypeStruct((r, width), BF16),
        scratch_shapes=[pltpu.VMEM((rows_per_step, LANES), F32)],
        compiler_params=_cparams(("arbitrary", "arbitrary")),
        name="s5_out",
    )(ylocal, xstart, emat)


def _s5_step_body(z_ref, x0_ref, al_ref, xf_ref, *, half):
    ar, ai = al_ref[0:1, :], al_ref[1:2, :]
    xr, xi = x0_ref[:, :half], x0_ref[:, half:]
    xf_ref[:, :half] = z_ref[:, :half] + ar * xr - ai * xi
    xf_ref[:, half:] = z_ref[:, half:] + ar * xi + ai * xr


def _s5_step(z, x0, alast):
    nt, n, sw = z.shape
    blk = pl.BlockSpec((None, n, sw), lambda j: (j, 0, 0))
    return pl.pallas_call(
        functools.partial(_s5_step_body, half=sw // 2),
        grid=(nt,),
        in_specs=[blk, blk, pl.BlockSpec((None, 2, sw // 2), lambda j: (j, 0, 0))],
        out_specs=blk,
        out_shape=jax.ShapeDtypeStruct(z.shape, F32),
        compiler_params=_cparams(("arbitrary",)),
        name="s5_step",
    )(z, x0, alast)


def _gdn_body(x_ref, z_ref, sm_ref, ci_ref, s0_ref, cw_ref, na_ref, db_ref, nw_ref, o_ref, sf_ref, s_sc, car_sc,
              *, c, sb, n_valid, n_heads, hd, kconv, a_lane, b_lane):
    ci = pl.program_id(1)
    gw = n_heads * hd

    @pl.when(ci == 0)
    def _():
        s_sc[...] = s0_ref[...]
        car_sc[...] = ci_ref[...]

    r = lax.broadcasted_iota(jnp.int32, (c, c), 0)
    cc = lax.broadcasted_iota(jnp.int32, (c, c), 1)
    incl = r >= cc
    strict = r > cc
    eye = (r == cc).astype(F32)
    row = ci * c + lax.broadcasted_iota(jnp.int32, (c, 1), 0)
    hist = car_sc.shape[1]
    chains = []
    for b in range(sb):
        x = x_ref[b]
        xc = jnp.concatenate([car_sc[b], x], axis=0)
        y = None
        for j in range(kconv):
            off = hist - (kconv - 1) + j
            term = xc[off:off + c, :] * cw_ref[j:j + 1, :]
            y = term if y is None else y + term
        car_sc[b] = x[c - hist:, :]
        qkv = y * _sigmoid(y)

        sm = sm_ref[b]
        sp_in = sm + db_ref[...]
        softplus = jnp.maximum(sp_in, 0.0) + jnp.log(1.0 + jnp.exp(-jnp.abs(sp_in)))
        gfull = na_ref[...] * softplus
        beta_full = _sigmoid(sm)
        if n_valid is not None:
            live = row < n_valid
            gfull = jnp.where(live, gfull, 0.0)
            beta_full = jnp.where(live, beta_full, 0.0)
            qkv = jnp.where(live, qkv, 0.0)
        cum_full = _dot_hi(incl.astype(F32), gfull)
        cum_t = cum_full.T
        for h in range(n_heads):
            q = qkv[:, h * hd:(h + 1) * hd]
            k = qkv[:, gw + h * hd:gw + (h + 1) * hd]
            v = qkv[:, 2 * gw + h * hd:2 * gw + (h + 1) * hd]
            q = q * lax.rsqrt(jnp.sum(q * q, axis=-1, keepdims=True) + 1e-6) * (hd ** -0.5)
            k = k * lax.rsqrt(jnp.sum(k * k, axis=-1, keepdims=True) + 1e-6)
            cum = cum_full[:, a_lane + h:a_lane + h + 1]
            cum_r = cum_t[a_lane + h:a_lane + h + 1, :]
            beta = beta_full[:, b_lane + h:b_lane + h + 1]
            decay = jnp.where(incl, jnp.exp(jnp.where(incl, cum - cum_r, 0.0)), 0.0)
            kb = k * beta
            k16 = k.astype(BF16)
            ecum = jnp.exp(cum)
            g_last = cum[c - 1:c, :]
            chains.append(dict(
                b=b, h=h, k16=k16, decay=decay, qe16=(q * ecum).astype(BF16), q16=q.astype(BF16),
                rhs=jnp.concatenate([v * beta, kb * ecum], axis=-1), kb16=kb.astype(BF16),
                kd=k * jnp.exp(g_last - cum), s_scale=jnp.exp(g_last)))
    for ch in chains:
        ch['lower'] = jnp.where(strict, _dot_nt(ch['kb16'], ch['k16']) * ch['decay'], 0.0)
        ch['aqk16'] = jnp.where(incl, _dot_nt(ch['q16'], ch['k16']) * ch['decay'], 0.0).astype(BF16)
    for ch in chains:
        ch['t_inv'] = eye - ch['lower']
        ch['pw'] = _dot3(ch['lower'], ch['lower'])
    span = 2
    while span < c:
        span *= 2
        for ch in chains:
            if span < c:
                both = _dot3(ch['pw'], jnp.concatenate([ch['t_inv'], ch['pw']], axis=1))
                ch['t_inv'] = ch['t_inv'] + both[:, :c]
                ch['pw'] = both[:, c:]
            else:
                ch['t_inv'] = ch['t_inv'] + _dot3(ch['pw'], ch['t_inv'])
    for ch in chains:
        ch['sol'] = _dot3(ch['t_inv'], ch['rhs'])
    for ch in chains:
        s_h = s_sc[ch['b'], ch['h']]
        s16 = s_h.astype(BF16)
        v_new = ch['sol'][:, :hd] - _dot(ch['sol'][:, hd:].astype(BF16), s16)
        ch['vn16'] = v_new.astype(BF16)
        ch['o'] = _dot(ch['qe16'], s16)
        ch['s_dec'] = s_h * ch['s_scale']
    for ch in chains:
        b, h = ch['b'], ch['h']
        o = ch['o'] + _dot(ch['aqk16'], ch['vn16'])
        s_sc[b, h] = ch['s_dec'] + _dot(ch['kd'].T.astype(BF16), ch['vn16'])
        o = o * lax.rsqrt(jnp.mean(o * o, axis=-1, keepdims=True) + RMS_EPS) * nw_ref[...]
        zz = z_ref[b, :, h * hd:(h + 1) * hd]
        o_ref[b, :, h * hd:(h + 1) * hd] = (o * (zz * _sigmoid(zz))).astype(BF16)

    @pl.when(ci == pl.num_programs(1) - 1)
    def _():
        sf_ref[...] = s_sc[...]


def _gdn(x, z, sm, conv_init, s0, conv_w, neg_a, dt_b, norm_w, sb, n_valid, a_lane, b_lane):
    ns, length, w3 = x.shape
    n_heads, hd = s0.shape[1], s0.shape[2]
    gw = n_heads * hd
    c = GDN_CHUNK
    kconv = conv_w.shape[0]
    hist = conv_init.shape[1]
    rowblk = lambda width: pl.BlockSpec((sb, c, width), lambda s, i: (s, i, 0))
    const = lambda shape: pl.BlockSpec(shape, lambda s, i: (0,) * len(shape))
    return pl.pallas_call(
        functools.partial(_gdn_body, c=c, sb=sb, n_valid=n_valid, n_heads=n_heads, hd=hd, kconv=kconv,
                          a_lane=a_lane, b_lane=b_lane),
        grid=(ns // sb, length // c),
        in_specs=[rowblk(w3), rowblk(gw), rowblk(LANES),
                  pl.BlockSpec((sb, hist, w3), lambda s, i: (s, 0, 0)),
                  pl.BlockSpec((sb, n_heads, hd, hd), lambda s, i: (s, 0, 0, 0)),
                  const((kconv, w3)), const((1, LANES)), const((1, LANES)), const((1, hd))],
        out_specs=[rowblk(gw), pl.BlockSpec((sb, n_heads, hd, hd), lambda s, i: (s, 0, 0, 0))],
        out_shape=[jax.ShapeDtypeStruct((ns, length, gw), BF16), jax.ShapeDtypeStruct(s0.shape, F32)],
        scratch_shapes=[pltpu.VMEM((sb, n_heads, hd, hd), F32), pltpu.VMEM((sb, hist, w3), F32)],
        compiler_params=_cparams(("arbitrary", "arbitrary")),
        name="gdn",
    )(x, z, sm, conv_init, s0, conv_w, neg_a, dt_b, norm_w)


def _merge_body(x_ref, h_ref, of_ref, z5_ref, og_ref, wf_ref, w5_ref, wg_ref, wgate_ref, wo_ref, o_ref, *, d):
    fox_br = _dot(of_ref[...], wf_ref[...])
    glu = _dot(z5_ref[...], w5_ref[...])
    s5_br = glu[:, :d] * _sigmoid(glu[:, d:])
    gdn_br = _dot(og_ref[...], wg_ref[...])
    h = h_ref[...]
    merged = (_sigmoid(_dot(h, wgate_ref[:, :d])) * fox_br
              + _sigmoid(_dot(h, wgate_ref[:, d:2 * d])) * s5_br
              + _sigmoid(_dot(h, wgate_ref[:, 2 * d:])) * gdn_br)
    o_ref[...] = x_ref[...] + _dot(merged.astype(BF16), wo_ref[...])


def _merge(x, h, o_fox, z5, o_gdn, wf, w5, wg, wgate, wo, layer):
    t, d = x.shape
    tm = _row_tile(t)
    row = lambda a: pl.BlockSpec((tm, a.shape[1]), lambda i: (i, 0))
    res = lambda a: _resident((None,) + a.shape[1:], lambda i: (layer, 0, 0))
    return pl.pallas_call(
        functools.partial(_merge_body, d=d),
        grid=(t // tm,),
        in_specs=[row(x), row(h), row(o_fox), row(z5), row(o_gdn), res(wf), res(w5), res(wg), res(wgate), res(wo)],
        out_specs=pl.BlockSpec((tm, d), lambda i: (i, 0)),
        out_shape=jax.ShapeDtypeStruct((t, d), F32),
        compiler_params=_cparams(("arbitrary",)),
        name="merge",
    )(x, h, o_fox, z5, o_gdn, wf, w5, wg, wgate, wo)


def _final_norm_body(x_ref, w_ref, o_ref):
    x = x_ref[...]
    ms = jnp.mean(x * x, axis=-1, keepdims=True)
    o_ref[...] = x * lax.rsqrt(ms + RMS_EPS) * w_ref[...]


def _final_norm(x, w):
    t, d = x.shape
    tm = _row_tile(t)
    return pl.pallas_call(
        _final_norm_body,
        grid=(t // tm,),
        in_specs=[pl.BlockSpec((tm, d), lambda i: (i, 0)), pl.BlockSpec((1, d), lambda i: (0, 0))],
        out_specs=pl.BlockSpec((tm, d), lambda i: (i, 0)),
        out_shape=jax.ShapeDtypeStruct((t, d), F32),
        compiler_params=_cparams(("arbitrary",)),
        name="final_norm",
    )(x, w)


def kernel(x_prompt, x_sample, cache_fox_k, cache_fox_v, cache_fox_logf, state_s5_re, state_s5_im, state_gdn, state_gdn_conv, page_table, meta_tokens, norm_ffn1, w_ffn1_in, w_ffn1_out, norm_mix, w_in, fox_b_f, w_fox_br, s5_A_re, s5_A_im, s5_B_re, s5_B_im, s5_C_re, s5_C_im, s5_D, s5_log_dt, w_s5_glu, gdn_conv_w, gdn_A_log, gdn_dt_bias, gdn_norm, w_gdn_br, w_out, norm_ffn2, w_ffn2_in, w_ffn2_out, norm_final):
    nb, seq, d = x_prompt.shape
    ds_b, ds_t, _ = x_sample.shape
    depth = w_in.shape[0]
    n_meta = meta_tokens.shape[0]
    fh, fhd = cache_fox_k.shape[3], cache_fox_k.shape[4]
    fw = fh * fhd
    page = cache_fox_k.shape[2]
    sg, sp, sgs = s5_B_re.shape[1:]
    sw = sg * sgs
    gh, ghd = state_gdn.shape[2], state_gdn.shape[3]
    gw = gh * ghd
    kconv = gdn_conv_w.shape[1]
    ck = S5_CHUNK
    gc = GDN_CHUNK
    assert fw % LANES == 0 and 2 * fhd == LANES and ghd == LANES and sw % LANES == 0 and LANES % sgs == 0
    assert ds_t <= ck and ds_t >= kconv - 1 and ds_t <= gc
    assert page == LANES and fh + 2 * gh <= LANES

    lreal = n_meta + seq
    pad = (-lreal) % LANES
    lp = lreal + pad
    tp = nb * lp
    ts = ds_b * ds_t
    tq = max(c for c in (128, 256, 384, 512) if lp % c == 0)
    n_pages = page_table.shape[1]
    g_pages = max(c for c in (1, 2, 4, 8) if n_pages % c == 0)
    hist = 8
    nt = sw // LANES
    gt = LANES // sgs
    ssw = 2 * gt * sp

    sizes = (fw, fw, fw, fh, sw, 3 * gw, gh, gh, gw, 3 * d)
    offs = [0]
    for s in sizes:
        offs.append(offs[-1] + s)
    col = lambda i: w_in[:, :, offs[i]:offs[i + 1]]
    small = jnp.concatenate([col(3), col(6), col(7),
                             jnp.zeros((depth, d, LANES - fh - 2 * gh), F32)], axis=2)
    a_lane, b_lane = fh, fh + gh
    w_proj = jnp.concatenate([col(0), col(1), col(2), col(4), col(5), col(8), small], axis=2).astype(BF16)
    w_gate = col(9).astype(BF16)
    w_kvt = jnp.transpose(w_in[:, :, offs[1]:offs[3]], (0, 2, 1)).astype(BF16)
    q_scale = fhd ** -0.5 * LOG2E
    wi1, wo1 = w_ffn1_in.astype(BF16), w_ffn1_out.astype(BF16)
    wi2, wo2 = w_ffn2_in.astype(BF16), w_ffn2_out.astype(BF16)
    wf16, w516, wg16, wo16 = (w.astype(BF16) for w in (w_fox_br, w_s5_glu, w_gdn_br, w_out))
    nrm = lambda w: w.reshape(depth, 1, d)
    n1, nm, n2 = nrm(norm_ffn1), nrm(norm_mix), nrm(norm_ffn2)
    lane_vec = lambda v, at: jnp.zeros((depth, 1, LANES), F32).at[:, 0, at:at + v.shape[1]].set(v)
    neg_a = lane_vec(-jnp.exp(gdn_A_log), a_lane)
    dt_b = lane_vec(gdn_dt_bias, a_lane)
    s5_tab = [_s5_tables(s5_A_re[l], s5_A_im[l], s5_B_re[l], s5_B_im[l], s5_C_re[l], s5_C_im[l], s5_D[l],
                         s5_log_dt[l], ds_t) for l in range(depth)]

    kt_cache = jnp.transpose(cache_fox_k, (0, 1, 3, 4, 2))
    vt_cache = jnp.transpose(cache_fox_v, (0, 1, 3, 4, 2))
    lf_cache = jnp.transpose(cache_fox_logf, (0, 1, 3, 2))

    meta = jnp.broadcast_to(meta_tokens[None].astype(F32), (nb, n_meta, d))
    xp = jnp.concatenate([jnp.zeros((nb, pad, d), F32), meta, x_prompt], axis=1).reshape(tp, d)
    xs = x_sample.reshape(ts, d)

    zeros_conv = jnp.zeros((nb, hist, 3 * gw), F32)
    zeros_gdn = jnp.zeros((nb, gh, ghd, ghd), F32)
    sb_p = 2 if nb % 2 == 0 else 1
    sb_s = 2 if ds_b % 2 == 0 else 1
    head_eye = jnp.eye(fh, dtype=BF16).reshape(1, 1, fh, fh, 1)
    to_tile = lambda st: jnp.transpose(st.reshape(ds_b, nt, gt * sp), (1, 0, 2))
    from_tile = lambda a, n: jnp.transpose(a, (1, 0, 2)).reshape(n, sg, sp)
    outs = [[] for _ in range(14)]

    for l in range(depth):
        xp = _ffn(xp, n1, wi1, wo1, l)
        xs = _ffn(xs, n1, wi1, wo1, l)
        hp, qp, kp, vp, kvp, up, gqp, zp, smp = _inproj(xp, nm, w_proj, l, fw, sw, gw, q_scale, w_kvt, lp)
        hs, qs, ks, vs, kvs, us, gqs, zs, sms = _inproj(xs, nm, w_proj, l, fw, sw, gw, q_scale)
        bias_f = fox_b_f[l].reshape(fh, 1)

        lg_p = jnp.transpose(smp[:, :fh].reshape(nb, lp, fh), (0, 2, 1))
        lf_p, cum_p = _fcum(lg_p, bias_f, pad)
        o_fox_p = _fox_prompt(qp, kp, vp, cum_p, nb, lp, fh, fhd, tq)
        lg_s = jnp.transpose(sms[:, :fh].reshape(ds_b, ds_t, fh), (0, 2, 1))
        lg_s = jnp.pad(lg_s, ((0, 0), (0, 0), (0, LANES - ds_t)))
        lf_s, cum_s = _fcum(lg_s, bias_f, 0)
        qbd = (qs.reshape(ds_b, ds_t, 1, fh, fhd) * head_eye).reshape(ds_b, ds_t * fh, fw)
        rows_pad = ((0, 0), (0, LANES - ds_t), (0, 0))
        k_new = jnp.pad(ks.reshape(ds_b, ds_t, fw), rows_pad)
        v_new = jnp.pad(vs.reshape(ds_b, ds_t, fw), rows_pad)
        o_fox_s = _fox_sample(page_table, qbd, cum_s, k_new, v_new, kt_cache, vt_cache, lf_cache, l, g_pages, ds_t)

        wcat, fmat, emat, apow, alast = s5_tab[l]
        y_loc, z_loc = _s5_local(up, wcat, fmat, lp)
        x_start, x_fin = _s5_scan(z_loc, apow, nb)
        z5_p = _s5_out(y_loc, x_start, emat, lp, 0)
        us_c = jnp.pad(us.reshape(ds_b, ds_t, sw), ((0, 0), (ck - ds_t, 0), (0, 0))).reshape(ds_b * ck, sw)
        x0 = jnp.concatenate([to_tile(state_s5_re[l]), to_tile(state_s5_im[l])], axis=2)
        y_loc_s, z_loc_s = _s5_local(us_c, wcat, fmat, ds_b * ck)
        z5_s = _s5_out(y_loc_s, x0, emat, ds_b * ck, ck - ds_t)
        z5_s = z5_s.reshape(ds_b, ck, sw)[:, ck - ds_t:].reshape(ts, sw)
        xf_s = _s5_step(z_loc_s, x0, alast)

        cw = gdn_conv_w[l]
        gnorm = gdn_norm[l].reshape(1, ghd)
        o_gdn_p, s_p = _gdn(gqp.reshape(nb, lp, 3 * gw), zp.reshape(nb, lp, gw), smp.reshape(nb, lp, LANES),
                            zeros_conv, zeros_gdn, cw, neg_a[l], dt_b[l], gnorm, sb_p, None, a_lane, b_lane)
        seq_pad = lambda a: jnp.pad(a.reshape(ds_b, ds_t, a.shape[1]), ((0, 0), (0, gc - ds_t), (0, 0)))
        conv_s = jnp.pad(state_gdn_conv[l], ((0, 0), (hist - (kconv - 1), 0), (0, 0)))
        o_gdn_s, s_s = _gdn(seq_pad(gqs), seq_pad(zs), seq_pad(sms), conv_s, state_gdn[l], cw, neg_a[l], dt_b[l],
                            gnorm, sb_s, ds_t, a_lane, b_lane)

        xp = _merge(xp, hp, o_fox_p, z5_p, o_gdn_p.reshape(tp, gw), wf16, w516, wg16, w_gate, wo16, l)
        xs = _merge(xs, hs, o_fox_s.reshape(ts, fw), z5_s, o_gdn_s[:, :ds_t].reshape(ts, gw),
                    wf16, w516, wg16, w_gate, wo16, l)
        xp = _ffn(xp, n2, wi2, wo2, l)
        xs = _ffn(xs, n2, wi2, wo2, l)

        kv_p = jnp.transpose(kvp.reshape(nb, 2, fh, fhd, lp)[..., pad:], (0, 4, 1, 2, 3))
        kv_s = kvs.reshape(ds_b, ds_t, 2, fh, fhd)
        xf_p = x_fin.reshape(nt, nb, ssw)
        gq_s = jnp.concatenate([state_gdn_conv[l], gqs.reshape(ds_b, ds_t, 3 * gw)], axis=1)
        new = (kv_p[:, :, 0], kv_p[:, :, 1], jnp.transpose(lf_p[:, :, pad:], (0, 2, 1)),
               kv_s[:, :, 0], kv_s[:, :, 1], jnp.transpose(lf_s[:, :, :ds_t], (0, 2, 1)),
               from_tile(xf_p[:, :, :ssw // 2], nb), from_tile(xf_p[:, :, ssw // 2:], nb),
               from_tile(xf_s[:, :, :ssw // 2], ds_b), from_tile(xf_s[:, :, ssw // 2:], ds_b),
               s_p, s_s, gqp.reshape(nb, lp, 3 * gw)[:, lp - (kconv - 1):], gq_s[:, -(kconv - 1):])
        for i, a in enumerate(new):
            outs[i].append(a)

    y_prompt = _final_norm(xp, norm_final.reshape(1, d)).reshape(nb, lp, d)[:, pad + n_meta:]
    y_sample = _final_norm(xs, norm_final.reshape(1, d)).reshape(ds_b, ds_t, d)
    return (y_prompt, y_sample) + tuple(jnp.stack(o) for o in outs)
```

```python
import functools
import math

import jax
import jax.numpy as jnp
from jax import lax
from jax.experimental import pallas as pl
from jax.experimental.pallas import tpu as pltpu

F32 = jnp.float32
BF16 = jnp.bfloat16
RMS_EPS = 1e-6
NEG_BIG = -1e30
LANES = 128
VMEM_LIMIT = 56 * 1024 * 1024
S5_CHUNK = 16
GDN_CHUNK = 128
HI = lax.Precision.HIGHEST
LOG2E = math.log2(math.e)


def _cparams(sem):
    return pltpu.CompilerParams(dimension_semantics=sem, vmem_limit_bytes=VMEM_LIMIT)


def _resident(shape, index_map):
    return pl.BlockSpec(shape, index_map, pipeline_mode=pl.Buffered(1))


def _dot(a, b):
    return jnp.dot(a, b, preferred_element_type=F32)


def _dot_hi(a, b):
    return jnp.dot(a, b, preferred_element_type=F32, precision=HI)


def _dot_nt(a, b):
    return lax.dot_general(a, b, (((1,), (1,)), ((), ())), preferred_element_type=F32)


def _split(x):
    hi = x.astype(BF16)
    return hi, (x - hi.astype(F32)).astype(BF16)


def _dot3s(a_hi, a_lo, b_hi, b_lo):
    return _dot(jnp.concatenate([a_hi, a_lo, a_hi], axis=1), jnp.concatenate([b_hi, b_hi, b_lo], axis=0))


def _dot3(a, b):
    return _dot3s(*_split(a), *_split(b))


def _rms_bf16(x, w):
    ms = jnp.mean(x * x, axis=-1, keepdims=True)
    return (x * lax.rsqrt(ms + RMS_EPS) * w).astype(BF16)


def _sigmoid(x):
    return 1.0 / (1.0 + jnp.exp(-x))


def _row_tile(t, cap=1024):
    best = 8
    for tm in range(8, min(t, cap) + 1, 8):
        if t % tm == 0:
            best = tm
    return best


def _ffn_body(x_ref, nw_ref, wi_ref, wo_ref, o_ref, act_ref, *, d_ff, fc):
    x = x_ref[...]
    h = _rms_bf16(x, nw_ref[...])
    for c in range(d_ff // fc):
        g = _dot(h, wi_ref[:, c * fc:(c + 1) * fc])
        u = _dot(h, wi_ref[:, d_ff + c * fc:d_ff + (c + 1) * fc])
        act_ref[:, c * fc:(c + 1) * fc] = (g * _sigmoid(g) * u).astype(BF16)
    o_ref[...] = x + 0.5 * _dot(act_ref[...], wo_ref[...])


def _ffn(x, nw, wi, wo, layer):
    t, d = x.shape
    tm = _row_tile(t)
    d_ff = wo.shape[1]
    fc = 256 if d_ff % 256 == 0 else d_ff
    return pl.pallas_call(
        functools.partial(_ffn_body, d_ff=d_ff, fc=fc),
        grid=(t // tm,),
        in_specs=[pl.BlockSpec((tm, d), lambda i: (i, 0)),
                  _resident((None, 1, d), lambda i: (layer, 0, 0)),
                  _resident((None, d, 2 * d_ff), lambda i: (layer, 0, 0)),
                  _resident((None, d_ff, d), lambda i: (layer, 0, 0))],
        out_specs=pl.BlockSpec((tm, d), lambda i: (i, 0)),
        out_shape=jax.ShapeDtypeStruct((t, d), F32),
        scratch_shapes=[pltpu.VMEM((tm, d_ff), BF16)],
        compiler_params=_cparams(("arbitrary",)),
        name="ffn",
    )(x, nw, wi, wo)


def _inproj_body(x_ref, nw_ref, w_ref, *rest, fw, sw, gw, q_scale, kv_t):
    if kv_t:
        wkvt_ref, h_ref, q_ref, k_ref, v_ref, kv_ref, u_ref, gq_ref, z_ref, sm_ref = rest
    else:
        h_ref, q_ref, k_ref, v_ref, kv_ref, u_ref, gq_ref, z_ref, sm_ref = rest
    h = _rms_bf16(x_ref[...], nw_ref[...])
    h_ref[...] = h
    o = 0
    q = _dot(h, w_ref[:, o:o + fw]); o += fw
    q_ref[...] = (q * q_scale).astype(BF16)
    kv = _dot(h, w_ref[:, o:o + 2 * fw]); o += 2 * fw
    if kv_t:
        kv_ref[...] = _dot_nt(wkvt_ref[...], h)
    else:
        kv_ref[...] = kv
    k_ref[...] = kv[:, :fw].astype(BF16)
    v_ref[...] = kv[:, fw:].astype(BF16)
    u_ref[...] = _dot(h, w_ref[:, o:o + sw]); o += sw
    gq_ref[...] = _dot(h, w_ref[:, o:o + 3 * gw]); o += 3 * gw
    z_ref[...] = _dot(h, w_ref[:, o:o + gw]); o += gw
    sm_ref[...] = _dot(h, w_ref[:, o:o + LANES])


def _inproj(x, nw, w, layer, fw, sw, gw, q_scale, w_kvt=None, seq_len=None):
    t, d = x.shape
    n = w.shape[2]
    kv_t = w_kvt is not None
    tm = max(c for c in (128, 256, 384, 512) if seq_len % c == 0) if kv_t else _row_tile(t)
    row = lambda width: pl.BlockSpec((tm, width), lambda i: (i, 0))
    widths = (d, fw, fw, fw, 2 * fw, sw, 3 * gw, gw, LANES)
    dts = (BF16, BF16, BF16, BF16, F32, F32, F32, F32, F32)
    in_specs = [row(d), _resident((None, 1, d), lambda i: (layer, 0, 0)),
                _resident((None, d, n), lambda i: (layer, 0, 0))]
    out_specs = [row(wd) for wd in widths]
    out_shape = [jax.ShapeDtypeStruct((t, wd), dt) for wd, dt in zip(widths, dts)]
    args = (x, nw, w)
    if kv_t:
        per_seq = seq_len // tm
        in_specs.append(_resident((None, 2 * fw, d), lambda i: (layer, 0, 0)))
        out_specs[4] = pl.BlockSpec((None, 2 * fw, tm), lambda i: (i // per_seq, 0, i % per_seq))
        out_shape[4] = jax.ShapeDtypeStruct((t // seq_len, 2 * fw, seq_len), F32)
        args = (x, nw, w, w_kvt)
    return pl.pallas_call(
        functools.partial(_inproj_body, fw=fw, sw=sw, gw=gw, q_scale=q_scale, kv_t=kv_t),
        grid=(t // tm,),
        in_specs=in_specs,
        out_specs=out_specs,
        out_shape=out_shape,
        compiler_params=_cparams(("arbitrary",)),
        name="inproj",
    )(*args)


def _log_sigmoid(x):
    return jnp.minimum(x, 0.0) - jnp.log(1.0 + jnp.exp(-jnp.abs(x)))


def _fcum_body(lg_ref, b_ref, lf_ref, cum_ref, *, nblk, pad):
    r = lax.broadcasted_iota(jnp.int32, (LANES, LANES), 0)
    c = lax.broadcasted_iota(jnp.int32, (LANES, LANES), 1)
    upper = (r <= c).astype(F32)
    lane = lax.broadcasted_iota(jnp.int32, (1, LANES), 1)
    carry = jnp.zeros((lg_ref.shape[0], 1), F32)
    for i in range(nblk):
        sl = slice(i * LANES, (i + 1) * LANES)
        lf = _log_sigmoid(lg_ref[:, sl] + b_ref[...])
        lf_ref[:, sl] = lf
        cs = _dot_hi(lf, upper) + carry
        carry = cs[:, LANES - 1:LANES]
        cum_ref[:, sl] = jnp.where(lane + i * LANES < pad, -NEG_BIG, cs) if i * LANES < pad else cs


def _fcum(logits, bias, pad):
    ns, nh, length = logits.shape
    blk = pl.BlockSpec((None, nh, length), lambda s: (s, 0, 0))
    return pl.pallas_call(
        functools.partial(_fcum_body, nblk=length // LANES, pad=pad),
        grid=(ns,),
        in_specs=[blk, pl.BlockSpec((nh, 1), lambda s: (0, 0))],
        out_specs=[blk, blk],
        out_shape=[jax.ShapeDtypeStruct(logits.shape, F32)] * 2,
        compiler_params=_cparams(("arbitrary",)),
        name="fcum",
    )(logits, bias)


def _fox_prompt_body(q_ref, k_ref, v_ref, fk_ref, o_ref, m_sc, l_sc, acc_sc, s_sc, *, tq, hd, n_heads):
    qi = pl.program_id(1)
    lane = lax.broadcasted_iota(jnp.int32, (1, LANES), 1)
    lo = lane < hd
    reps = tq // LANES
    causal = lax.broadcasted_iota(jnp.int32, (tq, tq), 1) <= lax.broadcasted_iota(jnp.int32, (tq, tq), 0)
    for p in range(n_heads * hd // LANES):
        qp = q_ref[:, p * LANES:(p + 1) * LANES]
        qm = (jnp.where(lo, qp, jnp.zeros_like(qp)), jnp.where(lo, jnp.zeros_like(qp), qp))
        m_sc[...] = jnp.full(m_sc.shape, 3.0 * NEG_BIG * LOG2E, F32)
        l_sc[...] = jnp.zeros(l_sc.shape, F32)
        acc_sc[...] = jnp.zeros(acc_sc.shape, F32)

        def scores(j, p=p, qm=qm):
            start = pl.multiple_of(j * tq, tq)
            kb = k_ref[pl.ds(start, tq), p * LANES:(p + 1) * LANES]
            fkb = fk_ref[:, pl.ds(start, tq)] * LOG2E
            return [_dot_nt(qm[hh], kb) - fkb[2 * p + hh:2 * p + hh + 1, :] for hh in range(2)]

        def consume(j, s, masked, p=p):
            vb = v_ref[pl.ds(pl.multiple_of(j * tq, tq), tq), p * LANES:(p + 1) * LANES]
            if masked:
                s = [jnp.where(causal, sh, 2.0 * NEG_BIG) for sh in s]
            m_old = [m_sc[hh] for hh in range(2)]
            m_new = [jnp.maximum(m_old[hh], jnp.max(s[hh], axis=-1, keepdims=True)) for hh in range(2)]
            alpha = [jnp.exp2(m_old[hh] - m_new[hh]) for hh in range(2)]
            e = [jnp.exp2(s[hh] - pltpu.repeat(m_new[hh], reps, 1)) for hh in range(2)]
            pv = [_dot(e[hh].astype(BF16), vb) for hh in range(2)]
            for hh in range(2):
                l_sc[hh] = alpha[hh] * l_sc[hh] + jnp.sum(e[hh], axis=-1, keepdims=True)
                m_sc[hh] = m_new[hh]
            acc_sc[...] = jnp.where(lo, alpha[0], alpha[1]) * acc_sc[...] + jnp.where(lo, pv[0], pv[1])

        s0 = scores(0)
        s_sc[0], s_sc[1] = s0[0], s0[1]

        def body(j, carry):
            s_cur = [s_sc[0], s_sc[1]]
            s_nxt = scores(j + 1)
            consume(j, s_cur, False)
            s_sc[0], s_sc[1] = s_nxt[0], s_nxt[1]
            return carry

        lax.fori_loop(0, qi, body, 0)
        consume(qi, [s_sc[0], s_sc[1]], True)
        inv = 1.0 / jnp.where(lo, l_sc[0], l_sc[1])
        o_ref[:, p * LANES:(p + 1) * LANES] = (acc_sc[...] * inv).astype(BF16)


def _fox_prompt(q, k, v, fcum, nb, lp, n_heads, hd, tq):
    fw = n_heads * hd
    nq = lp // tq
    return pl.pallas_call(
        functools.partial(_fox_prompt_body, tq=tq, hd=hd, n_heads=n_heads),
        grid=(nb, nq),
        in_specs=[pl.BlockSpec((tq, fw), lambda b, i: (b * nq + i, 0)),
                  pl.BlockSpec((lp, fw), lambda b, i: (b, 0)),
                  pl.BlockSpec((lp, fw), lambda b, i: (b, 0)),
                  pl.BlockSpec((None, n_heads, lp), lambda b, i: (b, 0, 0))],
        out_specs=pl.BlockSpec((tq, fw), lambda b, i: (b * nq + i, 0)),
        out_shape=jax.ShapeDtypeStruct((nb * lp, fw), BF16),
        scratch_shapes=[pltpu.VMEM((2, tq, LANES), F32), pltpu.VMEM((2, tq, LANES), F32),
                        pltpu.VMEM((tq, LANES), F32), pltpu.VMEM((2, tq, tq), F32)],
        compiler_params=_cparams(("arbitrary", "arbitrary")),
        name="fox_prompt",
    )(q, k, v, fcum)


def _fox_sample_body(pt_ref, qbd_ref, cn_ref, kn_ref, vn_ref, *rest, g_pages, n_q, n_heads, hd, page):
    kt_refs = rest[:g_pages]
    vt_refs = rest[g_pages:2 * g_pages]
    lf_ref, o_ref, m_sc, l_sc, acc_sc, car_sc = rest[2 * g_pages:]
    g = pl.program_id(1)
    n_pages = pl.num_programs(1) * g_pages
    lfs = [lf_ref[pt_ref[pl.program_id(0), n_pages - 1 - (g * g_pages + i)]] for i in range(g_pages)]
    nr = n_q * n_heads
    fw = n_heads * hd
    qbd = qbd_ref[...]
    rr = lax.broadcasted_iota(jnp.int32, (nr, 1), 0)
    qidx = rr // n_heads
    lane = lax.broadcasted_iota(jnp.int32, (1, LANES), 1)
    col_new = jnp.concatenate([cn_ref[...]] * n_q, axis=0)
    row_new = jnp.sum(jnp.where(lane == qidx, col_new, 0.0), axis=-1, keepdims=True)

    def update(slots, scores, pv_ofs):
        m_old = [m_sc[i] for i in slots]
        m_new = [jnp.maximum(mo, jnp.max(s, axis=-1, keepdims=True)) for mo, s in zip(m_old, scores)]
        alpha = [jnp.exp2(mo - mn) for mo, mn in zip(m_old, m_new)]
        e = [jnp.exp2(s - mn) for s, mn in zip(scores, m_new)]
        pv = [f(ee.astype(BF16)) for f, ee in zip(pv_ofs, e)]
        for n, i in enumerate(slots):
            l_sc[i] = alpha[n] * l_sc[i] + jnp.sum(e[n], axis=-1, keepdims=True)
            m_sc[i] = m_new[n]
            acc_sc[i] = alpha[n] * acc_sc[i] + pv[n]

    @pl.when(g == 0)
    def _():
        m_sc[...] = jnp.full(m_sc.shape, NEG_BIG, F32)
        l_sc[...] = jnp.zeros(l_sc.shape, F32)
        acc_sc[...] = jnp.zeros(acc_sc.shape, F32)
        car_sc[...] = jnp.zeros(car_sc.shape, F32)
        s = _dot_nt(qbd, kn_ref[...]) + (row_new - col_new) * LOG2E
        s = jnp.where((lane <= qidx) & (lane < n_q), s, NEG_BIG)
        update([0], [s], [lambda e: _dot(e, vn_ref[...])])

    r = lax.broadcasted_iota(jnp.int32, (page, page), 0)
    c = lax.broadcasted_iota(jnp.int32, (page, page), 1)
    later = (r > c).astype(F32)
    suffix = _dot_hi(jnp.concatenate(lfs, axis=0), later)
    carry = car_sc[...]
    scores, pv_ofs = [], []
    for i in range(g_pages):
        bias = jnp.concatenate([suffix[i * n_heads:(i + 1) * n_heads] + carry] * n_q, axis=0) + row_new
        kt = kt_refs[i][...].reshape(fw, page).astype(BF16)
        scores.append(_dot(qbd, kt) + bias * LOG2E)
        vt = vt_refs[i][...].reshape(fw, page).astype(BF16)
        pv_ofs.append(lambda e, vt=vt: _dot_nt(e, vt))
        carry = carry + jnp.sum(lfs[i], axis=-1, keepdims=True)
    car_sc[...] = carry
    update(list(range(g_pages)), scores, pv_ofs)

    @pl.when(g == pl.num_programs(1) - 1)
    def _():
        ms = [m_sc[i] for i in range(g_pages)]
        m_all = functools.reduce(jnp.maximum, ms)
        wts = [jnp.exp2(mi - m_all) for mi in ms]
        l_all = sum(w * l_sc[i] for i, w in enumerate(wts))
        o = sum(w * acc_sc[i] for i, w in enumerate(wts)) / l_all
        lane_w = lax.broadcasted_iota(jnp.int32, (1, fw), 1)
        o = jnp.where(lane_w // hd == rr % n_heads, o, 0.0)
        o_ref[...] = jnp.sum(o.reshape(n_q, n_heads, fw), axis=1).astype(BF16)


def _fox_sample(page_table, qbd, cum_new, k_new, v_new, kt, vt, lft, layer, g_pages, n_q):
    nb, n_pages = page_table.shape
    _, _, n_heads, hd, page = kt.shape
    fw = n_heads * hd
    nr = n_q * n_heads
    n_groups = n_pages // g_pages

    def page_map(i):
        return lambda b, g, pt: (layer, pt[b, n_pages - 1 - (g * g_pages + i)], 0, 0, 0)

    per_b = lambda shape: pl.BlockSpec((None,) + shape, lambda b, g, pt: (b, 0, 0))
    in_specs = ([per_b((nr, fw)), per_b((n_heads, LANES)), per_b((LANES, fw)), per_b((LANES, fw))]
                + [pl.BlockSpec((None, None, n_heads, hd, page), page_map(i)) for i in range(g_pages)]
                + [pl.BlockSpec((None, None, n_heads, hd, page), page_map(i)) for i in range(g_pages)]
                + [_resident((None,) + lft.shape[1:], lambda b, g, pt: (layer, 0, 0, 0))])
    return pl.pallas_call(
        functools.partial(_fox_sample_body, g_pages=g_pages, n_q=n_q, n_heads=n_heads, hd=hd, page=page),
        grid_spec=pltpu.PrefetchScalarGridSpec(
            num_scalar_prefetch=1, grid=(nb, n_groups), in_specs=in_specs,
            out_specs=per_b((n_q, fw)),
            scratch_shapes=[pltpu.VMEM((g_pages, nr, 1), F32), pltpu.VMEM((g_pages, nr, 1), F32),
                            pltpu.VMEM((g_pages, nr, fw), F32), pltpu.VMEM((n_heads, 1), F32)]),
        out_shape=jax.ShapeDtypeStruct((nb, n_q, fw), BF16),
        compiler_params=_cparams(("arbitrary", "arbitrary")),
        name="fox_sample",
    )(page_table, qbd, cum_new, k_new, v_new, *([kt] * g_pages), *([vt] * g_pages), lft)


def _gelu_tanh(x):
    return 0.5 * x * (1.0 + jnp.tanh(math.sqrt(2.0 / math.pi) * (x + 0.044715 * x * x * x)))


def _s5_tables(a_re, a_im, b_re, b_im, c_re, c_im, d_skip, log_dt, n_last):
    ck = S5_CHUNK
    g, p, gs = b_re.shape
    gt = LANES // gs
    nt = g // gt
    dt = jnp.exp(log_dt)[:, None]
    mag_l, ang = a_re * dt, a_im * dt

    def apow_fn(n):
        m = jnp.exp(mag_l * n)
        return m * jnp.cos(ang * n), m * jnp.sin(ang * n)

    ab_re, ab_im = apow_fn(1.0)
    den = a_re * a_re + a_im * a_im
    cf_re = ((ab_re - 1.0) * a_re + ab_im * a_im) / den
    cf_im = (ab_im * a_re - (ab_re - 1.0) * a_im) / den
    bb_re = cf_re[..., None] * b_re - cf_im[..., None] * b_im
    bb_im = cf_re[..., None] * b_im + cf_im[..., None] * b_re
    pw_re, pw_im = jax.vmap(apow_fn)(jnp.arange(ck + 1, dtype=F32))
    cab_re = c_re[None] * pw_re[:, :, None, :] - c_im[None] * pw_im[:, :, None, :]
    cab_im = c_re[None] * pw_im[:, :, None, :] + c_im[None] * pw_re[:, :, None, :]
    w = (jnp.einsum('lgcp,gpd->lgcd', cab_re[:ck], bb_re, precision=HI)
         - jnp.einsum('lgcp,gpd->lgcd', cab_im[:ck], bb_im, precision=HI))
    w = w.at[0].add(jax.vmap(jnp.diag)(d_skip))
    wc = jnp.transpose(w, (1, 3, 0, 2)).reshape(nt, gt * gs, ck * gs)
    e_pow = ck - 1 - jnp.arange(ck)
    f_re = pw_re[e_pow][..., None] * bb_re[None] - pw_im[e_pow][..., None] * bb_im[None]
    f_im = pw_re[e_pow][..., None] * bb_im[None] + pw_im[e_pow][..., None] * bb_re[None]
    f_rows = lambda f: jnp.transpose(f, (1, 0, 3, 2)).reshape(nt, gt, ck, gs, p)
    fc = jnp.transpose(jnp.concatenate([f_rows(f_re), f_rows(f_im)], axis=-1), (0, 2, 1, 3, 4)
                       ).reshape(nt, ck, gt * gs, 2 * p)
    e_rows = lambda m: jnp.transpose(m, (1, 0, 3, 2)).reshape(nt, gt, ck, p, gs)
    ec = jnp.transpose(jnp.stack([e_rows(cab_re[1:]), e_rows(-cab_im[1:])], axis=1), (0, 3, 1, 2, 4, 5)
                       ).reshape(nt, ck, 2 * gt * p, gs)

    tile_vec = lambda v: v.reshape(nt, gt * p)
    nlev = 16
    lev = [apow_fn(float(ck * 2 ** i)) for i in range(nlev)]
    apow = jnp.stack([jnp.stack([tile_vec(r), tile_vec(i)], axis=1) for r, i in lev], axis=1)
    al_re, al_im = apow_fn(float(n_last))
    alast = jnp.stack([tile_vec(al_re), tile_vec(al_im)], axis=1)
    wcat, fmat, emat = _s5_expand(wc.astype(BF16), fc.astype(BF16), ec.astype(BF16), gs, p)
    return wcat, fmat, emat, apow, alast


def _s5_expand_body(wc_ref, fc_ref, ec_ref, w_ref, f_ref, e_ref, *, ck, gs, p):
    gt = LANES // gs
    iota = lambda shape, axis: lax.broadcasted_iota(jnp.int32, shape, axis)
    bf = lambda m: m.astype(F32).astype(BF16)
    r, q = iota((ck * gs, ck * LANES), 0), iota((ck * gs, ck * LANES), 1)
    spread_w = bf((r // gs == q // LANES) & (r % gs == q % gs))
    r, q = iota((LANES, ck * LANES), 0), iota((LANES, ck * LANES), 1)
    w_ref[...] = jnp.where(r // gs == (q % LANES) // gs, _dot(wc_ref[...], spread_w), 0.0).astype(BF16)
    half = gt * p
    r, q = iota((2 * p, 2 * half), 0), iota((2 * p, 2 * half), 1)
    spread_f = bf((r // p == q // half) & (r % p == q % p))
    r, q = iota((LANES, 2 * half), 0), iota((LANES, 2 * half), 1)
    keep_f = r // gs == (q % half) // p
    r, q = iota((gs, LANES), 0), iota((gs, LANES), 1)
    spread_e = bf(r == q % gs)
    r, q = iota((2 * half, LANES), 0), iota((2 * half, LANES), 1)
    keep_e = (r % half) // p == q // gs
    for s in range(ck):
        f_ref[s // 2, (s % 2) * LANES:(s % 2 + 1) * LANES, :] = jnp.where(
            keep_f, _dot(fc_ref[s], spread_f), 0.0).astype(BF16)
        e_ref[s // 2, :, (s % 2) * LANES:(s % 2 + 1) * LANES] = jnp.where(
            keep_e, _dot(ec_ref[s], spread_e), 0.0).astype(BF16)


def _s5_expand(wc, fc, ec, gs, p):
    nt = wc.shape[0]
    ck = S5_CHUNK
    sw = 2 * (LANES // gs) * p
    blk = lambda a: pl.BlockSpec((None,) + a.shape[1:], lambda j: (j,) + (0,) * (a.ndim - 1))
    shapes = [jax.ShapeDtypeStruct((nt, LANES, ck * LANES), BF16),
              jax.ShapeDtypeStruct((nt, ck // 2, 2 * LANES, sw), BF16),
              jax.ShapeDtypeStruct((nt, ck // 2, sw, 2 * LANES), BF16)]
    return pl.pallas_call(
        functools.partial(_s5_expand_body, ck=ck, gs=gs, p=p),
        grid=(nt,),
        in_specs=[blk(wc), blk(fc), blk(ec)],
        out_specs=[blk(s) for s in shapes],
        out_shape=shapes,
        compiler_params=_cparams(("arbitrary",)),
        name="s5_expand",
    )(wc, fc, ec)


def _s5_local_body(u_ref, w_ref, f_ref, y_ref, z_ref, acc_ref, *, ck, n):
    accz = None
    zero = jnp.zeros((LANES, LANES), BF16)
    for s in range(0, ck, 2):
        us = jnp.concatenate([u_ref[pl.ds(s, n, stride=ck), :].astype(BF16),
                              u_ref[pl.ds(s + 1, n, stride=ck), :].astype(BF16)], axis=1)
        wide = (ck - s) * LANES
        w_two = jnp.concatenate([w_ref[:, :wide],
                                 jnp.concatenate([zero, w_ref[:, :wide - LANES]], axis=1)], axis=0)
        y = _dot(us, w_two)
        if s == 0:
            acc_ref[...] = y
        else:
            acc_ref[:, s * LANES:] += y
        z = _dot(us, f_ref[s // 2])
        accz = z if accz is None else accz + z
    for t in range(ck):
        y_ref[pl.ds(t, n, stride=ck), :] = acc_ref[:, t * LANES:(t + 1) * LANES]
    z_ref[...] = accz


def _s5_local(u, wcat, fmat, rows_per_step):
    r, width = u.shape
    ck = S5_CHUNK
    nt = width // LANES
    sw = fmat.shape[3]
    n = rows_per_step // ck
    return pl.pallas_call(
        functools.partial(_s5_local_body, ck=ck, n=n),
        grid=(nt, r // rows_per_step),
        in_specs=[pl.BlockSpec((rows_per_step, LANES), lambda j, i: (i, j)),
                  pl.BlockSpec((None, LANES, ck * LANES), lambda j, i: (j, 0, 0)),
                  pl.BlockSpec((None, ck // 2, 2 * LANES, sw), lambda j, i: (j, 0, 0, 0))],
        out_specs=[pl.BlockSpec((rows_per_step, LANES), lambda j, i: (i, j)),
                   pl.BlockSpec((None, n, sw), lambda j, i: (j, i, 0))],
        out_shape=[jax.ShapeDtypeStruct((r, width), F32), jax.ShapeDtypeStruct((nt, r // ck, sw), F32)],
        scratch_shapes=[pltpu.VMEM((n, ck * LANES), F32)],
        compiler_params=_cparams(("arbitrary", "arbitrary")),
        name="s5_local",
    )(u, wcat, fmat)


def _s5_scan_body(z_ref, ap_ref, xs_ref, xf_ref, *, nc, half):
    xr = z_ref[:, :half]
    xi = z_ref[:, half:]
    row = lax.broadcasted_iota(jnp.int32, (nc, 1), 0)
    lev, s = 0, 1
    while s < nc:
        ar = ap_ref[lev, 0:1, :]
        ai = ap_ref[lev, 1:2, :]
        keep = row >= s
        sr = jnp.where(keep, pltpu.roll(xr, s, 0), 0.0)
        si = jnp.where(keep, pltpu.roll(xi, s, 0), 0.0)
        xr, xi = xr + ar * sr - ai * si, xi + ar * si + ai * sr
        lev, s = lev + 1, s * 2
    xf_ref[:, :half] = xr[nc - 1:nc, :]
    xf_ref[:, half:] = xi[nc - 1:nc, :]
    xs_ref[:, :half] = jnp.where(row >= 1, pltpu.roll(xr, 1, 0), 0.0)
    xs_ref[:, half:] = jnp.where(row >= 1, pltpu.roll(xi, 1, 0), 0.0)


def _s5_scan(z, apow, nb):
    nt, n, sw = z.shape
    nc = n // nb
    nlev = apow.shape[1]
    return pl.pallas_call(
        functools.partial(_s5_scan_body, nc=nc, half=sw // 2),
        grid=(nt, nb),
        in_specs=[pl.BlockSpec((None, nc, sw), lambda j, b: (j, b, 0)),
                  pl.BlockSpec((None, nlev, 2, sw // 2), lambda j, b: (j, 0, 0, 0))],
        out_specs=[pl.BlockSpec((None, nc, sw), lambda j, b: (j, b, 0)),
                   pl.BlockSpec((None, None, 1, sw), lambda j, b: (j, b, 0, 0))],
        out_shape=[jax.ShapeDtypeStruct((nt, n, sw), F32), jax.ShapeDtypeStruct((nt, nb, 1, sw), F32)],
        compiler_params=_cparams(("arbitrary", "arbitrary")),
        name="s5_scan",
    )(z, apow)


def _s5_out_body(y_ref, xs_ref, e_ref, o_ref, act_ref, *, ck, n, first):
    xs = xs_ref[...].astype(BF16)
    for t in range(0, ck, 2):
        pair = [y_ref[pl.ds(t + i, n, stride=ck), :] for i in range(2)]
        if t >= first:
            both = _dot(xs, e_ref[(t - first) // 2])
            pair = [pair[i] + both[:, i * LANES:(i + 1) * LANES] for i in range(2)]
        for i in range(2):
            act_ref[pl.ds(t + i, n, stride=ck), :] = _gelu_tanh(pair[i])
    o_ref[...] = act_ref[...].astype(BF16)


def _s5_out(ylocal, xstart, emat, rows_per_step, first):
    r, width = ylocal.shape
    ck = S5_CHUNK
    nt = width // LANES
    sw = xstart.shape[2]
    n = rows_per_step // ck
    return pl.pallas_call(
        functools.partial(_s5_out_body, ck=ck, n=n, first=first),
        grid=(nt, r // rows_per_step),
        in_specs=[pl.BlockSpec((rows_per_step, LANES), lambda j, i: (i, j)),
                  pl.BlockSpec((None, n, sw), lambda j, i: (j, i, 0)),
                  pl.BlockSpec((None, ck // 2, sw, 2 * LANES), lambda j, i: (j, 0, 0, 0))],
        out_specs=pl.BlockSpec((rows_per_step, LANES), lambda j, i: (i, j)),
        out_shape=jax.ShapeDtypeStruct((r, width), BF16),
        scratch_shapes=[pltpu.VMEM((rows_per_step, LANES), F32)],
        compiler_params=_cparams(("arbitrary", "arbitrary")),
        name="s5_out",
    )(ylocal, xstart, emat)


def _s5_step_body(z_ref, x0_ref, al_ref, xf_ref, *, half):
    ar, ai = al_ref[0:1, :], al_ref[1:2, :]
    xr, xi = x0_ref[:, :half], x0_ref[:, half:]
    xf_ref[:, :half] = z_ref[:, :half] + ar * xr - ai * xi
    xf_ref[:, half:] = z_ref[:, half:] + ar * xi + ai * xr


def _s5_step(z, x0, alast):
    nt, n, sw = z.shape
    blk = pl.BlockSpec((None, n, sw), lambda j: (j, 0, 0))
    return pl.pallas_call(
        functools.partial(_s5_step_body, half=sw // 2),
        grid=(nt,),
        in_specs=[blk, blk, pl.BlockSpec((None, 2, sw // 2), lambda j: (j, 0, 0))],
        out_specs=blk,
        out_shape=jax.ShapeDtypeStruct(z.shape, F32),
        compiler_params=_cparams(("arbitrary",)),
        name="s5_step",
    )(z, x0, alast)


def _gdn_body(x_ref, z_ref, sm_ref, ci_ref, s0_ref, cw_ref, na_ref, db_ref, nw_ref, o_ref, sf_ref, s_sc, car_sc,
              *, c, sb, n_valid, n_heads, hd, kconv, a_lane, b_lane):
    ci = pl.program_id(1)
    gw = n_heads * hd

    hist = ci_ref.shape[1]

    @pl.when(ci == 0)
    def _():
        s_sc[...] = s0_ref[...]
        car_sc[:, :hist, :] = ci_ref[...]

    r = lax.broadcasted_iota(jnp.int32, (c, c), 0)
    cc = lax.broadcasted_iota(jnp.int32, (c, c), 1)
    incl = r >= cc
    strict = r > cc
    eye = (r == cc).astype(F32)
    row = ci * c + lax.broadcasted_iota(jnp.int32, (c, 1), 0)
    chains = []
    for b in range(sb):
        car_sc[b, hist:, :] = x_ref[b]
        y = None
        for j in range(kconv):
            term = car_sc[b, pl.ds(hist - (kconv - 1) + j, c), :] * cw_ref[j:j + 1, :]
            y = term if y is None else y + term
        car_sc[b, :hist, :] = car_sc[b, c:, :]
        qkv = y * _sigmoid(y)

        sm = sm_ref[b]
        sp_in = sm + db_ref[...]
        softplus = jnp.maximum(sp_in, 0.0) + jnp.log(1.0 + jnp.exp(-jnp.abs(sp_in)))
        gfull = na_ref[...] * softplus
        beta_full = _sigmoid(sm)
        if n_valid is not None:
            live = row < n_valid
            gfull = jnp.where(live, gfull, 0.0)
            beta_full = jnp.where(live, beta_full, 0.0)
            qkv = jnp.where(live, qkv, 0.0)
        cum_full = _dot_hi(incl.astype(F32), gfull)
        cum_t = cum_full.T
        for h in range(n_heads):
            q = qkv[:, h * hd:(h + 1) * hd]
            k = qkv[:, gw + h * hd:gw + (h + 1) * hd]
            v = qkv[:, 2 * gw + h * hd:2 * gw + (h + 1) * hd]
            q = q * lax.rsqrt(jnp.sum(q * q, axis=-1, keepdims=True) + 1e-6) * (hd ** -0.5)
            k = k * lax.rsqrt(jnp.sum(k * k, axis=-1, keepdims=True) + 1e-6)
            cum = cum_full[:, a_lane + h:a_lane + h + 1]
            cum_r = cum_t[a_lane + h:a_lane + h + 1, :]
            beta = beta_full[:, b_lane + h:b_lane + h + 1]
            decay = jnp.where(incl, jnp.exp(jnp.where(incl, cum - cum_r, 0.0)), 0.0)
            kb = k * beta
            k16 = k.astype(BF16)
            ecum = jnp.exp(cum)
            g_last = cum[c - 1:c, :]
            chains.append(dict(
                b=b, h=h, k16=k16, decay=decay, qe16=(q * ecum).astype(BF16), q16=q.astype(BF16),
                rhs=jnp.concatenate([v * beta, kb * ecum], axis=-1), kb16=kb.astype(BF16),
                kd=k * jnp.exp(g_last - cum), s_scale=jnp.exp(g_last)))
    for ch in chains:
        ch['lower'] = jnp.where(strict, _dot_nt(ch['kb16'], ch['k16']) * ch['decay'], 0.0)
        ch['aqk16'] = jnp.where(incl, _dot_nt(ch['q16'], ch['k16']) * ch['decay'], 0.0).astype(BF16)
    for ch in chains:
        ch['t_inv'] = eye - ch['lower']
        l_hi, l_lo = _split(ch['lower'])
        ch['pw'] = _dot3s(l_hi, l_lo, l_hi, l_lo)
    span = 2
    while span < c:
        span *= 2
        for ch in chains:
            p_hi, p_lo = _split(ch['pw'])
            t_hi, t_lo = _split(ch['t_inv'])
            if span < c:
                both = _dot3s(p_hi, p_lo, jnp.concatenate([t_hi, p_hi], axis=1),
                              jnp.concatenate([t_lo, p_lo], axis=1))
                ch['t_inv'] = ch['t_inv'] + both[:, :c]
                ch['pw'] = both[:, c:]
            else:
                ch['t_inv'] = ch['t_inv'] + _dot3s(p_hi, p_lo, t_hi, t_lo)
    for ch in chains:
        ch['sol'] = _dot3(ch['t_inv'], ch['rhs'])
    for ch in chains:
        s_h = s_sc[ch['b'], ch['h']]
        s16 = s_h.astype(BF16)
        v_new = ch['sol'][:, :hd] - _dot(ch['sol'][:, hd:].astype(BF16), s16)
        ch['vn16'] = v_new.astype(BF16)
        ch['o'] = _dot(ch['qe16'], s16)
        ch['s_dec'] = s_h * ch['s_scale']
    for ch in chains:
        b, h = ch['b'], ch['h']
        o = ch['o'] + _dot(ch['aqk16'], ch['vn16'])
        s_sc[b, h] = ch['s_dec'] + _dot(ch['kd'].T.astype(BF16), ch['vn16'])
        o = o * lax.rsqrt(jnp.mean(o * o, axis=-1, keepdims=True) + RMS_EPS) * nw_ref[...]
        zz = z_ref[b, :, h * hd:(h + 1) * hd]
        o_ref[b, :, h * hd:(h + 1) * hd] = (o * (zz * _sigmoid(zz))).astype(BF16)

    @pl.when(ci == pl.num_programs(1) - 1)
    def _():
        sf_ref[...] = s_sc[...]


def _gdn(x, z, sm, conv_init, s0, conv_w, neg_a, dt_b, norm_w, sb, n_valid, a_lane, b_lane):
    ns, length, w3 = x.shape
    n_heads, hd = s0.shape[1], s0.shape[2]
    gw = n_heads * hd
    c = GDN_CHUNK
    kconv = conv_w.shape[0]
    hist = conv_init.shape[1]
    rowblk = lambda width: pl.BlockSpec((sb, c, width), lambda s, i: (s, i, 0))
    const = lambda shape: pl.BlockSpec(shape, lambda s, i: (0,) * len(shape))
    return pl.pallas_call(
        functools.partial(_gdn_body, c=c, sb=sb, n_valid=n_valid, n_heads=n_heads, hd=hd, kconv=kconv,
                          a_lane=a_lane, b_lane=b_lane),
        grid=(ns // sb, length // c),
        in_specs=[rowblk(w3), rowblk(gw), rowblk(LANES),
                  pl.BlockSpec((sb, hist, w3), lambda s, i: (s, 0, 0)),
                  pl.BlockSpec((sb, n_heads, hd, hd), lambda s, i: (s, 0, 0, 0)),
                  const((kconv, w3)), const((1, LANES)), const((1, LANES)), const((1, hd))],
        out_specs=[rowblk(gw), pl.BlockSpec((sb, n_heads, hd, hd), lambda s, i: (s, 0, 0, 0))],
        out_shape=[jax.ShapeDtypeStruct((ns, length, gw), BF16), jax.ShapeDtypeStruct(s0.shape, F32)],
        scratch_shapes=[pltpu.VMEM((sb, n_heads, hd, hd), F32), pltpu.VMEM((sb, hist + c, w3), F32)],
        compiler_params=_cparams(("arbitrary", "arbitrary")),
        name="gdn",
    )(x, z, sm, conv_init, s0, conv_w, neg_a, dt_b, norm_w)


def _merge_body(x_ref, h_ref, of_ref, z5_ref, og_ref, wf_ref, w5_ref, wg_ref, wgate_ref, wo_ref, o_ref, *, d):
    fox_br = _dot(of_ref[...], wf_ref[...])
    glu = _dot(z5_ref[...], w5_ref[...])
    s5_br = glu[:, :d] * _sigmoid(glu[:, d:])
    gdn_br = _dot(og_ref[...], wg_ref[...])
    h = h_ref[...]
    merged = (_sigmoid(_dot(h, wgate_ref[:, :d])) * fox_br
              + _sigmoid(_dot(h, wgate_ref[:, d:2 * d])) * s5_br
              + _sigmoid(_dot(h, wgate_ref[:, 2 * d:])) * gdn_br)
    o_ref[...] = x_ref[...] + _dot(merged.astype(BF16), wo_ref[...])


def _merge(x, h, o_fox, z5, o_gdn, wf, w5, wg, wgate, wo, layer):
    t, d = x.shape
    tm = _row_tile(t)
    row = lambda a: pl.BlockSpec((tm, a.shape[1]), lambda i: (i, 0))
    res = lambda a: _resident((None,) + a.shape[1:], lambda i: (layer, 0, 0))
    return pl.pallas_call(
        functools.partial(_merge_body, d=d),
        grid=(t // tm,),
        in_specs=[row(x), row(h), row(o_fox), row(z5), row(o_gdn), res(wf), res(w5), res(wg), res(wgate), res(wo)],
        out_specs=pl.BlockSpec((tm, d), lambda i: (i, 0)),
        out_shape=jax.ShapeDtypeStruct((t, d), F32),
        compiler_params=_cparams(("arbitrary",)),
        name="merge",
    )(x, h, o_fox, z5, o_gdn, wf, w5, wg, wgate, wo)


def _final_norm_body(x_ref, w_ref, o_ref):
    x = x_ref[...]
    ms = jnp.mean(x * x, axis=-1, keepdims=True)
    o_ref[...] = x * lax.rsqrt(ms + RMS_EPS) * w_ref[...]


def _final_norm(x, w):
    t, d = x.shape
    tm = _row_tile(t)
    return pl.pallas_call(
        _final_norm_body,
        grid=(t // tm,),
        in_specs=[pl.BlockSpec((tm, d), lambda i: (i, 0)), pl.BlockSpec((1, d), lambda i: (0, 0))],
        out_specs=pl.BlockSpec((tm, d), lambda i: (i, 0)),
        out_shape=jax.ShapeDtypeStruct((t, d), F32),
        compiler_params=_cparams(("arbitrary",)),
        name="final_norm",
    )(x, w)


def kernel(x_prompt, x_sample, cache_fox_k, cache_fox_v, cache_fox_logf, state_s5_re, state_s5_im, state_gdn, state_gdn_conv, page_table, meta_tokens, norm_ffn1, w_ffn1_in, w_ffn1_out, norm_mix, w_in, fox_b_f, w_fox_br, s5_A_re, s5_A_im, s5_B_re, s5_B_im, s5_C_re, s5_C_im, s5_D, s5_log_dt, w_s5_glu, gdn_conv_w, gdn_A_log, gdn_dt_bias, gdn_norm, w_gdn_br, w_out, norm_ffn2, w_ffn2_in, w_ffn2_out, norm_final):
    nb, seq, d = x_prompt.shape
    ds_b, ds_t, _ = x_sample.shape
    depth = w_in.shape[0]
    n_meta = meta_tokens.shape[0]
    fh, fhd = cache_fox_k.shape[3], cache_fox_k.shape[4]
    fw = fh * fhd
    page = cache_fox_k.shape[2]
    sg, sp, sgs = s5_B_re.shape[1:]
    sw = sg * sgs
    gh, ghd = state_gdn.shape[2], state_gdn.shape[3]
    gw = gh * ghd
    kconv = gdn_conv_w.shape[1]
    ck = S5_CHUNK
    gc = GDN_CHUNK
    assert fw % LANES == 0 and 2 * fhd == LANES and ghd == LANES and sw % LANES == 0 and LANES % sgs == 0
    assert ds_t <= ck and (ck - ds_t) % 2 == 0 and ds_t >= kconv - 1 and ds_t <= gc
    assert page == LANES and fh + 2 * gh <= LANES

    lreal = n_meta + seq
    pad = (-lreal) % LANES
    lp = lreal + pad
    tp = nb * lp
    ts = ds_b * ds_t
    tq = max(c for c in (128, 256, 384, 512) if lp % c == 0)
    n_pages = page_table.shape[1]
    g_pages = max(c for c in (1, 2, 4, 8) if n_pages % c == 0)
    hist = 8
    nt = sw // LANES
    gt = LANES // sgs
    ssw = 2 * gt * sp

    sizes = (fw, fw, fw, fh, sw, 3 * gw, gh, gh, gw, 3 * d)
    offs = [0]
    for s in sizes:
        offs.append(offs[-1] + s)
    col = lambda i: w_in[:, :, offs[i]:offs[i + 1]]
    small = jnp.concatenate([col(3), col(6), col(7),
                             jnp.zeros((depth, d, LANES - fh - 2 * gh), F32)], axis=2)
    a_lane, b_lane = fh, fh + gh
    w_proj = jnp.concatenate([col(0), col(1), col(2), col(4), col(5), col(8), small], axis=2).astype(BF16)
    w_gate = col(9).astype(BF16)
    w_kvt = jnp.transpose(w_in[:, :, offs[1]:offs[3]], (0, 2, 1)).astype(BF16)
    q_scale = fhd ** -0.5 * LOG2E
    wi1, wo1 = w_ffn1_in.astype(BF16), w_ffn1_out.astype(BF16)
    wi2, wo2 = w_ffn2_in.astype(BF16), w_ffn2_out.astype(BF16)
    wf16, w516, wg16, wo16 = (w.astype(BF16) for w in (w_fox_br, w_s5_glu, w_gdn_br, w_out))
    nrm = lambda w: w.reshape(depth, 1, d)
    n1, nm, n2 = nrm(norm_ffn1), nrm(norm_mix), nrm(norm_ffn2)
    lane_vec = lambda v, at: jnp.zeros((depth, 1, LANES), F32).at[:, 0, at:at + v.shape[1]].set(v)
    neg_a = lane_vec(-jnp.exp(gdn_A_log), a_lane)
    dt_b = lane_vec(gdn_dt_bias, a_lane)
    s5_tab = [_s5_tables(s5_A_re[l], s5_A_im[l], s5_B_re[l], s5_B_im[l], s5_C_re[l], s5_C_im[l], s5_D[l],
                         s5_log_dt[l], ds_t) for l in range(depth)]

    kt_cache = jnp.transpose(cache_fox_k, (0, 1, 3, 4, 2))
    vt_cache = jnp.transpose(cache_fox_v, (0, 1, 3, 4, 2))
    lf_cache = jnp.transpose(cache_fox_logf, (0, 1, 3, 2))

    meta = jnp.broadcast_to(meta_tokens[None].astype(F32), (nb, n_meta, d))
    xp = jnp.concatenate([jnp.zeros((nb, pad, d), F32), meta, x_prompt], axis=1).reshape(tp, d)
    xs = x_sample.reshape(ts, d)

    zeros_conv = jnp.zeros((nb, hist, 3 * gw), F32)
    zeros_gdn = jnp.zeros((nb, gh, ghd, ghd), F32)
    sb_p = 2 if nb % 2 == 0 else 1
    sb_s = 2 if ds_b % 2 == 0 else 1
    head_eye = jnp.eye(fh, dtype=BF16).reshape(1, 1, fh, fh, 1)
    to_tile = lambda st: jnp.transpose(st.reshape(ds_b, nt, gt * sp), (1, 0, 2))
    from_tile = lambda a, n: jnp.transpose(a, (1, 0, 2)).reshape(n, sg, sp)
    outs = [[] for _ in range(14)]

    for l in range(depth):
        xp = _ffn(xp, n1, wi1, wo1, l)
        xs = _ffn(xs, n1, wi1, wo1, l)
        hp, qp, kp, vp, kvp, up, gqp, zp, smp = _inproj(xp, nm, w_proj, l, fw, sw, gw, q_scale, w_kvt, lp)
        hs, qs, ks, vs, kvs, us, gqs, zs, sms = _inproj(xs, nm, w_proj, l, fw, sw, gw, q_scale)
        bias_f = fox_b_f[l].reshape(fh, 1)

        lg_p = jnp.transpose(smp[:, :fh].reshape(nb, lp, fh), (0, 2, 1))
        lf_p, cum_p = _fcum(lg_p, bias_f, pad)
        o_fox_p = _fox_prompt(qp, kp, vp, cum_p, nb, lp, fh, fhd, tq)
        lg_s = jnp.transpose(sms[:, :fh].reshape(ds_b, ds_t, fh), (0, 2, 1))
        lg_s = jnp.pad(lg_s, ((0, 0), (0, 0), (0, LANES - ds_t)))
        lf_s, cum_s = _fcum(lg_s, bias_f, 0)
        qbd = (qs.reshape(ds_b, ds_t, 1, fh, fhd) * head_eye).reshape(ds_b, ds_t * fh, fw)
        rows_pad = ((0, 0), (0, LANES - ds_t), (0, 0))
        k_new = jnp.pad(ks.reshape(ds_b, ds_t, fw), rows_pad)
        v_new = jnp.pad(vs.reshape(ds_b, ds_t, fw), rows_pad)
        o_fox_s = _fox_sample(page_table, qbd, cum_s, k_new, v_new, kt_cache, vt_cache, lf_cache, l, g_pages, ds_t)

        wcat, fmat, emat, apow, alast = s5_tab[l]
        y_loc, z_loc = _s5_local(up, wcat, fmat, lp)
        x_start, x_fin = _s5_scan(z_loc, apow, nb)
        z5_p = _s5_out(y_loc, x_start, emat, lp, 0)
        us_c = jnp.pad(us.reshape(ds_b, ds_t, sw), ((0, 0), (ck - ds_t, 0), (0, 0))).reshape(ds_b * ck, sw)
        x0 = jnp.concatenate([to_tile(state_s5_re[l]), to_tile(state_s5_im[l])], axis=2)
        y_loc_s, z_loc_s = _s5_local(us_c, wcat, fmat, ds_b * ck)
        z5_s = _s5_out(y_loc_s, x0, emat, ds_b * ck, ck - ds_t)
        z5_s = z5_s.reshape(ds_b, ck, sw)[:, ck - ds_t:].reshape(ts, sw)
        xf_s = _s5_step(z_loc_s, x0, alast)

        cw = gdn_conv_w[l]
        gnorm = gdn_norm[l].reshape(1, ghd)
        o_gdn_p, s_p = _gdn(gqp.reshape(nb, lp, 3 * gw), zp.reshape(nb, lp, gw), smp.reshape(nb, lp, LANES),
                            zeros_conv, zeros_gdn, cw, neg_a[l], dt_b[l], gnorm, sb_p, None, a_lane, b_lane)
        seq_pad = lambda a: jnp.pad(a.reshape(ds_b, ds_t, a.shape[1]), ((0, 0), (0, gc - ds_t), (0, 0)))
        conv_s = jnp.pad(state_gdn_conv[l], ((0, 0), (hist - (kconv - 1), 0), (0, 0)))
        o_gdn_s, s_s = _gdn(seq_pad(gqs), seq_pad(zs), seq_pad(sms), conv_s, state_gdn[l], cw, neg_a[l], dt_b[l],
                            gnorm, sb_s, ds_t, a_lane, b_lane)

        xp = _merge(xp, hp, o_fox_p, z5_p, o_gdn_p.reshape(tp, gw), wf16, w516, wg16, w_gate, wo16, l)
        xs = _merge(xs, hs, o_fox_s.reshape(ts, fw), z5_s, o_gdn_s[:, :ds_t].reshape(ts, gw),
                    wf16, w516, wg16, w_gate, wo16, l)
        xp = _ffn(xp, n2, wi2, wo2, l)
        xs = _ffn(xs, n2, wi2, wo2, l)

        kv_p = jnp.transpose(kvp.reshape(nb, 2, fh, fhd, lp)[..., pad:], (0, 4, 1, 2, 3))
        kv_s = kvs.reshape(ds_b, ds_t, 2, fh, fhd)
        xf_p = x_fin.reshape(nt, nb, ssw)
        gq_s = jnp.concatenate([state_gdn_conv[l], gqs.reshape(ds_b, ds_t, 3 * gw)], axis=1)
        new = (kv_p[:, :, 0], kv_p[:, :, 1], jnp.transpose(lf_p[:, :, pad:], (0, 2, 1)),
               kv_s[:, :, 0], kv_s[:, :, 1], jnp.transpose(lf_s[:, :, :ds_t], (0, 2, 1)),
               from_tile(xf_p[:, :, :ssw // 2], nb), from_tile(xf_p[:, :, ssw // 2:], nb),
               from_tile(xf_s[:, :, :ssw // 2], ds_b), from_tile(xf_s[:, :, ssw // 2:], ds_b),
               s_p, s_s, gqp.reshape(nb, lp, 3 * gw)[:, lp - (kconv - 1):], gq_s[:, -(kconv - 1):])
        for i, a in enumerate(new):
            outs[i].append(a)

    y_prompt = _final_norm(xp, norm_final.reshape(1, d)).reshape(nb, lp, d)[:, pad + n_meta:]
    y_sample = _final_norm(xs, norm_final.reshape(1, d)).reshape(ds_b, ds_t, d)
    return (y_prompt, y_sample) + tuple(jnp.stack(o) for o in outs)
```

```python
import functools
import math

import jax
import jax.numpy as jnp
from jax import lax
from jax.experimental import pallas as pl
from jax.experimental.pallas import tpu as pltpu

F32 = jnp.float32
BF16 = jnp.bfloat16
RMS_EPS = 1e-6
NEG_BIG = -1e30
LANES = 128
VMEM_LIMIT = 56 * 1024 * 1024
S5_CHUNK = 16
GDN_CHUNK = 128
HI = lax.Precision.HIGHEST
LOG2E = math.log2(math.e)


def _cparams(sem):
    return pltpu.CompilerParams(dimension_semantics=sem, vmem_limit_bytes=VMEM_LIMIT)


def _resident(shape, index_map):
    return pl.BlockSpec(shape, index_map, pipeline_mode=pl.Buffered(1))


def _dot(a, b):
    return jnp.dot(a, b, preferred_element_type=F32)


def _dot_hi(a, b):
    return jnp.dot(a, b, preferred_element_type=F32, precision=HI)


def _dot_nt(a, b):
    return lax.dot_general(a, b, (((1,), (1,)), ((), ())), preferred_element_type=F32)


def _split(x):
    hi = x.astype(BF16)
    return hi, (x - hi.astype(F32)).astype(BF16)


def _dot3s(a_hi, a_lo, b_hi, b_lo):
    return _dot(jnp.concatenate([a_hi, a_lo, a_hi], axis=1), jnp.concatenate([b_hi, b_hi, b_lo], axis=0))


def _dot3(a, b):
    return _dot3s(*_split(a), *_split(b))


def _rms_bf16(x, w):
    ms = jnp.mean(x * x, axis=-1, keepdims=True)
    return (x * lax.rsqrt(ms + RMS_EPS) * w).astype(BF16)


def _sigmoid(x):
    return 1.0 / (1.0 + jnp.exp(-x))


def _row_tile(t, cap=1024):
    best = 8
    for tm in range(8, min(t, cap) + 1, 8):
        if t % tm == 0:
            best = tm
    return best


def _ffn_body(x_ref, nw_ref, wi_ref, wo_ref, o_ref, act_ref, *, d_ff, fc):
    x = x_ref[...]
    h = _rms_bf16(x, nw_ref[...])
    for c in range(d_ff // fc):
        g = _dot(h, wi_ref[:, c * fc:(c + 1) * fc])
        u = _dot(h, wi_ref[:, d_ff + c * fc:d_ff + (c + 1) * fc])
        act_ref[:, c * fc:(c + 1) * fc] = (g * _sigmoid(g) * u).astype(BF16)
    o_ref[...] = x + 0.5 * _dot(act_ref[...], wo_ref[...])


def _ffn(x, nw, wi, wo, layer):
    t, d = x.shape
    tm = _row_tile(t)
    d_ff = wo.shape[1]
    fc = 256 if d_ff % 256 == 0 else d_ff
    return pl.pallas_call(
        functools.partial(_ffn_body, d_ff=d_ff, fc=fc),
        grid=(t // tm,),
        in_specs=[pl.BlockSpec((tm, d), lambda i: (i, 0)),
                  _resident((None, 1, d), lambda i: (layer, 0, 0)),
                  _resident((None, d, 2 * d_ff), lambda i: (layer, 0, 0)),
                  _resident((None, d_ff, d), lambda i: (layer, 0, 0))],
        out_specs=pl.BlockSpec((tm, d), lambda i: (i, 0)),
        out_shape=jax.ShapeDtypeStruct((t, d), F32),
        scratch_shapes=[pltpu.VMEM((tm, d_ff), BF16)],
        compiler_params=_cparams(("arbitrary",)),
        name="ffn",
    )(x, nw, wi, wo)


def _inproj_body(x_ref, nw_ref, w_ref, *rest, fw, sw, gw, q_scale, kv_t):
    if kv_t:
        wkvt_ref, h_ref, q_ref, k_ref, v_ref, kv_ref, u_ref, gq_ref, z_ref, sm_ref = rest
    else:
        h_ref, q_ref, k_ref, v_ref, kv_ref, u_ref, gq_ref, z_ref, sm_ref = rest
    h = _rms_bf16(x_ref[...], nw_ref[...])
    h_ref[...] = h
    o = 0
    q = _dot(h, w_ref[:, o:o + fw]); o += fw
    q_ref[...] = (q * q_scale).astype(BF16)
    kv = _dot(h, w_ref[:, o:o + 2 * fw]); o += 2 * fw
    if kv_t:
        kv_ref[...] = _dot_nt(wkvt_ref[...], h)
    else:
        kv_ref[...] = kv
    k_ref[...] = kv[:, :fw].astype(BF16)
    v_ref[...] = kv[:, fw:].astype(BF16)
    u_ref[...] = _dot(h, w_ref[:, o:o + sw]); o += sw
    gq_ref[...] = _dot(h, w_ref[:, o:o + 3 * gw]); o += 3 * gw
    z_ref[...] = _dot(h, w_ref[:, o:o + gw]); o += gw
    sm_ref[...] = _dot(h, w_ref[:, o:o + LANES])


def _inproj(x, nw, w, layer, fw, sw, gw, q_scale, w_kvt=None, seq_len=None):
    t, d = x.shape
    n = w.shape[2]
    kv_t = w_kvt is not None
    tm = max(c for c in (128, 256, 384, 512) if seq_len % c == 0) if kv_t else _row_tile(t)
    row = lambda width: pl.BlockSpec((tm, width), lambda i: (i, 0))
    widths = (d, fw, fw, fw, 2 * fw, sw, 3 * gw, gw, LANES)
    dts = (BF16, BF16, BF16, BF16, F32, F32, F32, F32, F32)
    in_specs = [row(d), _resident((None, 1, d), lambda i: (layer, 0, 0)),
                _resident((None, d, n), lambda i: (layer, 0, 0))]
    out_specs = [row(wd) for wd in widths]
    out_shape = [jax.ShapeDtypeStruct((t, wd), dt) for wd, dt in zip(widths, dts)]
    args = (x, nw, w)
    if kv_t:
        per_seq = seq_len // tm
        in_specs.append(_resident((None, 2 * fw, d), lambda i: (layer, 0, 0)))
        out_specs[4] = pl.BlockSpec((None, 2 * fw, tm), lambda i: (i // per_seq, 0, i % per_seq))
        out_shape[4] = jax.ShapeDtypeStruct((t // seq_len, 2 * fw, seq_len), F32)
        args = (x, nw, w, w_kvt)
    return pl.pallas_call(
        functools.partial(_inproj_body, fw=fw, sw=sw, gw=gw, q_scale=q_scale, kv_t=kv_t),
        grid=(t // tm,),
        in_specs=in_specs,
        out_specs=out_specs,
        out_shape=out_shape,
        compiler_params=_cparams(("arbitrary",)),
        name="inproj",
    )(*args)


def _log_sigmoid(x):
    return jnp.minimum(x, 0.0) - jnp.log(1.0 + jnp.exp(-jnp.abs(x)))


def _fcum_body(lg_ref, b_ref, lf_ref, cum_ref, *, nblk):
    r = lax.broadcasted_iota(jnp.int32, (LANES, LANES), 0)
    c = lax.broadcasted_iota(jnp.int32, (LANES, LANES), 1)
    upper = (r <= c).astype(F32)
    carry = jnp.zeros((lg_ref.shape[0], 1), F32)
    for i in range(nblk):
        sl = slice(i * LANES, (i + 1) * LANES)
        lf = _log_sigmoid(lg_ref[:, sl] + b_ref[...])
        lf_ref[:, sl] = lf
        cs = _dot_hi(lf, upper) + carry
        carry = cs[:, LANES - 1:LANES]
        cum_ref[:, sl] = cs


def _fcum(logits, bias):
    ns, nh, length = logits.shape
    blk = pl.BlockSpec((None, nh, length), lambda s: (s, 0, 0))
    return pl.pallas_call(
        functools.partial(_fcum_body, nblk=length // LANES),
        grid=(ns,),
        in_specs=[blk, pl.BlockSpec((nh, 1), lambda s: (0, 0))],
        out_specs=[blk, blk],
        out_shape=[jax.ShapeDtypeStruct(logits.shape, F32)] * 2,
        compiler_params=_cparams(("arbitrary",)),
        name="fcum",
    )(logits, bias)


def _fox_prompt_body(q_ref, k_ref, v_ref, fk_ref, o_ref, m_sc, l_sc, acc_sc, s_sc, *, tq, hd, n_heads):
    qi = pl.program_id(1)
    lane = lax.broadcasted_iota(jnp.int32, (1, LANES), 1)
    lo = lane < hd
    reps = tq // LANES
    causal = lax.broadcasted_iota(jnp.int32, (tq, tq), 1) <= lax.broadcasted_iota(jnp.int32, (tq, tq), 0)
    for p in range(n_heads * hd // LANES):
        qp = q_ref[:, p * LANES:(p + 1) * LANES]
        qm = (jnp.where(lo, qp, jnp.zeros_like(qp)), jnp.where(lo, jnp.zeros_like(qp), qp))
        m_sc[...] = jnp.full(m_sc.shape, NEG_BIG, F32)
        l_sc[...] = jnp.zeros(l_sc.shape, F32)
        acc_sc[...] = jnp.zeros(acc_sc.shape, F32)

        def scores(j, p=p, qm=qm):
            start = pl.multiple_of(j * tq, tq)
            kb = k_ref[pl.ds(start, tq), p * LANES:(p + 1) * LANES]
            fkb = fk_ref[:, pl.ds(start, tq)] * LOG2E
            return [_dot_nt(qm[hh], kb) - fkb[2 * p + hh:2 * p + hh + 1, :] for hh in range(2)]

        def consume(j, s, masked, p=p):
            vb = v_ref[pl.ds(pl.multiple_of(j * tq, tq), tq), p * LANES:(p + 1) * LANES]
            if masked:
                s = [jnp.where(causal, sh, NEG_BIG) for sh in s]
            m_old = [m_sc[hh] for hh in range(2)]
            m_new = [jnp.maximum(m_old[hh], jnp.max(s[hh], axis=-1, keepdims=True)) for hh in range(2)]
            alpha = [jnp.exp2(m_old[hh] - m_new[hh]) for hh in range(2)]
            e = [jnp.exp2((s[hh] - pltpu.repeat(m_new[hh], reps, 1)).astype(BF16)) for hh in range(2)]
            pv = [_dot(e[hh], jnp.concatenate([vb, jnp.ones_like(vb)], axis=1)) for hh in range(2)]
            for hh in range(2):
                l_sc[hh] = alpha[hh] * l_sc[hh] + pv[hh][:, LANES:]
                m_sc[hh] = m_new[hh]
            acc_sc[...] = (jnp.where(lo, alpha[0], alpha[1]) * acc_sc[...]
                           + jnp.where(lo, pv[0][:, :LANES], pv[1][:, :LANES]))

        s0 = scores(0)
        s_sc[0], s_sc[1] = s0[0], s0[1]

        def body(j, carry):
            s_cur = [s_sc[0], s_sc[1]]
            s_nxt = scores(j + 1)
            consume(j, s_cur, False)
            s_sc[0], s_sc[1] = s_nxt[0], s_nxt[1]
            return carry

        lax.fori_loop(0, qi, body, 0)
        consume(qi, [s_sc[0], s_sc[1]], True)
        inv = 1.0 / jnp.where(lo, l_sc[0], l_sc[1])
        o_ref[:, p * LANES:(p + 1) * LANES] = (acc_sc[...] * inv).astype(BF16)


def _fox_prompt(q, k, v, fcum, nb, lp, n_heads, hd, tq):
    fw = n_heads * hd
    nq = lp // tq
    return pl.pallas_call(
        functools.partial(_fox_prompt_body, tq=tq, hd=hd, n_heads=n_heads),
        grid=(nb, nq),
        in_specs=[pl.BlockSpec((tq, fw), lambda b, i: (b * nq + i, 0)),
                  pl.BlockSpec((lp, fw), lambda b, i: (b, 0)),
                  pl.BlockSpec((lp, fw), lambda b, i: (b, 0)),
                  pl.BlockSpec((None, n_heads, lp), lambda b, i: (b, 0, 0))],
        out_specs=pl.BlockSpec((tq, fw), lambda b, i: (b * nq + i, 0)),
        out_shape=jax.ShapeDtypeStruct((nb * lp, fw), BF16),
        scratch_shapes=[pltpu.VMEM((2, tq, LANES), F32), pltpu.VMEM((2, tq, LANES), F32),
                        pltpu.VMEM((tq, LANES), F32), pltpu.VMEM((2, tq, tq), F32)],
        compiler_params=_cparams(("arbitrary", "arbitrary")),
        name="fox_prompt",
    )(q, k, v, fcum)


def _fox_sample_body(pt_ref, qbd_ref, cn_ref, kn_ref, vn_ref, *rest, g_pages, n_q, n_heads, hd, page):
    kt_refs = rest[:g_pages]
    vt_refs = rest[g_pages:2 * g_pages]
    lf_ref, o_ref, m_sc, l_sc, acc_sc, car_sc = rest[2 * g_pages:]
    g = pl.program_id(1)
    n_pages = pl.num_programs(1) * g_pages
    lfs = [lf_ref[pt_ref[pl.program_id(0), n_pages - 1 - (g * g_pages + i)]] for i in range(g_pages)]
    nr = n_q * n_heads
    fw = n_heads * hd
    qbd = qbd_ref[...]
    rr = lax.broadcasted_iota(jnp.int32, (nr, 1), 0)
    qidx = rr // n_heads
    lane = lax.broadcasted_iota(jnp.int32, (1, LANES), 1)
    col_new = jnp.concatenate([cn_ref[...]] * n_q, axis=0)
    row_new = jnp.sum(jnp.where(lane == qidx, col_new, 0.0), axis=-1, keepdims=True)

    def update(slots, scores, pv_ofs):
        m_old = [m_sc[i] for i in slots]
        m_new = [jnp.maximum(mo, jnp.max(s, axis=-1, keepdims=True)) for mo, s in zip(m_old, scores)]
        alpha = [jnp.exp2(mo - mn) for mo, mn in zip(m_old, m_new)]
        e = [jnp.exp2(s - mn) for s, mn in zip(scores, m_new)]
        pv = [f(ee.astype(BF16)) for f, ee in zip(pv_ofs, e)]
        for n, i in enumerate(slots):
            l_sc[i] = alpha[n] * l_sc[i] + jnp.sum(e[n], axis=-1, keepdims=True)
            m_sc[i] = m_new[n]
            acc_sc[i] = alpha[n] * acc_sc[i] + pv[n]

    @pl.when(g == 0)
    def _():
        m_sc[...] = jnp.full(m_sc.shape, NEG_BIG, F32)
        l_sc[...] = jnp.zeros(l_sc.shape, F32)
        acc_sc[...] = jnp.zeros(acc_sc.shape, F32)
        car_sc[...] = jnp.zeros(car_sc.shape, F32)
        s = _dot_nt(qbd, kn_ref[...]) + (row_new - col_new) * LOG2E
        s = jnp.where((lane <= qidx) & (lane < n_q), s, NEG_BIG)
        update([0], [s], [lambda e: _dot(e, vn_ref[...])])

    r = lax.broadcasted_iota(jnp.int32, (page, page), 0)
    c = lax.broadcasted_iota(jnp.int32, (page, page), 1)
    later = (r > c).astype(F32)
    suffix = _dot_hi(jnp.concatenate(lfs, axis=0), later)
    carry = car_sc[...]
    scores, pv_ofs = [], []
    for i in range(g_pages):
        bias = jnp.concatenate([suffix[i * n_heads:(i + 1) * n_heads] + carry] * n_q, axis=0) + row_new
        kt = kt_refs[i][...].reshape(fw, page).astype(BF16)
        scores.append(_dot(qbd, kt) + bias * LOG2E)
        vt = vt_refs[i][...].reshape(fw, page).astype(BF16)
        pv_ofs.append(lambda e, vt=vt: _dot_nt(e, vt))
        carry = carry + jnp.sum(lfs[i], axis=-1, keepdims=True)
    car_sc[...] = carry
    update(list(range(g_pages)), scores, pv_ofs)

    @pl.when(g == pl.num_programs(1) - 1)
    def _():
        ms = [m_sc[i] for i in range(g_pages)]
        m_all = functools.reduce(jnp.maximum, ms)
        wts = [jnp.exp2(mi - m_all) for mi in ms]
        l_all = sum(w * l_sc[i] for i, w in enumerate(wts))
        o = sum(w * acc_sc[i] for i, w in enumerate(wts)) / l_all
        lane_w = lax.broadcasted_iota(jnp.int32, (1, fw), 1)
        o = jnp.where(lane_w // hd == rr % n_heads, o, 0.0)
        o_ref[...] = jnp.sum(o.reshape(n_q, n_heads, fw), axis=1).astype(BF16)


def _fox_sample(page_table, qbd, cum_new, k_new, v_new, kt, vt, lft, layer, g_pages, n_q):
    nb, n_pages = page_table.shape
    _, _, n_heads, hd, page = kt.shape
    fw = n_heads * hd
    nr = n_q * n_heads
    n_groups = n_pages // g_pages

    def page_map(i):
        return lambda b, g, pt: (layer, pt[b, n_pages - 1 - (g * g_pages + i)], 0, 0, 0)

    per_b = lambda shape: pl.BlockSpec((None,) + shape, lambda b, g, pt: (b, 0, 0))
    in_specs = ([per_b((nr, fw)), per_b((n_heads, LANES)), per_b((LANES, fw)), per_b((LANES, fw))]
                + [pl.BlockSpec((None, None, n_heads, hd, page), page_map(i)) for i in range(g_pages)]
                + [pl.BlockSpec((None, None, n_heads, hd, page), page_map(i)) for i in range(g_pages)]
                + [_resident((None,) + lft.shape[1:], lambda b, g, pt: (layer, 0, 0, 0))])
    return pl.pallas_call(
        functools.partial(_fox_sample_body, g_pages=g_pages, n_q=n_q, n_heads=n_heads, hd=hd, page=page),
        grid_spec=pltpu.PrefetchScalarGridSpec(
            num_scalar_prefetch=1, grid=(nb, n_groups), in_specs=in_specs,
            out_specs=per_b((n_q, fw)),
            scratch_shapes=[pltpu.VMEM((g_pages, nr, 1), F32), pltpu.VMEM((g_pages, nr, 1), F32),
                            pltpu.VMEM((g_pages, nr, fw), F32), pltpu.VMEM((n_heads, 1), F32)]),
        out_shape=jax.ShapeDtypeStruct((nb, n_q, fw), BF16),
        compiler_params=_cparams(("arbitrary", "arbitrary")),
        name="fox_sample",
    )(page_table, qbd, cum_new, k_new, v_new, *([kt] * g_pages), *([vt] * g_pages), lft)


def _gelu_tanh(x):
    return 0.5 * x * (1.0 + jnp.tanh(math.sqrt(2.0 / math.pi) * (x + 0.044715 * x * x * x)))


def _s5_tables(a_re, a_im, b_re, b_im, c_re, c_im, d_skip, log_dt, n_last):
    ck = S5_CHUNK
    g, p, gs = b_re.shape
    gt = LANES // gs
    nt = g // gt
    dt = jnp.exp(log_dt)[:, None]
    mag_l, ang = a_re * dt, a_im * dt

    def apow_fn(n):
        m = jnp.exp(mag_l * n)
        return m * jnp.cos(ang * n), m * jnp.sin(ang * n)

    ab_re, ab_im = apow_fn(1.0)
    den = a_re * a_re + a_im * a_im
    cf_re = ((ab_re - 1.0) * a_re + ab_im * a_im) / den
    cf_im = (ab_im * a_re - (ab_re - 1.0) * a_im) / den
    bb_re = cf_re[..., None] * b_re - cf_im[..., None] * b_im
    bb_im = cf_re[..., None] * b_im + cf_im[..., None] * b_re
    pw_re, pw_im = jax.vmap(apow_fn)(jnp.arange(ck + 1, dtype=F32))
    cab_re = c_re[None] * pw_re[:, :, None, :] - c_im[None] * pw_im[:, :, None, :]
    cab_im = c_re[None] * pw_im[:, :, None, :] + c_im[None] * pw_re[:, :, None, :]
    w = (jnp.einsum('lgcp,gpd->lgcd', cab_re[:ck], bb_re, precision=HI)
         - jnp.einsum('lgcp,gpd->lgcd', cab_im[:ck], bb_im, precision=HI))
    w = w.at[0].add(jax.vmap(jnp.diag)(d_skip))
    wc = jnp.transpose(w, (1, 3, 0, 2)).reshape(nt, gt * gs, ck * gs)
    e_pow = ck - 1 - jnp.arange(ck)
    f_re = pw_re[e_pow][..., None] * bb_re[None] - pw_im[e_pow][..., None] * bb_im[None]
    f_im = pw_re[e_pow][..., None] * bb_im[None] + pw_im[e_pow][..., None] * bb_re[None]
    f_rows = lambda f: jnp.transpose(f, (1, 0, 3, 2)).reshape(nt, gt, ck, gs, p)
    fc = jnp.transpose(jnp.concatenate([f_rows(f_re), f_rows(f_im)], axis=-1), (0, 2, 1, 3, 4)
                       ).reshape(nt, ck, gt * gs, 2 * p)
    e_rows = lambda m: jnp.transpose(m, (1, 0, 3, 2)).reshape(nt, gt, ck, p, gs)
    ec = jnp.transpose(jnp.stack([e_rows(cab_re[1:]), e_rows(-cab_im[1:])], axis=1), (0, 3, 1, 2, 4, 5)
                       ).reshape(nt, ck, 2 * gt * p, gs)

    tile_vec = lambda v: v.reshape(nt, gt * p)
    nlev = 16
    lev = [apow_fn(float(ck * 2 ** i)) for i in range(nlev)]
    apow = jnp.stack([jnp.stack([tile_vec(r), tile_vec(i)], axis=1) for r, i in lev], axis=1)
    al_re, al_im = apow_fn(float(n_last))
    alast = jnp.stack([tile_vec(al_re), tile_vec(al_im)], axis=1)
    wcat, fmat, emat = _s5_expand(wc.astype(BF16), fc.astype(BF16), ec.astype(BF16), gs, p)
    return wcat, fmat, emat, apow, alast


def _s5_expand_body(wc_ref, fc_ref, ec_ref, w_ref, f_ref, e_ref, *, ck, gs, p):
    gt = LANES // gs
    iota = lambda shape, axis: lax.broadcasted_iota(jnp.int32, shape, axis)
    bf = lambda m: m.astype(F32).astype(BF16)
    r, q = iota((ck * gs, ck * LANES), 0), iota((ck * gs, ck * LANES), 1)
    spread_w = bf((r // gs == q // LANES) & (r % gs == q % gs))
    r, q = iota((LANES, ck * LANES), 0), iota((LANES, ck * LANES), 1)
    w_ref[...] = jnp.where(r // gs == (q % LANES) // gs, _dot(wc_ref[...], spread_w), 0.0).astype(BF16)
    half = gt * p
    r, q = iota((2 * p, 2 * half), 0), iota((2 * p, 2 * half), 1)
    spread_f = bf((r // p == q // half) & (r % p == q % p))
    r, q = iota((LANES, 2 * half), 0), iota((LANES, 2 * half), 1)
    keep_f = r // gs == (q % half) // p
    r, q = iota((gs, LANES), 0), iota((gs, LANES), 1)
    spread_e = bf(r == q % gs)
    r, q = iota((2 * half, LANES), 0), iota((2 * half, LANES), 1)
    keep_e = (r % half) // p == q // gs
    for s in range(ck):
        f_ref[s // 2, (s % 2) * LANES:(s % 2 + 1) * LANES, :] = jnp.where(
            keep_f, _dot(fc_ref[s], spread_f), 0.0).astype(BF16)
        e_ref[s // 2, :, (s % 2) * LANES:(s % 2 + 1) * LANES] = jnp.where(
            keep_e, _dot(ec_ref[s], spread_e), 0.0).astype(BF16)


def _s5_expand(wc, fc, ec, gs, p):
    nt = wc.shape[0]
    ck = S5_CHUNK
    sw = 2 * (LANES // gs) * p
    blk = lambda a: pl.BlockSpec((None,) + a.shape[1:], lambda j: (j,) + (0,) * (a.ndim - 1))
    shapes = [jax.ShapeDtypeStruct((nt, LANES, ck * LANES), BF16),
              jax.ShapeDtypeStruct((nt, ck // 2, 2 * LANES, sw), BF16),
              jax.ShapeDtypeStruct((nt, ck // 2, sw, 2 * LANES), BF16)]
    return pl.pallas_call(
        functools.partial(_s5_expand_body, ck=ck, gs=gs, p=p),
        grid=(nt,),
        in_specs=[blk(wc), blk(fc), blk(ec)],
        out_specs=[blk(s) for s in shapes],
        out_shape=shapes,
        compiler_params=_cparams(("arbitrary",)),
        name="s5_expand",
    )(wc, fc, ec)


def _s5_local_body(u_ref, w_ref, f_ref, y_ref, z_ref, acc_ref, *, ck, n):
    accz = None
    zero = jnp.zeros((LANES, LANES), BF16)
    for s in range(0, ck, 2):
        us = jnp.concatenate([u_ref[pl.ds(s, n, stride=ck), :].astype(BF16),
                              u_ref[pl.ds(s + 1, n, stride=ck), :].astype(BF16)], axis=1)
        wide = (ck - s) * LANES
        w_two = jnp.concatenate([w_ref[:, :wide],
                                 jnp.concatenate([zero, w_ref[:, :wide - LANES]], axis=1)], axis=0)
        y = _dot(us, w_two)
        if s == 0:
            acc_ref[...] = y
        else:
            acc_ref[:, s * LANES:] += y
        z = _dot(us, f_ref[s // 2])
        accz = z if accz is None else accz + z
    for t in range(ck):
        y_ref[pl.ds(t, n, stride=ck), :] = acc_ref[:, t * LANES:(t + 1) * LANES]
    z_ref[...] = accz


def _s5_local(u, wcat, fmat, rows_per_step):
    r, width = u.shape
    ck = S5_CHUNK
    nt = width // LANES
    sw = fmat.shape[3]
    n = rows_per_step // ck
    return pl.pallas_call(
        functools.partial(_s5_local_body, ck=ck, n=n),
        grid=(nt, r // rows_per_step),
        in_specs=[pl.BlockSpec((rows_per_step, LANES), lambda j, i: (i, j)),
                  pl.BlockSpec((None, LANES, ck * LANES), lambda j, i: (j, 0, 0)),
                  pl.BlockSpec((None, ck // 2, 2 * LANES, sw), lambda j, i: (j, 0, 0, 0))],
        out_specs=[pl.BlockSpec((rows_per_step, LANES), lambda j, i: (i, j)),
                   pl.BlockSpec((None, n, sw), lambda j, i: (j, i, 0))],
        out_shape=[jax.ShapeDtypeStruct((r, width), F32), jax.ShapeDtypeStruct((nt, r // ck, sw), F32)],
        scratch_shapes=[pltpu.VMEM((n, ck * LANES), F32)],
        compiler_params=_cparams(("arbitrary", "arbitrary")),
        name="s5_local",
    )(u, wcat, fmat)


def _s5_scan_body(z_ref, ap_ref, xs_ref, xf_ref, *, nc, half, last):
    xr = z_ref[:, :half]
    xi = z_ref[:, half:]
    row = lax.broadcasted_iota(jnp.int32, (nc, 1), 0)
    lev, s = 0, 1
    while s < nc:
        ar = ap_ref[lev, 0:1, :]
        ai = ap_ref[lev, 1:2, :]
        keep = row >= s
        sr = jnp.where(keep, pltpu.roll(xr, s, 0), 0.0)
        si = jnp.where(keep, pltpu.roll(xi, s, 0), 0.0)
        xr, xi = xr + ar * sr - ai * si, xi + ar * si + ai * sr
        lev, s = lev + 1, s * 2
    xf_ref[:, :half] = xr[last:last + 1, :]
    xf_ref[:, half:] = xi[last:last + 1, :]
    xs_ref[:, :half] = jnp.where(row >= 1, pltpu.roll(xr, 1, 0), 0.0)
    xs_ref[:, half:] = jnp.where(row >= 1, pltpu.roll(xi, 1, 0), 0.0)


def _s5_scan(z, apow, nb, n_real):
    nt, n, sw = z.shape
    nc = n // nb
    nlev = apow.shape[1]
    return pl.pallas_call(
        functools.partial(_s5_scan_body, nc=nc, half=sw // 2, last=n_real - 1),
        grid=(nt, nb),
        in_specs=[pl.BlockSpec((None, nc, sw), lambda j, b: (j, b, 0)),
                  pl.BlockSpec((None, nlev, 2, sw // 2), lambda j, b: (j, 0, 0, 0))],
        out_specs=[pl.BlockSpec((None, nc, sw), lambda j, b: (j, b, 0)),
                   pl.BlockSpec((None, None, 1, sw), lambda j, b: (j, b, 0, 0))],
        out_shape=[jax.ShapeDtypeStruct((nt, n, sw), F32), jax.ShapeDtypeStruct((nt, nb, 1, sw), F32)],
        compiler_params=_cparams(("arbitrary", "arbitrary")),
        name="s5_scan",
    )(z, apow)


def _s5_out_body(y_ref, xs_ref, e_ref, o_ref, act_ref, *, ck, n, first):
    xs = xs_ref[...].astype(BF16)
    for t in range(0, ck, 2):
        pair = [y_ref[pl.ds(t + i, n, stride=ck), :] for i in range(2)]
        if t >= first:
            both = _dot(xs, e_ref[(t - first) // 2])
            pair = [pair[i] + both[:, i * LANES:(i + 1) * LANES] for i in range(2)]
        for i in range(2):
            act_ref[pl.ds(t + i, n, stride=ck), :] = _gelu_tanh(pair[i])
    o_ref[...] = act_ref[...].astype(BF16)


def _s5_out(ylocal, xstart, emat, rows_per_step, first):
    r, width = ylocal.shape
    ck = S5_CHUNK
    nt = width // LANES
    sw = xstart.shape[2]
    n = rows_per_step // ck
    return pl.pallas_call(
        functools.partial(_s5_out_body, ck=ck, n=n, first=first),
        grid=(nt, r // rows_per_step),
        in_specs=[pl.BlockSpec((rows_per_step, LANES), lambda j, i: (i, j)),
                  pl.BlockSpec((None, n, sw), lambda j, i: (j, i, 0)),
                  pl.BlockSpec((None, ck // 2, sw, 2 * LANES), lambda j, i: (j, 0, 0, 0))],
        out_specs=pl.BlockSpec((rows_per_step, LANES), lambda j, i: (i, j)),
        out_shape=jax.ShapeDtypeStruct((r, width), BF16),
        scratch_shapes=[pltpu.VMEM((rows_per_step, LANES), F32)],
        compiler_params=_cparams(("arbitrary", "arbitrary")),
        name="s5_out",
    )(ylocal, xstart, emat)


def _s5_step_body(z_ref, x0_ref, al_ref, xf_ref, *, half):
    ar, ai = al_ref[0:1, :], al_ref[1:2, :]
    xr, xi = x0_ref[:, :half], x0_ref[:, half:]
    xf_ref[:, :half] = z_ref[:, :half] + ar * xr - ai * xi
    xf_ref[:, half:] = z_ref[:, half:] + ar * xi + ai * xr


def _s5_step(z, x0, alast):
    nt, n, sw = z.shape
    blk = pl.BlockSpec((None, n, sw), lambda j: (j, 0, 0))
    return pl.pallas_call(
        functools.partial(_s5_step_body, half=sw // 2),
        grid=(nt,),
        in_specs=[blk, blk, pl.BlockSpec((None, 2, sw // 2), lambda j: (j, 0, 0))],
        out_specs=blk,
        out_shape=jax.ShapeDtypeStruct(z.shape, F32),
        compiler_params=_cparams(("arbitrary",)),
        name="s5_step",
    )(z, x0, alast)


def _gdn_body(x_ref, z_ref, sm_ref, ci_ref, s0_ref, cw_ref, na_ref, db_ref, nw_ref, o_ref, sf_ref, s_sc, car_sc,
              *, c, sb, n_valid, n_heads, hd, kconv, a_lane, b_lane):
    ci = pl.program_id(1)
    gw = n_heads * hd

    hist = ci_ref.shape[1]

    @pl.when(ci == 0)
    def _():
        s_sc[...] = s0_ref[...]
        car_sc[:, :hist, :] = ci_ref[...]

    r = lax.broadcasted_iota(jnp.int32, (c, c), 0)
    cc = lax.broadcasted_iota(jnp.int32, (c, c), 1)
    incl = r >= cc
    strict = r > cc
    eye = (r == cc).astype(F32)
    row = ci * c + lax.broadcasted_iota(jnp.int32, (c, 1), 0)
    chains = []
    for b in range(sb):
        car_sc[b, hist:, :] = x_ref[b]
        y = None
        for j in range(kconv):
            term = car_sc[b, pl.ds(hist - (kconv - 1) + j, c), :] * cw_ref[j:j + 1, :]
            y = term if y is None else y + term
        car_sc[b, :hist, :] = car_sc[b, c:, :]
        qkv = y * _sigmoid(y)

        sm = sm_ref[b]
        sp_in = sm + db_ref[...]
        softplus = jnp.maximum(sp_in, 0.0) + jnp.log(1.0 + jnp.exp(-jnp.abs(sp_in)))
        gfull = na_ref[...] * softplus
        beta_full = _sigmoid(sm)
        if n_valid is not None:
            live = row < n_valid
            gfull = jnp.where(live, gfull, 0.0)
            beta_full = jnp.where(live, beta_full, 0.0)
            qkv = jnp.where(live, qkv, 0.0)
        cum_full = _dot_hi(incl.astype(F32), gfull)
        cum_t = cum_full.T
        for h in range(n_heads):
            q = qkv[:, h * hd:(h + 1) * hd]
            k = qkv[:, gw + h * hd:gw + (h + 1) * hd]
            v = qkv[:, 2 * gw + h * hd:2 * gw + (h + 1) * hd]
            q = q * lax.rsqrt(jnp.sum(q * q, axis=-1, keepdims=True) + 1e-6) * (hd ** -0.5)
            k = k * lax.rsqrt(jnp.sum(k * k, axis=-1, keepdims=True) + 1e-6)
            cum = cum_full[:, a_lane + h:a_lane + h + 1]
            cum_r = cum_t[a_lane + h:a_lane + h + 1, :]
            beta = beta_full[:, b_lane + h:b_lane + h + 1]
            decay = jnp.where(incl, jnp.exp(jnp.where(incl, cum - cum_r, 0.0)), 0.0)
            kb = k * beta
            k16 = k.astype(BF16)
            ecum = jnp.exp(cum)
            g_last = cum[c - 1:c, :]
            chains.append(dict(
                b=b, h=h, k16=k16, decay=decay, qe16=(q * ecum).astype(BF16), q16=q.astype(BF16),
                rhs=jnp.concatenate([v * beta, kb * ecum], axis=-1), kb16=kb.astype(BF16),
                kd=k * jnp.exp(g_last - cum), s_scale=jnp.exp(g_last)))
    for ch in chains:
        ch['lower'] = jnp.where(strict, _dot_nt(ch['kb16'], ch['k16']) * ch['decay'], 0.0)
        ch['aqk16'] = jnp.where(incl, _dot_nt(ch['q16'], ch['k16']) * ch['decay'], 0.0).astype(BF16)
    for ch in chains:
        ch['t_inv'] = eye - ch['lower']
        l_hi, l_lo = _split(ch['lower'])
        ch['pw'] = _dot3s(l_hi, l_lo, l_hi, l_lo)
    span = 2
    while span < c:
        span *= 2
        for ch in chains:
            p_hi, p_lo = _split(ch['pw'])
            t_hi, t_lo = _split(ch['t_inv'])
            if span < c:
                both = _dot3s(p_hi, p_lo, jnp.concatenate([t_hi, p_hi], axis=1),
                              jnp.concatenate([t_lo, p_lo], axis=1))
                ch['t_inv'] = ch['t_inv'] + both[:, :c]
                ch['pw'] = both[:, c:]
            else:
                ch['t_inv'] = ch['t_inv'] + _dot3s(p_hi, p_lo, t_hi, t_lo)
    for ch in chains:
        ch['sol'] = _dot3(ch['t_inv'], ch['rhs'])
    for ch in chains:
        s_h = s_sc[ch['b'], ch['h']]
        s16 = s_h.astype(BF16)
        v_new = ch['sol'][:, :hd] - _dot(ch['sol'][:, hd:].astype(BF16), s16)
        ch['vn16'] = v_new.astype(BF16)
        ch['o'] = _dot(ch['qe16'], s16)
        ch['s_dec'] = s_h * ch['s_scale']
    for ch in chains:
        b, h = ch['b'], ch['h']
        o = ch['o'] + _dot(ch['aqk16'], ch['vn16'])
        s_sc[b, h] = ch['s_dec'] + _dot(ch['kd'].T.astype(BF16), ch['vn16'])
        o = o * lax.rsqrt(jnp.mean(o * o, axis=-1, keepdims=True) + RMS_EPS) * nw_ref[...]
        zz = z_ref[b, :, h * hd:(h + 1) * hd]
        o_ref[b, :, h * hd:(h + 1) * hd] = (o * (zz * _sigmoid(zz))).astype(BF16)

    @pl.when(ci == pl.num_programs(1) - 1)
    def _():
        sf_ref[...] = s_sc[...]


def _gdn(x, z, sm, conv_init, s0, conv_w, neg_a, dt_b, norm_w, sb, n_valid, a_lane, b_lane):
    ns, length, w3 = x.shape
    n_heads, hd = s0.shape[1], s0.shape[2]
    gw = n_heads * hd
    c = GDN_CHUNK
    kconv = conv_w.shape[0]
    hist = conv_init.shape[1]
    rowblk = lambda width: pl.BlockSpec((sb, c, width), lambda s, i: (s, i, 0))
    const = lambda shape: pl.BlockSpec(shape, lambda s, i: (0,) * len(shape))
    return pl.pallas_call(
        functools.partial(_gdn_body, c=c, sb=sb, n_valid=n_valid, n_heads=n_heads, hd=hd, kconv=kconv,
                          a_lane=a_lane, b_lane=b_lane),
        grid=(ns // sb, length // c),
        in_specs=[rowblk(w3), rowblk(gw), rowblk(LANES),
                  pl.BlockSpec((sb, hist, w3), lambda s, i: (s, 0, 0)),
                  pl.BlockSpec((sb, n_heads, hd, hd), lambda s, i: (s, 0, 0, 0)),
                  const((kconv, w3)), const((1, LANES)), const((1, LANES)), const((1, hd))],
        out_specs=[rowblk(gw), pl.BlockSpec((sb, n_heads, hd, hd), lambda s, i: (s, 0, 0, 0))],
        out_shape=[jax.ShapeDtypeStruct((ns, length, gw), BF16), jax.ShapeDtypeStruct(s0.shape, F32)],
        scratch_shapes=[pltpu.VMEM((sb, n_heads, hd, hd), F32), pltpu.VMEM((sb, hist + c, w3), F32)],
        compiler_params=_cparams(("arbitrary", "arbitrary")),
        name="gdn",
    )(x, z, sm, conv_init, s0, conv_w, neg_a, dt_b, norm_w)


def _merge_body(x_ref, h_ref, of_ref, z5_ref, og_ref, wf_ref, w5_ref, wg_ref, wgate_ref, wo_ref, o_ref, *, d):
    fox_br = _dot(of_ref[...], wf_ref[...])
    glu = _dot(z5_ref[...], w5_ref[...])
    s5_br = glu[:, :d] * _sigmoid(glu[:, d:])
    gdn_br = _dot(og_ref[...], wg_ref[...])
    h = h_ref[...]
    merged = (_sigmoid(_dot(h, wgate_ref[:, :d])) * fox_br
              + _sigmoid(_dot(h, wgate_ref[:, d:2 * d])) * s5_br
              + _sigmoid(_dot(h, wgate_ref[:, 2 * d:])) * gdn_br)
    o_ref[...] = x_ref[...] + _dot(merged.astype(BF16), wo_ref[...])


def _merge(x, h, o_fox, z5, o_gdn, wf, w5, wg, wgate, wo, layer):
    t, d = x.shape
    tm = _row_tile(t)
    row = lambda a: pl.BlockSpec((tm, a.shape[1]), lambda i: (i, 0))
    res = lambda a: _resident((None,) + a.shape[1:], lambda i: (layer, 0, 0))
    return pl.pallas_call(
        functools.partial(_merge_body, d=d),
        grid=(t // tm,),
        in_specs=[row(x), row(h), row(o_fox), row(z5), row(o_gdn), res(wf), res(w5), res(wg), res(wgate), res(wo)],
        out_specs=pl.BlockSpec((tm, d), lambda i: (i, 0)),
        out_shape=jax.ShapeDtypeStruct((t, d), F32),
        compiler_params=_cparams(("arbitrary",)),
        name="merge",
    )(x, h, o_fox, z5, o_gdn, wf, w5, wg, wgate, wo)


def _final_norm_body(x_ref, w_ref, o_ref):
    x = x_ref[...]
    ms = jnp.mean(x * x, axis=-1, keepdims=True)
    o_ref[...] = x * lax.rsqrt(ms + RMS_EPS) * w_ref[...]


def _final_norm(x, w):
    t, d = x.shape
    tm = _row_tile(t)
    return pl.pallas_call(
        _final_norm_body,
        grid=(t // tm,),
        in_specs=[pl.BlockSpec((tm, d), lambda i: (i, 0)), pl.BlockSpec((1, d), lambda i: (0, 0))],
        out_specs=pl.BlockSpec((tm, d), lambda i: (i, 0)),
        out_shape=jax.ShapeDtypeStruct((t, d), F32),
        compiler_params=_cparams(("arbitrary",)),
        name="final_norm",
    )(x, w)


def kernel(x_prompt, x_sample, cache_fox_k, cache_fox_v, cache_fox_logf, state_s5_re, state_s5_im, state_gdn, state_gdn_conv, page_table, meta_tokens, norm_ffn1, w_ffn1_in, w_ffn1_out, norm_mix, w_in, fox_b_f, w_fox_br, s5_A_re, s5_A_im, s5_B_re, s5_B_im, s5_C_re, s5_C_im, s5_D, s5_log_dt, w_s5_glu, gdn_conv_w, gdn_A_log, gdn_dt_bias, gdn_norm, w_gdn_br, w_out, norm_ffn2, w_ffn2_in, w_ffn2_out, norm_final):
    nb, seq, d = x_prompt.shape
    ds_b, ds_t, _ = x_sample.shape
    depth = w_in.shape[0]
    n_meta = meta_tokens.shape[0]
    fh, fhd = cache_fox_k.shape[3], cache_fox_k.shape[4]
    fw = fh * fhd
    page = cache_fox_k.shape[2]
    sg, sp, sgs = s5_B_re.shape[1:]
    sw = sg * sgs
    gh, ghd = state_gdn.shape[2], state_gdn.shape[3]
    gw = gh * ghd
    kconv = gdn_conv_w.shape[1]
    ck = S5_CHUNK
    gc = GDN_CHUNK
    assert fw % LANES == 0 and 2 * fhd == LANES and ghd == LANES and sw % LANES == 0 and LANES % sgs == 0
    assert ds_t <= ck and (ck - ds_t) % 2 == 0 and ds_t >= kconv - 1 and ds_t <= gc
    assert (n_meta + seq) % ck == 0 and n_meta + seq >= kconv - 1
    assert page == LANES and fh + 2 * gh <= LANES

    lreal = n_meta + seq
    pad = (-lreal) % LANES
    lp = lreal + pad
    tp = nb * lp
    ts = ds_b * ds_t
    tq = max(c for c in (128, 256, 384, 512) if lp % c == 0)
    n_pages = page_table.shape[1]
    g_pages = max(c for c in (1, 2, 4, 8) if n_pages % c == 0)
    hist = 8
    nt = sw // LANES
    gt = LANES // sgs
    ssw = 2 * gt * sp

    sizes = (fw, fw, fw, fh, sw, 3 * gw, gh, gh, gw, 3 * d)
    offs = [0]
    for s in sizes:
        offs.append(offs[-1] + s)
    col = lambda i: w_in[:, :, offs[i]:offs[i + 1]]
    small = jnp.concatenate([col(3), col(6), col(7),
                             jnp.zeros((depth, d, LANES - fh - 2 * gh), F32)], axis=2)
    a_lane, b_lane = fh, fh + gh
    w_proj = jnp.concatenate([col(0), col(1), col(2), col(4), col(5), col(8), small], axis=2).astype(BF16)
    w_gate = col(9).astype(BF16)
    w_kvt = jnp.transpose(w_in[:, :, offs[1]:offs[3]], (0, 2, 1)).astype(BF16)
    q_scale = fhd ** -0.5 * LOG2E
    wi1, wo1 = w_ffn1_in.astype(BF16), w_ffn1_out.astype(BF16)
    wi2, wo2 = w_ffn2_in.astype(BF16), w_ffn2_out.astype(BF16)
    wf16, w516, wg16, wo16 = (w.astype(BF16) for w in (w_fox_br, w_s5_glu, w_gdn_br, w_out))
    nrm = lambda w: w.reshape(depth, 1, d)
    n1, nm, n2 = nrm(norm_ffn1), nrm(norm_mix), nrm(norm_ffn2)
    lane_vec = lambda v, at: jnp.zeros((depth, 1, LANES), F32).at[:, 0, at:at + v.shape[1]].set(v)
    neg_a = lane_vec(-jnp.exp(gdn_A_log), a_lane)
    dt_b = lane_vec(gdn_dt_bias, a_lane)
    s5_tab = [_s5_tables(s5_A_re[l], s5_A_im[l], s5_B_re[l], s5_B_im[l], s5_C_re[l], s5_C_im[l], s5_D[l],
                         s5_log_dt[l], ds_t) for l in range(depth)]

    kt_cache = jnp.transpose(cache_fox_k, (0, 1, 3, 4, 2))
    vt_cache = jnp.transpose(cache_fox_v, (0, 1, 3, 4, 2))
    lf_cache = jnp.transpose(cache_fox_logf, (0, 1, 3, 2))

    meta = jnp.broadcast_to(meta_tokens[None].astype(F32), (nb, n_meta, d))
    xp = jnp.concatenate([meta, x_prompt, jnp.zeros((nb, pad, d), F32)], axis=1).reshape(tp, d)
    xs = x_sample.reshape(ts, d)

    zeros_conv = jnp.zeros((nb, hist, 3 * gw), F32)
    zeros_gdn = jnp.zeros((nb, gh, ghd, ghd), F32)
    sb_p = 2 if nb % 2 == 0 else 1
    sb_s = 2 if ds_b % 2 == 0 else 1
    head_eye = jnp.eye(fh, dtype=BF16).reshape(1, 1, fh, fh, 1)
    to_tile = lambda st: jnp.transpose(st.reshape(ds_b, nt, gt * sp), (1, 0, 2))
    from_tile = lambda a, n: jnp.transpose(a, (1, 0, 2)).reshape(n, sg, sp)
    outs = [[] for _ in range(14)]

    for l in range(depth):
        xp = _ffn(xp, n1, wi1, wo1, l)
        xs = _ffn(xs, n1, wi1, wo1, l)
        hp, qp, kp, vp, kvp, up, gqp, zp, smp = _inproj(xp, nm, w_proj, l, fw, sw, gw, q_scale, w_kvt, lp)
        hs, qs, ks, vs, kvs, us, gqs, zs, sms = _inproj(xs, nm, w_proj, l, fw, sw, gw, q_scale)
        bias_f = fox_b_f[l].reshape(fh, 1)

        lg_p = jnp.transpose(smp[:, :fh].reshape(nb, lp, fh), (0, 2, 1))
        lf_p, cum_p = _fcum(lg_p, bias_f)
        o_fox_p = _fox_prompt(qp, kp, vp, cum_p, nb, lp, fh, fhd, tq)
        lg_s = jnp.transpose(sms[:, :fh].reshape(ds_b, ds_t, fh), (0, 2, 1))
        lg_s = jnp.pad(lg_s, ((0, 0), (0, 0), (0, LANES - ds_t)))
        lf_s, cum_s = _fcum(lg_s, bias_f)
        qbd = (qs.reshape(ds_b, ds_t, 1, fh, fhd) * head_eye).reshape(ds_b, ds_t * fh, fw)
        rows_pad = ((0, 0), (0, LANES - ds_t), (0, 0))
        k_new = jnp.pad(ks.reshape(ds_b, ds_t, fw), rows_pad)
        v_new = jnp.pad(vs.reshape(ds_b, ds_t, fw), rows_pad)
        o_fox_s = _fox_sample(page_table, qbd, cum_s, k_new, v_new, kt_cache, vt_cache, lf_cache, l, g_pages, ds_t)

        wcat, fmat, emat, apow, alast = s5_tab[l]
        y_loc, z_loc = _s5_local(up, wcat, fmat, lp)
        x_start, x_fin = _s5_scan(z_loc, apow, nb, lreal // ck)
        z5_p = _s5_out(y_loc, x_start, emat, lp, 0)
        us_c = jnp.pad(us.reshape(ds_b, ds_t, sw), ((0, 0), (ck - ds_t, 0), (0, 0))).reshape(ds_b * ck, sw)
        x0 = jnp.concatenate([to_tile(state_s5_re[l]), to_tile(state_s5_im[l])], axis=2)
        y_loc_s, z_loc_s = _s5_local(us_c, wcat, fmat, ds_b * ck)
        z5_s = _s5_out(y_loc_s, x0, emat, ds_b * ck, ck - ds_t)
        z5_s = z5_s.reshape(ds_b, ck, sw)[:, ck - ds_t:].reshape(ts, sw)
        xf_s = _s5_step(z_loc_s, x0, alast)

        cw = gdn_conv_w[l]
        gnorm = gdn_norm[l].reshape(1, ghd)
        o_gdn_p, s_p = _gdn(gqp.reshape(nb, lp, 3 * gw), zp.reshape(nb, lp, gw), smp.reshape(nb, lp, LANES),
                            zeros_conv, zeros_gdn, cw, neg_a[l], dt_b[l], gnorm, sb_p, lreal if pad else None,
                            a_lane, b_lane)
        seq_pad = lambda a: jnp.pad(a.reshape(ds_b, ds_t, a.shape[1]), ((0, 0), (0, gc - ds_t), (0, 0)))
        conv_s = jnp.pad(state_gdn_conv[l], ((0, 0), (hist - (kconv - 1), 0), (0, 0)))
        o_gdn_s, s_s = _gdn(seq_pad(gqs), seq_pad(zs), seq_pad(sms), conv_s, state_gdn[l], cw, neg_a[l], dt_b[l],
                            gnorm, sb_s, ds_t, a_lane, b_lane)

        xp = _merge(xp, hp, o_fox_p, z5_p, o_gdn_p.reshape(tp, gw), wf16, w516, wg16, w_gate, wo16, l)
        xs = _merge(xs, hs, o_fox_s.reshape(ts, fw), z5_s, o_gdn_s[:, :ds_t].reshape(ts, gw),
                    wf16, w516, wg16, w_gate, wo16, l)
        xp = _ffn(xp, n2, wi2, wo2, l)
        xs = _ffn(xs, n2, wi2, wo2, l)

        kv_p = jnp.transpose(kvp.reshape(nb, 2, fh, fhd, lp)[..., :lreal], (0, 4, 1, 2, 3))
        kv_s = kvs.reshape(ds_b, ds_t, 2, fh, fhd)
        xf_p = x_fin.reshape(nt, nb, ssw)
        gq_s = jnp.concatenate([state_gdn_conv[l], gqs.reshape(ds_b, ds_t, 3 * gw)], axis=1)
        new = (kv_p[:, :, 0], kv_p[:, :, 1], jnp.transpose(lf_p[:, :, :lreal], (0, 2, 1)),
               kv_s[:, :, 0], kv_s[:, :, 1], jnp.transpose(lf_s[:, :, :ds_t], (0, 2, 1)),
               from_tile(xf_p[:, :, :ssw // 2], nb), from_tile(xf_p[:, :, ssw // 2:], nb),
               from_tile(xf_s[:, :, :ssw // 2], ds_b), from_tile(xf_s[:, :, ssw // 2:], ds_b),
               s_p, s_s, gqp.reshape(nb, lp, 3 * gw)[:, lreal - (kconv - 1):lreal], gq_s[:, -(kconv - 1):])
        for i, a in enumerate(new):
            outs[i].append(a)

    y_prompt = _final_norm(xp, norm_final.reshape(1, d)).reshape(nb, lp, d)[:, n_meta:lreal]
    y_sample = _final_norm(xs, norm_final.reshape(1, d)).reshape(ds_b, ds_t, d)
    return (y_prompt, y_sample) + tuple(jnp.stack(o) for o in outs)
```

```python
import functools
import math

import jax
import jax.numpy as jnp
from jax import lax
from jax.experimental import pallas as pl
from jax.experimental.pallas import tpu as pltpu

F32 = jnp.float32
BF16 = jnp.bfloat16
RMS_EPS = 1e-6
NEG_BIG = -1e30
LANES = 128
VMEM_LIMIT = 56 * 1024 * 1024
S5_CHUNK = 16
GDN_CHUNK = 128
HI = lax.Precision.HIGHEST
LOG2E = math.log2(math.e)


def _cparams(sem):
    return pltpu.CompilerParams(dimension_semantics=sem, vmem_limit_bytes=VMEM_LIMIT)


def _resident(shape, index_map):
    return pl.BlockSpec(shape, index_map, pipeline_mode=pl.Buffered(1))


def _dot(a, b):
    return jnp.dot(a, b, preferred_element_type=F32)


def _dot_hi(a, b):
    return jnp.dot(a, b, preferred_element_type=F32, precision=HI)


def _dot_nt(a, b):
    return lax.dot_general(a, b, (((1,), (1,)), ((), ())), preferred_element_type=F32)


def _split(x):
    hi = x.astype(BF16)
    return hi, (x - hi.astype(F32)).astype(BF16)


def _dot3s(a_hi, a_lo, b_hi, b_lo):
    return _dot(jnp.concatenate([a_hi, a_lo, a_hi], axis=1), jnp.concatenate([b_hi, b_hi, b_lo], axis=0))


def _dot3(a, b):
    return _dot3s(*_split(a), *_split(b))


def _rms_bf16(x, w):
    ms = jnp.mean(x * x, axis=-1, keepdims=True)
    return (x * lax.rsqrt(ms + RMS_EPS) * w).astype(BF16)


def _sigmoid(x):
    return 1.0 / (1.0 + jnp.exp(-x))


def _row_tile(t, cap=1024):
    best = 8
    for tm in range(8, min(t, cap) + 1, 8):
        if t % tm == 0:
            best = tm
    return best


def _ffn_body(x_ref, nw_ref, wi_ref, wo_ref, o_ref, act_ref, *, d_ff, fc):
    x = x_ref[...]
    h = _rms_bf16(x, nw_ref[...])
    for c in range(d_ff // fc):
        g = _dot(h, wi_ref[:, c * fc:(c + 1) * fc])
        u = _dot(h, wi_ref[:, d_ff + c * fc:d_ff + (c + 1) * fc])
        act_ref[:, c * fc:(c + 1) * fc] = (g * _sigmoid(g) * u).astype(BF16)
    o_ref[...] = x + 0.5 * _dot(act_ref[...], wo_ref[...])


def _ffn(x, nw, wi, wo, layer):
    t, d = x.shape
    tm = _row_tile(t)
    d_ff = wo.shape[1]
    fc = 256 if d_ff % 256 == 0 else d_ff
    return pl.pallas_call(
        functools.partial(_ffn_body, d_ff=d_ff, fc=fc),
        grid=(t // tm,),
        in_specs=[pl.BlockSpec((tm, d), lambda i: (i, 0)),
                  _resident((None, 1, d), lambda i: (layer, 0, 0)),
                  _resident((None, d, 2 * d_ff), lambda i: (layer, 0, 0)),
                  _resident((None, d_ff, d), lambda i: (layer, 0, 0))],
        out_specs=pl.BlockSpec((tm, d), lambda i: (i, 0)),
        out_shape=jax.ShapeDtypeStruct((t, d), F32),
        scratch_shapes=[pltpu.VMEM((tm, d_ff), BF16)],
        compiler_params=_cparams(("arbitrary",)),
        name="ffn",
    )(x, nw, wi, wo)


def _inproj_body(x_ref, nw_ref, w_ref, *rest, fw, sw, gw, q_scale, kv_t, n_alias):
    if kv_t:
        wkvt_ref = rest[0]
        h_ref, q_ref, k_ref, v_ref, u_ref, gq_ref, z_ref, sm_ref, kt_ref, vt_ref = rest[1 + n_alias:]
    else:
        h_ref, q_ref, k_ref, v_ref, kv_ref, u_ref, gq_ref, z_ref, sm_ref = rest
    h = _rms_bf16(x_ref[...], nw_ref[...])
    h_ref[...] = h
    o = 0
    q = _dot(h, w_ref[:, o:o + fw]); o += fw
    q_ref[...] = (q * q_scale).astype(BF16)
    kv = _dot(h, w_ref[:, o:o + 2 * fw]); o += 2 * fw
    if kv_t:
        kvt = _dot_nt(wkvt_ref[...], h)
        kt_ref[...] = kvt[:fw]
        vt_ref[...] = kvt[fw:]
    else:
        kv_ref[...] = kv
    k_ref[...] = kv[:, :fw].astype(BF16)
    v_ref[...] = kv[:, fw:].astype(BF16)
    u_ref[...] = _dot(h, w_ref[:, o:o + sw]); o += sw
    gq_ref[...] = _dot(h, w_ref[:, o:o + 3 * gw]); o += 3 * gw
    z_ref[...] = _dot(h, w_ref[:, o:o + gw]); o += gw
    sm_ref[...] = _dot(h, w_ref[:, o:o + LANES])


def _inproj(x, nw, w, layer, fw, sw, gw, q_scale, w_kvt=None, seq_len=None, kv_stack=None, stack_shape=None):
    t, d = x.shape
    n = w.shape[2]
    kv_t = w_kvt is not None
    tm = max(c for c in (128, 256, 384, 512) if seq_len % c == 0) if kv_t else _row_tile(t)
    row = lambda width: pl.BlockSpec((tm, width), lambda i: (i, 0))
    in_specs = [row(d), _resident((None, 1, d), lambda i: (layer, 0, 0)),
                _resident((None, d, n), lambda i: (layer, 0, 0))]
    args = [x, nw, w]
    aliases = {}
    if kv_t:
        widths = (d, fw, fw, fw, sw, 3 * gw, gw, LANES)
        dts = (BF16, BF16, BF16, BF16, F32, F32, F32, F32)
        per_seq = seq_len // tm
        in_specs.append(_resident((None, 2 * fw, d), lambda i: (layer, 0, 0)))
        args.append(w_kvt)
        stack_spec = pl.BlockSpec((None, None, fw, tm), lambda i: (layer, i // per_seq, 0, i % per_seq))
        out_specs = [row(wd) for wd in widths] + [stack_spec, stack_spec]
        out_shape = ([jax.ShapeDtypeStruct((t, wd), dt) for wd, dt in zip(widths, dts)]
                     + [jax.ShapeDtypeStruct(stack_shape, F32)] * 2)
        if kv_stack is not None:
            in_specs += [pl.BlockSpec(memory_space=pl.ANY)] * 2
            args += list(kv_stack)
            aliases = {4: len(widths), 5: len(widths) + 1}
    else:
        widths = (d, fw, fw, fw, 2 * fw, sw, 3 * gw, gw, LANES)
        dts = (BF16, BF16, BF16, BF16, F32, F32, F32, F32, F32)
        out_specs = [row(wd) for wd in widths]
        out_shape = [jax.ShapeDtypeStruct((t, wd), dt) for wd, dt in zip(widths, dts)]
    return pl.pallas_call(
        functools.partial(_inproj_body, fw=fw, sw=sw, gw=gw, q_scale=q_scale, kv_t=kv_t, n_alias=len(aliases)),
        grid=(t // tm,),
        in_specs=in_specs,
        out_specs=out_specs,
        out_shape=out_shape,
        input_output_aliases=aliases,
        compiler_params=_cparams(("arbitrary",)),
        name="inproj",
    )(*args)


def _log_sigmoid(x):
    return jnp.minimum(x, 0.0) - jnp.log(1.0 + jnp.exp(-jnp.abs(x)))


def _fcum_body(lg_ref, b_ref, lf_ref, cum_ref, *, nblk):
    r = lax.broadcasted_iota(jnp.int32, (LANES, LANES), 0)
    c = lax.broadcasted_iota(jnp.int32, (LANES, LANES), 1)
    upper = (r <= c).astype(F32)
    carry = jnp.zeros((lg_ref.shape[0], 1), F32)
    for i in range(nblk):
        sl = slice(i * LANES, (i + 1) * LANES)
        lf = _log_sigmoid(lg_ref[:, sl] + b_ref[...])
        lf_ref[:, sl] = lf
        cs = _dot_hi(lf, upper) + carry
        carry = cs[:, LANES - 1:LANES]
        cum_ref[:, sl] = cs


def _fcum(logits, bias):
    ns, nh, length = logits.shape
    blk = pl.BlockSpec((None, nh, length), lambda s: (s, 0, 0))
    return pl.pallas_call(
        functools.partial(_fcum_body, nblk=length // LANES),
        grid=(ns,),
        in_specs=[blk, pl.BlockSpec((nh, 1), lambda s: (0, 0))],
        out_specs=[blk, blk],
        out_shape=[jax.ShapeDtypeStruct(logits.shape, F32)] * 2,
        compiler_params=_cparams(("arbitrary",)),
        name="fcum",
    )(logits, bias)


def _fox_prompt_body(q_ref, k_ref, v_ref, fk_ref, o_ref, m_sc, l_sc, acc_sc, s_sc, *, tq, hd, n_heads):
    qi = pl.program_id(1)
    lane = lax.broadcasted_iota(jnp.int32, (1, LANES), 1)
    lo = lane < hd
    reps = tq // LANES
    causal = lax.broadcasted_iota(jnp.int32, (tq, tq), 1) <= lax.broadcasted_iota(jnp.int32, (tq, tq), 0)
    for p in range(n_heads * hd // LANES):
        qp = q_ref[:, p * LANES:(p + 1) * LANES]
        qm = (jnp.where(lo, qp, jnp.zeros_like(qp)), jnp.where(lo, jnp.zeros_like(qp), qp))
        m_sc[...] = jnp.full(m_sc.shape, NEG_BIG, F32)
        l_sc[...] = jnp.zeros(l_sc.shape, F32)
        acc_sc[...] = jnp.zeros(acc_sc.shape, F32)

        def scores(j, p=p, qm=qm):
            start = pl.multiple_of(j * tq, tq)
            kb = k_ref[pl.ds(start, tq), p * LANES:(p + 1) * LANES]
            fkb = fk_ref[:, pl.ds(start, tq)] * LOG2E
            return [_dot_nt(qm[hh], kb) - fkb[2 * p + hh:2 * p + hh + 1, :] for hh in range(2)]

        def consume(j, s, masked, p=p):
            vb = v_ref[pl.ds(pl.multiple_of(j * tq, tq), tq), p * LANES:(p + 1) * LANES]
            if masked:
                s = [jnp.where(causal, sh, NEG_BIG) for sh in s]
            m_old = [m_sc[hh] for hh in range(2)]
            m_new = [jnp.maximum(m_old[hh], jnp.max(s[hh], axis=-1, keepdims=True)) for hh in range(2)]
            alpha = [jnp.exp2(m_old[hh] - m_new[hh]) for hh in range(2)]
            e = [jnp.exp2((s[hh] - pltpu.repeat(m_new[hh], reps, 1)).astype(BF16)) for hh in range(2)]
            pv = [_dot(e[hh], jnp.concatenate([vb, jnp.ones_like(vb)], axis=1)) for hh in range(2)]
            for hh in range(2):
                l_sc[hh] = alpha[hh] * l_sc[hh] + pv[hh][:, LANES:]
                m_sc[hh] = m_new[hh]
            acc_sc[...] = (jnp.where(lo, alpha[0], alpha[1]) * acc_sc[...]
                           + jnp.where(lo, pv[0][:, :LANES], pv[1][:, :LANES]))

        s0 = scores(0)
        s_sc[0], s_sc[1] = s0[0], s0[1]

        def body(j, carry):
            s_cur = [s_sc[0], s_sc[1]]
            s_nxt = scores(j + 1)
            consume(j, s_cur, False)
            s_sc[0], s_sc[1] = s_nxt[0], s_nxt[1]
            return carry

        lax.fori_loop(0, qi, body, 0)
        consume(qi, [s_sc[0], s_sc[1]], True)
        inv = 1.0 / jnp.where(lo, l_sc[0], l_sc[1])
        o_ref[:, p * LANES:(p + 1) * LANES] = (acc_sc[...] * inv).astype(BF16)


def _fox_prompt(q, k, v, fcum, nb, lp, n_heads, hd, tq):
    fw = n_heads * hd
    nq = lp // tq
    return pl.pallas_call(
        functools.partial(_fox_prompt_body, tq=tq, hd=hd, n_heads=n_heads),
        grid=(nb, nq),
        in_specs=[pl.BlockSpec((tq, fw), lambda b, i: (b * nq + i, 0)),
                  pl.BlockSpec((lp, fw), lambda b, i: (b, 0)),
                  pl.BlockSpec((lp, fw), lambda b, i: (b, 0)),
                  pl.BlockSpec((None, n_heads, lp), lambda b, i: (b, 0, 0))],
        out_specs=pl.BlockSpec((tq, fw), lambda b, i: (b * nq + i, 0)),
        out_shape=jax.ShapeDtypeStruct((nb * lp, fw), BF16),
        scratch_shapes=[pltpu.VMEM((2, tq, LANES), F32), pltpu.VMEM((2, tq, LANES), F32),
                        pltpu.VMEM((tq, LANES), F32), pltpu.VMEM((2, tq, tq), F32)],
        compiler_params=_cparams(("arbitrary", "arbitrary")),
        name="fox_prompt",
    )(q, k, v, fcum)


def _fox_sample_body(pt_ref, qbd_ref, cn_ref, kn_ref, vn_ref, *rest, g_pages, n_q, n_heads, hd, page):
    kt_refs = rest[:g_pages]
    vt_refs = rest[g_pages:2 * g_pages]
    lf_ref, o_ref, m_sc, l_sc, acc_sc, car_sc = rest[2 * g_pages:]
    g = pl.program_id(1)
    n_pages = pl.num_programs(1) * g_pages
    lfs = [lf_ref[pt_ref[pl.program_id(0), n_pages - 1 - (g * g_pages + i)]] for i in range(g_pages)]
    nr = n_q * n_heads
    fw = n_heads * hd
    qbd = qbd_ref[...]
    rr = lax.broadcasted_iota(jnp.int32, (nr, 1), 0)
    qidx = rr // n_heads
    lane = lax.broadcasted_iota(jnp.int32, (1, LANES), 1)
    col_new = jnp.concatenate([cn_ref[...]] * n_q, axis=0)
    row_new = jnp.sum(jnp.where(lane == qidx, col_new, 0.0), axis=-1, keepdims=True)

    def update(slots, scores, pv_ofs):
        m_old = [m_sc[i] for i in slots]
        m_new = [jnp.maximum(mo, jnp.max(s, axis=-1, keepdims=True)) for mo, s in zip(m_old, scores)]
        alpha = [jnp.exp2(mo - mn) for mo, mn in zip(m_old, m_new)]
        e = [jnp.exp2(s - mn) for s, mn in zip(scores, m_new)]
        pv = [f(ee.astype(BF16)) for f, ee in zip(pv_ofs, e)]
        for n, i in enumerate(slots):
            l_sc[i] = alpha[n] * l_sc[i] + jnp.sum(e[n], axis=-1, keepdims=True)
            m_sc[i] = m_new[n]
            acc_sc[i] = alpha[n] * acc_sc[i] + pv[n]

    @pl.when(g == 0)
    def _():
        m_sc[...] = jnp.full(m_sc.shape, NEG_BIG, F32)
        l_sc[...] = jnp.zeros(l_sc.shape, F32)
        acc_sc[...] = jnp.zeros(acc_sc.shape, F32)
        car_sc[...] = jnp.zeros(car_sc.shape, F32)
        s = _dot_nt(qbd, kn_ref[...]) + (row_new - col_new) * LOG2E
        s = jnp.where((lane <= qidx) & (lane < n_q), s, NEG_BIG)
        update([0], [s], [lambda e: _dot(e, vn_ref[...])])

    r = lax.broadcasted_iota(jnp.int32, (page, page), 0)
    c = lax.broadcasted_iota(jnp.int32, (page, page), 1)
    later = (r > c).astype(F32)
    suffix = _dot_hi(jnp.concatenate(lfs, axis=0), later)
    carry = car_sc[...]
    scores, pv_ofs = [], []
    for i in range(g_pages):
        bias = jnp.concatenate([suffix[i * n_heads:(i + 1) * n_heads] + carry] * n_q, axis=0) + row_new
        kt = kt_refs[i][...].reshape(fw, page).astype(BF16)
        scores.append(_dot(qbd, kt) + bias * LOG2E)
        vt = vt_refs[i][...].reshape(fw, page).astype(BF16)
        pv_ofs.append(lambda e, vt=vt: _dot_nt(e, vt))
        carry = carry + jnp.sum(lfs[i], axis=-1, keepdims=True)
    car_sc[...] = carry
    update(list(range(g_pages)), scores, pv_ofs)

    @pl.when(g == pl.num_programs(1) - 1)
    def _():
        ms = [m_sc[i] for i in range(g_pages)]
        m_all = functools.reduce(jnp.maximum, ms)
        wts = [jnp.exp2(mi - m_all) for mi in ms]
        l_all = sum(w * l_sc[i] for i, w in enumerate(wts))
        o = sum(w * acc_sc[i] for i, w in enumerate(wts)) / l_all
        lane_w = lax.broadcasted_iota(jnp.int32, (1, fw), 1)
        o = jnp.where(lane_w // hd == rr % n_heads, o, 0.0)
        o_ref[...] = jnp.sum(o.reshape(n_q, n_heads, fw), axis=1).astype(BF16)


def _fox_sample(page_table, qbd, cum_new, k_new, v_new, kt, vt, lft, layer, g_pages, n_q):
    nb, n_pages = page_table.shape
    _, _, n_heads, hd, page = kt.shape
    fw = n_heads * hd
    nr = n_q * n_heads
    n_groups = n_pages // g_pages

    def page_map(i):
        return lambda b, g, pt: (layer, pt[b, n_pages - 1 - (g * g_pages + i)], 0, 0, 0)

    per_b = lambda shape: pl.BlockSpec((None,) + shape, lambda b, g, pt: (b, 0, 0))
    in_specs = ([per_b((nr, fw)), per_b((n_heads, LANES)), per_b((LANES, fw)), per_b((LANES, fw))]
                + [pl.BlockSpec((None, None, n_heads, hd, page), page_map(i)) for i in range(g_pages)]
                + [pl.BlockSpec((None, None, n_heads, hd, page), page_map(i)) for i in range(g_pages)]
                + [_resident((None,) + lft.shape[1:], lambda b, g, pt: (layer, 0, 0, 0))])
    return pl.pallas_call(
        functools.partial(_fox_sample_body, g_pages=g_pages, n_q=n_q, n_heads=n_heads, hd=hd, page=page),
        grid_spec=pltpu.PrefetchScalarGridSpec(
            num_scalar_prefetch=1, grid=(nb, n_groups), in_specs=in_specs,
            out_specs=per_b((n_q, fw)),
            scratch_shapes=[pltpu.VMEM((g_pages, nr, 1), F32), pltpu.VMEM((g_pages, nr, 1), F32),
                            pltpu.VMEM((g_pages, nr, fw), F32), pltpu.VMEM((n_heads, 1), F32)]),
        out_shape=jax.ShapeDtypeStruct((nb, n_q, fw), BF16),
        compiler_params=_cparams(("arbitrary", "arbitrary")),
        name="fox_sample",
    )(page_table, qbd, cum_new, k_new, v_new, *([kt] * g_pages), *([vt] * g_pages), lft)


def _gelu_tanh(x):
    return 0.5 * x * (1.0 + jnp.tanh(math.sqrt(2.0 / math.pi) * (x + 0.044715 * x * x * x)))


def _s5_tables(a_re, a_im, b_re, b_im, c_re, c_im, d_skip, log_dt, n_last):
    ck = S5_CHUNK
    g, p, gs = b_re.shape
    gt = LANES // gs
    nt = g // gt
    dt = jnp.exp(log_dt)[:, None]
    mag_l, ang = a_re * dt, a_im * dt

    def apow_fn(n):
        m = jnp.exp(mag_l * n)
        return m * jnp.cos(ang * n), m * jnp.sin(ang * n)

    ab_re, ab_im = apow_fn(1.0)
    den = a_re * a_re + a_im * a_im
    cf_re = ((ab_re - 1.0) * a_re + ab_im * a_im) / den
    cf_im = (ab_im * a_re - (ab_re - 1.0) * a_im) / den
    bb_re = cf_re[..., None] * b_re - cf_im[..., None] * b_im
    bb_im = cf_re[..., None] * b_im + cf_im[..., None] * b_re
    pw_re, pw_im = jax.vmap(apow_fn)(jnp.arange(ck + 1, dtype=F32))
    cab_re = c_re[None] * pw_re[:, :, None, :] - c_im[None] * pw_im[:, :, None, :]
    cab_im = c_re[None] * pw_im[:, :, None, :] + c_im[None] * pw_re[:, :, None, :]
    w = (jnp.einsum('lgcp,gpd->lgcd', cab_re[:ck], bb_re, precision=HI)
         - jnp.einsum('lgcp,gpd->lgcd', cab_im[:ck], bb_im, precision=HI))
    w = w.at[0].add(jax.vmap(jnp.diag)(d_skip))
    wc = jnp.transpose(w, (1, 3, 0, 2)).reshape(nt, gt * gs, ck * gs)
    e_pow = ck - 1 - jnp.arange(ck)
    f_re = pw_re[e_pow][..., None] * bb_re[None] - pw_im[e_pow][..., None] * bb_im[None]
    f_im = pw_re[e_pow][..., None] * bb_im[None] + pw_im[e_pow][..., None] * bb_re[None]
    f_rows = lambda f: jnp.transpose(f, (1, 0, 3, 2)).reshape(nt, gt, ck, gs, p)
    fc = jnp.transpose(jnp.concatenate([f_rows(f_re), f_rows(f_im)], axis=-1), (0, 2, 1, 3, 4)
                       ).reshape(nt, ck, gt * gs, 2 * p)
    e_rows = lambda m: jnp.transpose(m, (1, 0, 3, 2)).reshape(nt, gt, ck, p, gs)
    ec = jnp.transpose(jnp.stack([e_rows(cab_re[1:]), e_rows(-cab_im[1:])], axis=1), (0, 3, 1, 2, 4, 5)
                       ).reshape(nt, ck, 2 * gt * p, gs)

    tile_vec = lambda v: v.reshape(nt, gt * p)
    nlev = 16
    lev = [apow_fn(float(ck * 2 ** i)) for i in range(nlev)]
    apow = jnp.stack([jnp.stack([tile_vec(r), tile_vec(i)], axis=1) for r, i in lev], axis=1)
    al_re, al_im = apow_fn(float(n_last))
    alast = jnp.stack([tile_vec(al_re), tile_vec(al_im)], axis=1)
    wcat, fmat, emat = _s5_expand(wc.astype(BF16), fc.astype(BF16), ec.astype(BF16), gs, p)
    return wcat, fmat, emat, apow, alast


def _s5_expand_body(wc_ref, fc_ref, ec_ref, w_ref, f_ref, e_ref, *, ck, gs, p):
    gt = LANES // gs
    iota = lambda shape, axis: lax.broadcasted_iota(jnp.int32, shape, axis)
    bf = lambda m: m.astype(F32).astype(BF16)
    r, q = iota((ck * gs, ck * LANES), 0), iota((ck * gs, ck * LANES), 1)
    spread_w = bf((r // gs == q // LANES) & (r % gs == q % gs))
    r, q = iota((LANES, ck * LANES), 0), iota((LANES, ck * LANES), 1)
    w_ref[...] = jnp.where(r // gs == (q % LANES) // gs, _dot(wc_ref[...], spread_w), 0.0).astype(BF16)
    half = gt * p
    r, q = iota((2 * p, 2 * half), 0), iota((2 * p, 2 * half), 1)
    spread_f = bf((r // p == q // half) & (r % p == q % p))
    r, q = iota((LANES, 2 * half), 0), iota((LANES, 2 * half), 1)
    keep_f = r // gs == (q % half) // p
    r, q = iota((gs, LANES), 0), iota((gs, LANES), 1)
    spread_e = bf(r == q % gs)
    r, q = iota((2 * half, LANES), 0), iota((2 * half, LANES), 1)
    keep_e = (r % half) // p == q // gs
    for s in range(ck):
        f_ref[s // 2, (s % 2) * LANES:(s % 2 + 1) * LANES, :] = jnp.where(
            keep_f, _dot(fc_ref[s], spread_f), 0.0).astype(BF16)
        e_ref[s // 2, :, (s % 2) * LANES:(s % 2 + 1) * LANES] = jnp.where(
            keep_e, _dot(ec_ref[s], spread_e), 0.0).astype(BF16)


def _s5_expand(wc, fc, ec, gs, p):
    nt = wc.shape[0]
    ck = S5_CHUNK
    sw = 2 * (LANES // gs) * p
    blk = lambda a: pl.BlockSpec((None,) + a.shape[1:], lambda j: (j,) + (0,) * (a.ndim - 1))
    shapes = [jax.ShapeDtypeStruct((nt, LANES, ck * LANES), BF16),
              jax.ShapeDtypeStruct((nt, ck // 2, 2 * LANES, sw), BF16),
              jax.ShapeDtypeStruct((nt, ck // 2, sw, 2 * LANES), BF16)]
    return pl.pallas_call(
        functools.partial(_s5_expand_body, ck=ck, gs=gs, p=p),
        grid=(nt,),
        in_specs=[blk(wc), blk(fc), blk(ec)],
        out_specs=[blk(s) for s in shapes],
        out_shape=shapes,
        compiler_params=_cparams(("arbitrary",)),
        name="s5_expand",
    )(wc, fc, ec)


def _s5_local_body(u_ref, w_ref, f_ref, y_ref, z_ref, acc_ref, *, ck, n):
    accz = None
    zero = jnp.zeros((LANES, LANES), BF16)
    for s in range(0, ck, 2):
        us = jnp.concatenate([u_ref[pl.ds(s, n, stride=ck), :].astype(BF16),
                              u_ref[pl.ds(s + 1, n, stride=ck), :].astype(BF16)], axis=1)
        wide = (ck - s) * LANES
        w_two = jnp.concatenate([w_ref[:, :wide],
                                 jnp.concatenate([zero, w_ref[:, :wide - LANES]], axis=1)], axis=0)
        y = _dot(us, w_two)
        if s == 0:
            acc_ref[...] = y
        else:
            acc_ref[:, s * LANES:] += y
        z = _dot(us, f_ref[s // 2])
        accz = z if accz is None else accz + z
    for t in range(ck):
        y_ref[pl.ds(t, n, stride=ck), :] = acc_ref[:, t * LANES:(t + 1) * LANES]
    z_ref[...] = accz


def _s5_local(u, wcat, fmat, rows_per_step):
    r, width = u.shape
    ck = S5_CHUNK
    nt = width // LANES
    sw = fmat.shape[3]
    n = rows_per_step // ck
    return pl.pallas_call(
        functools.partial(_s5_local_body, ck=ck, n=n),
        grid=(nt, r // rows_per_step),
        in_specs=[pl.BlockSpec((rows_per_step, LANES), lambda j, i: (i, j)),
                  pl.BlockSpec((None, LANES, ck * LANES), lambda j, i: (j, 0, 0)),
                  pl.BlockSpec((None, ck // 2, 2 * LANES, sw), lambda j, i: (j, 0, 0, 0))],
        out_specs=[pl.BlockSpec((rows_per_step, LANES), lambda j, i: (i, j)),
                   pl.BlockSpec((None, n, sw), lambda j, i: (j, i, 0))],
        out_shape=[jax.ShapeDtypeStruct((r, width), F32), jax.ShapeDtypeStruct((nt, r // ck, sw), F32)],
        scratch_shapes=[pltpu.VMEM((n, ck * LANES), F32)],
        compiler_params=_cparams(("arbitrary", "arbitrary")),
        name="s5_local",
    )(u, wcat, fmat)


def _s5_scan_body(z_ref, ap_ref, xs_ref, xf_ref, *, nc, half, last):
    xr = z_ref[:, :half]
    xi = z_ref[:, half:]
    row = lax.broadcasted_iota(jnp.int32, (nc, 1), 0)
    lev, s = 0, 1
    while s < nc:
        ar = ap_ref[lev, 0:1, :]
        ai = ap_ref[lev, 1:2, :]
        keep = row >= s
        sr = jnp.where(keep, pltpu.roll(xr, s, 0), 0.0)
        si = jnp.where(keep, pltpu.roll(xi, s, 0), 0.0)
        xr, xi = xr + ar * sr - ai * si, xi + ar * si + ai * sr
        lev, s = lev + 1, s * 2
    xf_ref[:, :half] = xr[last:last + 1, :]
    xf_ref[:, half:] = xi[last:last + 1, :]
    xs_ref[:, :half] = jnp.where(row >= 1, pltpu.roll(xr, 1, 0), 0.0)
    xs_ref[:, half:] = jnp.where(row >= 1, pltpu.roll(xi, 1, 0), 0.0)


def _s5_scan(z, apow, nb, n_real):
    nt, n, sw = z.shape
    nc = n // nb
    nlev = apow.shape[1]
    return pl.pallas_call(
        functools.partial(_s5_scan_body, nc=nc, half=sw // 2, last=n_real - 1),
        grid=(nt, nb),
        in_specs=[pl.BlockSpec((None, nc, sw), lambda j, b: (j, b, 0)),
                  pl.BlockSpec((None, nlev, 2, sw // 2), lambda j, b: (j, 0, 0, 0))],
        out_specs=[pl.BlockSpec((None, nc, sw), lambda j, b: (j, b, 0)),
                   pl.BlockSpec((None, None, 1, sw), lambda j, b: (j, b, 0, 0))],
        out_shape=[jax.ShapeDtypeStruct((nt, n, sw), F32), jax.ShapeDtypeStruct((nt, nb, 1, sw), F32)],
        compiler_params=_cparams(("arbitrary", "arbitrary")),
        name="s5_scan",
    )(z, apow)


def _s5_out_body(y_ref, xs_ref, e_ref, o_ref, act_ref, *, ck, n, first):
    xs = xs_ref[...].astype(BF16)
    for t in range(0, ck, 2):
        pair = [y_ref[pl.ds(t + i, n, stride=ck), :] for i in range(2)]
        if t >= first:
            both = _dot(xs, e_ref[(t - first) // 2])
            pair = [pair[i] + both[:, i * LANES:(i + 1) * LANES] for i in range(2)]
        for i in range(2):
            act_ref[pl.ds(t + i, n, stride=ck), :] = _gelu_tanh(pair[i])
    o_ref[...] = act_ref[...].astype(BF16)


def _s5_out(ylocal, xstart, emat, rows_per_step, first):
    r, width = ylocal.shape
    ck = S5_CHUNK
    nt = width // LANES
    sw = xstart.shape[2]
    n = rows_per_step // ck
    return pl.pallas_call(
        functools.partial(_s5_out_body, ck=ck, n=n, first=first),
        grid=(nt, r // rows_per_step),
        in_specs=[pl.BlockSpec((rows_per_step, LANES), lambda j, i: (i, j)),
                  pl.BlockSpec((None, n, sw), lambda j, i: (j, i, 0)),
                  pl.BlockSpec((None, ck // 2, sw, 2 * LANES), lambda j, i: (j, 0, 0, 0))],
        out_specs=pl.BlockSpec((rows_per_step, LANES), lambda j, i: (i, j)),
        out_shape=jax.ShapeDtypeStruct((r, width), BF16),
        scratch_shapes=[pltpu.VMEM((rows_per_step, LANES), F32)],
        compiler_params=_cparams(("arbitrary", "arbitrary")),
        name="s5_out",
    )(ylocal, xstart, emat)


def _s5_step_body(z_ref, x0_ref, al_ref, xf_ref, *, half):
    ar, ai = al_ref[0:1, :], al_ref[1:2, :]
    xr, xi = x0_ref[:, :half], x0_ref[:, half:]
    xf_ref[:, :half] = z_ref[:, :half] + ar * xr - ai * xi
    xf_ref[:, half:] = z_ref[:, half:] + ar * xi + ai * xr


def _s5_step(z, x0, alast):
    nt, n, sw = z.shape
    blk = pl.BlockSpec((None, n, sw), lambda j: (j, 0, 0))
    return pl.pallas_call(
        functools.partial(_s5_step_body, half=sw // 2),
        grid=(nt,),
        in_specs=[blk, blk, pl.BlockSpec((None, 2, sw // 2), lambda j: (j, 0, 0))],
        out_specs=blk,
        out_shape=jax.ShapeDtypeStruct(z.shape, F32),
        compiler_params=_cparams(("arbitrary",)),
        name="s5_step",
    )(z, x0, alast)


def _gdn_body(x_ref, z_ref, sm_ref, ci_ref, s0_ref, cw_ref, na_ref, db_ref, nw_ref, o_ref, sf_ref, s_sc, car_sc,
              *, c, sb, n_valid, n_heads, hd, kconv, a_lane, b_lane):
    ci = pl.program_id(1)
    gw = n_heads * hd

    hist = ci_ref.shape[1]

    @pl.when(ci == 0)
    def _():
        s_sc[...] = s0_ref[...]
        car_sc[:, :hist, :] = ci_ref[...]

    r = lax.broadcasted_iota(jnp.int32, (c, c), 0)
    cc = lax.broadcasted_iota(jnp.int32, (c, c), 1)
    incl = r >= cc
    strict = r > cc
    eye = (r == cc).astype(F32)
    row = ci * c + lax.broadcasted_iota(jnp.int32, (c, 1), 0)
    chains = []
    for b in range(sb):
        car_sc[b, hist:, :] = x_ref[b]
        y = None
        for j in range(kconv):
            term = car_sc[b, pl.ds(hist - (kconv - 1) + j, c), :] * cw_ref[j:j + 1, :]
            y = term if y is None else y + term
        car_sc[b, :hist, :] = car_sc[b, c:, :]
        qkv = y * _sigmoid(y)

        sm = sm_ref[b]
        sp_in = sm + db_ref[...]
        softplus = jnp.maximum(sp_in, 0.0) + jnp.log(1.0 + jnp.exp(-jnp.abs(sp_in)))
        gfull = na_ref[...] * softplus
        beta_full = _sigmoid(sm)
        if n_valid is not None:
            live = row < n_valid
            gfull = jnp.where(live, gfull, 0.0)
            beta_full = jnp.where(live, beta_full, 0.0)
            qkv = jnp.where(live, qkv, 0.0)
        cum_full = _dot_hi(incl.astype(F32), gfull)
        cum_t = cum_full.T
        for h in range(n_heads):
            q = qkv[:, h * hd:(h + 1) * hd]
            k = qkv[:, gw + h * hd:gw + (h + 1) * hd]
            v = qkv[:, 2 * gw + h * hd:2 * gw + (h + 1) * hd]
            q = q * lax.rsqrt(jnp.sum(q * q, axis=-1, keepdims=True) + 1e-6) * (hd ** -0.5)
            k = k * lax.rsqrt(jnp.sum(k * k, axis=-1, keepdims=True) + 1e-6)
            cum = cum_full[:, a_lane + h:a_lane + h + 1]
            cum_r = cum_t[a_lane + h:a_lane + h + 1, :]
            beta = beta_full[:, b_lane + h:b_lane + h + 1]
            decay = jnp.where(incl, jnp.exp(jnp.where(incl, cum - cum_r, 0.0)), 0.0)
            kb = k * beta
            k16 = k.astype(BF16)
            ecum = jnp.exp(cum)
            g_last = cum[c - 1:c, :]
            chains.append(dict(
                b=b, h=h, k16=k16, decay=decay, qe16=(q * ecum).astype(BF16), q16=q.astype(BF16),
                rhs=jnp.concatenate([v * beta, kb * ecum], axis=-1), kb16=kb.astype(BF16),
                kd=k * jnp.exp(g_last - cum), s_scale=jnp.exp(g_last)))
    for ch in chains:
        ch['lower'] = jnp.where(strict, _dot_nt(ch['kb16'], ch['k16']) * ch['decay'], 0.0)
        ch['aqk16'] = jnp.where(incl, _dot_nt(ch['q16'], ch['k16']) * ch['decay'], 0.0).astype(BF16)
    for ch in chains:
        ch['t_inv'] = eye - ch['lower']
        l_hi, l_lo = _split(ch['lower'])
        ch['pw'] = _dot3s(l_hi, l_lo, l_hi, l_lo)
    span = 2
    while span < c:
        span *= 2
        for ch in chains:
            p_hi, p_lo = _split(ch['pw'])
            t_hi, t_lo = _split(ch['t_inv'])
            if span < c:
                both = _dot3s(p_hi, p_lo, jnp.concatenate([t_hi, p_hi], axis=1),
                              jnp.concatenate([t_lo, p_lo], axis=1))
                ch['t_inv'] = ch['t_inv'] + both[:, :c]
                ch['pw'] = both[:, c:]
            else:
                ch['t_inv'] = ch['t_inv'] + _dot3s(p_hi, p_lo, t_hi, t_lo)
    for ch in chains:
        ch['sol'] = _dot3(ch['t_inv'], ch['rhs'])
    for ch in chains:
        s_h = s_sc[ch['b'], ch['h']]
        s16 = s_h.astype(BF16)
        v_new = ch['sol'][:, :hd] - _dot(ch['sol'][:, hd:].astype(BF16), s16)
        ch['vn16'] = v_new.astype(BF16)
        ch['o'] = _dot(ch['qe16'], s16)
        ch['s_dec'] = s_h * ch['s_scale']
    for ch in chains:
        b, h = ch['b'], ch['h']
        o = ch['o'] + _dot(ch['aqk16'], ch['vn16'])
        s_sc[b, h] = ch['s_dec'] + _dot(ch['kd'].T.astype(BF16), ch['vn16'])
        o = o * lax.rsqrt(jnp.mean(o * o, axis=-1, keepdims=True) + RMS_EPS) * nw_ref[...]
        zz = z_ref[b, :, h * hd:(h + 1) * hd]
        o_ref[b, :, h * hd:(h + 1) * hd] = (o * (zz * _sigmoid(zz))).astype(BF16)

    @pl.when(ci == pl.num_programs(1) - 1)
    def _():
        sf_ref[...] = s_sc[...]


def _gdn(x, z, sm, conv_init, s0, conv_w, neg_a, dt_b, norm_w, sb, n_valid, a_lane, b_lane):
    ns, length, w3 = x.shape
    n_heads, hd = s0.shape[1], s0.shape[2]
    gw = n_heads * hd
    c = GDN_CHUNK
    kconv = conv_w.shape[0]
    hist = conv_init.shape[1]
    rowblk = lambda width: pl.BlockSpec((sb, c, width), lambda s, i: (s, i, 0))
    const = lambda shape: pl.BlockSpec(shape, lambda s, i: (0,) * len(shape))
    return pl.pallas_call(
        functools.partial(_gdn_body, c=c, sb=sb, n_valid=n_valid, n_heads=n_heads, hd=hd, kconv=kconv,
                          a_lane=a_lane, b_lane=b_lane),
        grid=(ns // sb, length // c),
        in_specs=[rowblk(w3), rowblk(gw), rowblk(LANES),
                  pl.BlockSpec((sb, hist, w3), lambda s, i: (s, 0, 0)),
                  pl.BlockSpec((sb, n_heads, hd, hd), lambda s, i: (s, 0, 0, 0)),
                  const((kconv, w3)), const((1, LANES)), const((1, LANES)), const((1, hd))],
        out_specs=[rowblk(gw), pl.BlockSpec((sb, n_heads, hd, hd), lambda s, i: (s, 0, 0, 0))],
        out_shape=[jax.ShapeDtypeStruct((ns, length, gw), BF16), jax.ShapeDtypeStruct(s0.shape, F32)],
        scratch_shapes=[pltpu.VMEM((sb, n_heads, hd, hd), F32), pltpu.VMEM((sb, hist + c, w3), F32)],
        compiler_params=_cparams(("arbitrary", "arbitrary")),
        name="gdn",
    )(x, z, sm, conv_init, s0, conv_w, neg_a, dt_b, norm_w)


def _merge_body(x_ref, h_ref, of_ref, z5_ref, og_ref, wf_ref, w5_ref, wg_ref, wgate_ref, wo_ref, o_ref, *, d):
    fox_br = _dot(of_ref[...], wf_ref[...])
    glu = _dot(z5_ref[...], w5_ref[...])
    s5_br = glu[:, :d] * _sigmoid(glu[:, d:])
    gdn_br = _dot(og_ref[...], wg_ref[...])
    h = h_ref[...]
    merged = (_sigmoid(_dot(h, wgate_ref[:, :d])) * fox_br
              + _sigmoid(_dot(h, wgate_ref[:, d:2 * d])) * s5_br
              + _sigmoid(_dot(h, wgate_ref[:, 2 * d:])) * gdn_br)
    o_ref[...] = x_ref[...] + _dot(merged.astype(BF16), wo_ref[...])


def _merge(x, h, o_fox, z5, o_gdn, wf, w5, wg, wgate, wo, layer):
    t, d = x.shape
    tm = _row_tile(t)
    row = lambda a: pl.BlockSpec((tm, a.shape[1]), lambda i: (i, 0))
    res = lambda a: _resident((None,) + a.shape[1:], lambda i: (layer, 0, 0))
    return pl.pallas_call(
        functools.partial(_merge_body, d=d),
        grid=(t // tm,),
        in_specs=[row(x), row(h), row(o_fox), row(z5), row(o_gdn), res(wf), res(w5), res(wg), res(wgate), res(wo)],
        out_specs=pl.BlockSpec((tm, d), lambda i: (i, 0)),
        out_shape=jax.ShapeDtypeStruct((t, d), F32),
        compiler_params=_cparams(("arbitrary",)),
        name="merge",
    )(x, h, o_fox, z5, o_gdn, wf, w5, wg, wgate, wo)


def _final_norm_body(x_ref, w_ref, o_ref):
    x = x_ref[...]
    ms = jnp.mean(x * x, axis=-1, keepdims=True)
    o_ref[...] = x * lax.rsqrt(ms + RMS_EPS) * w_ref[...]


def _final_norm(x, w):
    t, d = x.shape
    tm = _row_tile(t)
    return pl.pallas_call(
        _final_norm_body,
        grid=(t // tm,),
        in_specs=[pl.BlockSpec((tm, d), lambda i: (i, 0)), pl.BlockSpec((1, d), lambda i: (0, 0))],
        out_specs=pl.BlockSpec((tm, d), lambda i: (i, 0)),
        out_shape=jax.ShapeDtypeStruct((t, d), F32),
        compiler_params=_cparams(("arbitrary",)),
        name="final_norm",
    )(x, w)


def kernel(x_prompt, x_sample, cache_fox_k, cache_fox_v, cache_fox_logf, state_s5_re, state_s5_im, state_gdn, state_gdn_conv, page_table, meta_tokens, norm_ffn1, w_ffn1_in, w_ffn1_out, norm_mix, w_in, fox_b_f, w_fox_br, s5_A_re, s5_A_im, s5_B_re, s5_B_im, s5_C_re, s5_C_im, s5_D, s5_log_dt, w_s5_glu, gdn_conv_w, gdn_A_log, gdn_dt_bias, gdn_norm, w_gdn_br, w_out, norm_ffn2, w_ffn2_in, w_ffn2_out, norm_final):
    nb, seq, d = x_prompt.shape
    ds_b, ds_t, _ = x_sample.shape
    depth = w_in.shape[0]
    n_meta = meta_tokens.shape[0]
    fh, fhd = cache_fox_k.shape[3], cache_fox_k.shape[4]
    fw = fh * fhd
    page = cache_fox_k.shape[2]
    sg, sp, sgs = s5_B_re.shape[1:]
    sw = sg * sgs
    gh, ghd = state_gdn.shape[2], state_gdn.shape[3]
    gw = gh * ghd
    kconv = gdn_conv_w.shape[1]
    ck = S5_CHUNK
    gc = GDN_CHUNK
    assert fw % LANES == 0 and 2 * fhd == LANES and ghd == LANES and sw % LANES == 0 and LANES % sgs == 0
    assert ds_t <= ck and (ck - ds_t) % 2 == 0 and ds_t >= kconv - 1 and ds_t <= gc
    assert (n_meta + seq) % ck == 0 and n_meta + seq >= kconv - 1
    assert page == LANES and fh + 2 * gh <= LANES

    lreal = n_meta + seq
    pad = (-lreal) % LANES
    lp = lreal + pad
    tp = nb * lp
    ts = ds_b * ds_t
    tq = max(c for c in (128, 256, 384, 512) if lp % c == 0)
    n_pages = page_table.shape[1]
    g_pages = max(c for c in (1, 2, 4, 8) if n_pages % c == 0)
    hist = 8
    nt = sw // LANES
    gt = LANES // sgs
    ssw = 2 * gt * sp

    sizes = (fw, fw, fw, fh, sw, 3 * gw, gh, gh, gw, 3 * d)
    offs = [0]
    for s in sizes:
        offs.append(offs[-1] + s)
    col = lambda i: w_in[:, :, offs[i]:offs[i + 1]]
    small = jnp.concatenate([col(3), col(6), col(7),
                             jnp.zeros((depth, d, LANES - fh - 2 * gh), F32)], axis=2)
    a_lane, b_lane = fh, fh + gh
    w_proj = jnp.concatenate([col(0), col(1), col(2), col(4), col(5), col(8), small], axis=2).astype(BF16)
    w_gate = col(9).astype(BF16)
    w_kvt = jnp.transpose(w_in[:, :, offs[1]:offs[3]], (0, 2, 1)).astype(BF16)
    q_scale = fhd ** -0.5 * LOG2E
    wi1, wo1 = w_ffn1_in.astype(BF16), w_ffn1_out.astype(BF16)
    wi2, wo2 = w_ffn2_in.astype(BF16), w_ffn2_out.astype(BF16)
    wf16, w516, wg16, wo16 = (w.astype(BF16) for w in (w_fox_br, w_s5_glu, w_gdn_br, w_out))
    nrm = lambda w: w.reshape(depth, 1, d)
    n1, nm, n2 = nrm(norm_ffn1), nrm(norm_mix), nrm(norm_ffn2)
    lane_vec = lambda v, at: jnp.zeros((depth, 1, LANES), F32).at[:, 0, at:at + v.shape[1]].set(v)
    neg_a = lane_vec(-jnp.exp(gdn_A_log), a_lane)
    dt_b = lane_vec(gdn_dt_bias, a_lane)
    s5_tab = [_s5_tables(s5_A_re[l], s5_A_im[l], s5_B_re[l], s5_B_im[l], s5_C_re[l], s5_C_im[l], s5_D[l],
                         s5_log_dt[l], ds_t) for l in range(depth)]

    kt_cache = jnp.transpose(cache_fox_k, (0, 1, 3, 4, 2))
    vt_cache = jnp.transpose(cache_fox_v, (0, 1, 3, 4, 2))
    lf_cache = jnp.transpose(cache_fox_logf, (0, 1, 3, 2))

    meta = jnp.broadcast_to(meta_tokens[None].astype(F32), (nb, n_meta, d))
    xp = jnp.concatenate([meta, x_prompt, jnp.zeros((nb, pad, d), F32)], axis=1).reshape(tp, d)
    xs = x_sample.reshape(ts, d)

    zeros_conv = jnp.zeros((nb, hist, 3 * gw), F32)
    zeros_gdn = jnp.zeros((nb, gh, ghd, ghd), F32)
    sb_p = 2 if nb % 2 == 0 else 1
    sb_s = 2 if ds_b % 2 == 0 else 1
    head_eye = jnp.eye(fh, dtype=BF16).reshape(1, 1, fh, fh, 1)
    to_tile = lambda st: jnp.transpose(st.reshape(ds_b, nt, gt * sp), (1, 0, 2))
    from_tile = lambda a, n: jnp.transpose(a, (1, 0, 2)).reshape(n, sg, sp)
    outs = [[] for _ in range(12)]
    kv_stack = None

    for l in range(depth):
        xp = _ffn(xp, n1, wi1, wo1, l)
        xs = _ffn(xs, n1, wi1, wo1, l)
        hp, qp, kp, vp, up, gqp, zp, smp, k_stack, v_stack = _inproj(
            xp, nm, w_proj, l, fw, sw, gw, q_scale, w_kvt, lp, kv_stack, (depth, nb, fw, lp))
        kv_stack = (k_stack, v_stack)
        hs, qs, ks, vs, kvs, us, gqs, zs, sms = _inproj(xs, nm, w_proj, l, fw, sw, gw, q_scale)
        bias_f = fox_b_f[l].reshape(fh, 1)

        lg_p = jnp.transpose(smp[:, :fh].reshape(nb, lp, fh), (0, 2, 1))
        lf_p, cum_p = _fcum(lg_p, bias_f)
        o_fox_p = _fox_prompt(qp, kp, vp, cum_p, nb, lp, fh, fhd, tq)
        lg_s = jnp.transpose(sms[:, :fh].reshape(ds_b, ds_t, fh), (0, 2, 1))
        lg_s = jnp.pad(lg_s, ((0, 0), (0, 0), (0, LANES - ds_t)))
        lf_s, cum_s = _fcum(lg_s, bias_f)
        qbd = (qs.reshape(ds_b, ds_t, 1, fh, fhd) * head_eye).reshape(ds_b, ds_t * fh, fw)
        rows_pad = ((0, 0), (0, LANES - ds_t), (0, 0))
        k_new = jnp.pad(ks.reshape(ds_b, ds_t, fw), rows_pad)
        v_new = jnp.pad(vs.reshape(ds_b, ds_t, fw), rows_pad)
        o_fox_s = _fox_sample(page_table, qbd, cum_s, k_new, v_new, kt_cache, vt_cache, lf_cache, l, g_pages, ds_t)

        wcat, fmat, emat, apow, alast = s5_tab[l]
        y_loc, z_loc = _s5_local(up, wcat, fmat, lp)
        x_start, x_fin = _s5_scan(z_loc, apow, nb, lreal // ck)
        z5_p = _s5_out(y_loc, x_start, emat, lp, 0)
        us_c = jnp.pad(us.reshape(ds_b, ds_t, sw), ((0, 0), (ck - ds_t, 0), (0, 0))).reshape(ds_b * ck, sw)
        x0 = jnp.concatenate([to_tile(state_s5_re[l]), to_tile(state_s5_im[l])], axis=2)
        y_loc_s, z_loc_s = _s5_local(us_c, wcat, fmat, ds_b * ck)
        z5_s = _s5_out(y_loc_s, x0, emat, ds_b * ck, ck - ds_t)
        z5_s = z5_s.reshape(ds_b, ck, sw)[:, ck - ds_t:].reshape(ts, sw)
        xf_s = _s5_step(z_loc_s, x0, alast)

        cw = gdn_conv_w[l]
        gnorm = gdn_norm[l].reshape(1, ghd)
        o_gdn_p, s_p = _gdn(gqp.reshape(nb, lp, 3 * gw), zp.reshape(nb, lp, gw), smp.reshape(nb, lp, LANES),
                            zeros_conv, zeros_gdn, cw, neg_a[l], dt_b[l], gnorm, sb_p, lreal if pad else None,
                            a_lane, b_lane)
        seq_pad = lambda a: jnp.pad(a.reshape(ds_b, ds_t, a.shape[1]), ((0, 0), (0, gc - ds_t), (0, 0)))
        conv_s = jnp.pad(state_gdn_conv[l], ((0, 0), (hist - (kconv - 1), 0), (0, 0)))
        o_gdn_s, s_s = _gdn(seq_pad(gqs), seq_pad(zs), seq_pad(sms), conv_s, state_gdn[l], cw, neg_a[l], dt_b[l],
                            gnorm, sb_s, ds_t, a_lane, b_lane)

        xp = _merge(xp, hp, o_fox_p, z5_p, o_gdn_p.reshape(tp, gw), wf16, w516, wg16, w_gate, wo16, l)
        xs = _merge(xs, hs, o_fox_s.reshape(ts, fw), z5_s, o_gdn_s[:, :ds_t].reshape(ts, gw),
                    wf16, w516, wg16, w_gate, wo16, l)
        xp = _ffn(xp, n2, wi2, wo2, l)
        xs = _ffn(xs, n2, wi2, wo2, l)

        kv_s = kvs.reshape(ds_b, ds_t, 2, fh, fhd)
        xf_p = x_fin.reshape(nt, nb, ssw)
        gq_s = jnp.concatenate([state_gdn_conv[l], gqs.reshape(ds_b, ds_t, 3 * gw)], axis=1)
        new = (jnp.transpose(lf_p[:, :, :lreal], (0, 2, 1)),
               kv_s[:, :, 0], kv_s[:, :, 1], jnp.transpose(lf_s[:, :, :ds_t], (0, 2, 1)),
               from_tile(xf_p[:, :, :ssw // 2], nb), from_tile(xf_p[:, :, ssw // 2:], nb),
               from_tile(xf_s[:, :, :ssw // 2], ds_b), from_tile(xf_s[:, :, ssw // 2:], ds_b),
               s_p, s_s, gqp.reshape(nb, lp, 3 * gw)[:, lreal - (kconv - 1):lreal], gq_s[:, -(kconv - 1):])
        for i, a in enumerate(new):
            outs[i].append(a)

    y_prompt = _final_norm(xp, norm_final.reshape(1, d)).reshape(nb, lp, d)[:, n_meta:lreal]
    y_sample = _final_norm(xs, norm_final.reshape(1, d)).reshape(ds_b, ds_t, d)
    k_prompt, v_prompt = (jnp.transpose(a.reshape(depth, nb, fh, fhd, lp)[..., :lreal], (0, 1, 4, 2, 3))
                          for a in kv_stack)
    return (y_prompt, y_sample, k_prompt, v_prompt) + tuple(jnp.stack(o) for o in outs)
```

```python
import functools
import math

import jax
import jax.numpy as jnp
from jax import lax
from jax.experimental import pallas as pl
from jax.experimental.pallas import tpu as pltpu

F32 = jnp.float32
BF16 = jnp.bfloat16
RMS_EPS = 1e-6
NEG_BIG = -1e30
LANES = 128
VMEM_LIMIT = 56 * 1024 * 1024
S5_CHUNK = 16
GDN_CHUNK = 128
HI = lax.Precision.HIGHEST
LOG2E = math.log2(math.e)


def _cparams(sem):
    return pltpu.CompilerParams(dimension_semantics=sem, vmem_limit_bytes=VMEM_LIMIT)


def _resident(shape, index_map):
    return pl.BlockSpec(shape, index_map, pipeline_mode=pl.Buffered(1))


def _dot(a, b):
    return jnp.dot(a, b, preferred_element_type=F32)


def _dot_hi(a, b):
    return jnp.dot(a, b, preferred_element_type=F32, precision=HI)


def _dot_nt(a, b):
    return lax.dot_general(a, b, (((1,), (1,)), ((), ())), preferred_element_type=F32)


def _split(x):
    hi = x.astype(BF16)
    return hi, (x - hi.astype(F32)).astype(BF16)


def _dot3s(a_hi, a_lo, b_hi, b_lo):
    return _dot(jnp.concatenate([a_hi, a_lo, a_hi], axis=1), jnp.concatenate([b_hi, b_hi, b_lo], axis=0))


def _dot3(a, b):
    return _dot3s(*_split(a), *_split(b))


def _rms_bf16(x, w):
    ms = jnp.mean(x * x, axis=-1, keepdims=True)
    return (x * lax.rsqrt(ms + RMS_EPS) * w).astype(BF16)


def _sigmoid(x):
    return 1.0 / (1.0 + jnp.exp(-x))


def _row_tile(t, cap=1024):
    best = 8
    for tm in range(8, min(t, cap) + 1, 8):
        if t % tm == 0:
            best = tm
    return best


def _ffn_body(x_ref, nw_ref, wi_ref, wo_ref, o_ref, act_ref, *, d_ff, fc):
    x = x_ref[...]
    h = _rms_bf16(x, nw_ref[...])
    for c in range(d_ff // fc):
        g = _dot(h, wi_ref[:, c * fc:(c + 1) * fc])
        u = _dot(h, wi_ref[:, d_ff + c * fc:d_ff + (c + 1) * fc])
        act_ref[:, c * fc:(c + 1) * fc] = (g * _sigmoid(g) * u).astype(BF16)
    o_ref[...] = x + 0.5 * _dot(act_ref[...], wo_ref[...])


def _ffn(x, nw, wi, wo, layer):
    t, d = x.shape
    tm = _row_tile(t)
    d_ff = wo.shape[1]
    fc = 256 if d_ff % 256 == 0 else d_ff
    return pl.pallas_call(
        functools.partial(_ffn_body, d_ff=d_ff, fc=fc),
        grid=(t // tm,),
        in_specs=[pl.BlockSpec((tm, d), lambda i: (i, 0)),
                  _resident((None, 1, d), lambda i: (layer, 0, 0)),
                  _resident((None, d, 2 * d_ff), lambda i: (layer, 0, 0)),
                  _resident((None, d_ff, d), lambda i: (layer, 0, 0))],
        out_specs=pl.BlockSpec((tm, d), lambda i: (i, 0)),
        out_shape=jax.ShapeDtypeStruct((t, d), F32),
        scratch_shapes=[pltpu.VMEM((tm, d_ff), BF16)],
        compiler_params=_cparams(("arbitrary",)),
        name="ffn",
    )(x, nw, wi, wo)


def _inproj_body(x_ref, nw_ref, w_ref, *rest, fw, sw, gw, q_scale, kv_t, n_alias):
    if kv_t:
        wkvt_ref = rest[0]
        h_ref, q_ref, k_ref, v_ref, u_ref, gq_ref, z_ref, sm_ref, kt_ref, vt_ref = rest[1 + n_alias:]
    else:
        h_ref, q_ref, k_ref, v_ref, kv_ref, u_ref, gq_ref, z_ref, sm_ref = rest
    h = _rms_bf16(x_ref[...], nw_ref[...])
    h_ref[...] = h
    o = 0
    q = _dot(h, w_ref[:, o:o + fw]); o += fw
    q_ref[...] = (q * q_scale).astype(BF16)
    kv = _dot(h, w_ref[:, o:o + 2 * fw]); o += 2 * fw
    if kv_t:
        kvt = _dot_nt(wkvt_ref[...], h)
        kt_ref[...] = kvt[:fw]
        vt_ref[...] = kvt[fw:]
    else:
        kv_ref[...] = kv
    k_ref[...] = kv[:, :fw].astype(BF16)
    v_ref[...] = kv[:, fw:].astype(BF16)
    u_ref[...] = _dot(h, w_ref[:, o:o + sw]); o += sw
    gq_ref[...] = _dot(h, w_ref[:, o:o + 3 * gw]); o += 3 * gw
    z_ref[...] = _dot(h, w_ref[:, o:o + gw]); o += gw
    sm_ref[...] = _dot(h, w_ref[:, o:o + LANES])


def _inproj(x, nw, w, layer, fw, sw, gw, q_scale, w_kvt=None, seq_len=None, kv_stack=None, stack_shape=None):
    t, d = x.shape
    n = w.shape[2]
    kv_t = w_kvt is not None
    tm = max(c for c in (128, 256, 384, 512) if seq_len % c == 0) if kv_t else _row_tile(t)
    row = lambda width: pl.BlockSpec((tm, width), lambda i: (i, 0))
    in_specs = [row(d), _resident((None, 1, d), lambda i: (layer, 0, 0)),
                _resident((None, d, n), lambda i: (layer, 0, 0))]
    args = [x, nw, w]
    aliases = {}
    if kv_t:
        widths = (d, fw, fw, fw, sw, 3 * gw, gw, LANES)
        dts = (BF16, BF16, BF16, BF16, F32, F32, F32, F32)
        per_seq = seq_len // tm
        in_specs.append(_resident((None, 2 * fw, d), lambda i: (layer, 0, 0)))
        args.append(w_kvt)
        stack_spec = pl.BlockSpec((None, None, fw, tm), lambda i: (layer, i // per_seq, 0, i % per_seq))
        out_specs = [row(wd) for wd in widths] + [stack_spec, stack_spec]
        out_shape = ([jax.ShapeDtypeStruct((t, wd), dt) for wd, dt in zip(widths, dts)]
                     + [jax.ShapeDtypeStruct(stack_shape, F32)] * 2)
        if kv_stack is not None:
            in_specs += [pl.BlockSpec(memory_space=pl.ANY)] * 2
            args += list(kv_stack)
            aliases = {4: len(widths), 5: len(widths) + 1}
    else:
        widths = (d, fw, fw, fw, 2 * fw, sw, 3 * gw, gw, LANES)
        dts = (BF16, BF16, BF16, BF16, F32, F32, F32, F32, F32)
        out_specs = [row(wd) for wd in widths]
        out_shape = [jax.ShapeDtypeStruct((t, wd), dt) for wd, dt in zip(widths, dts)]
    return pl.pallas_call(
        functools.partial(_inproj_body, fw=fw, sw=sw, gw=gw, q_scale=q_scale, kv_t=kv_t, n_alias=len(aliases)),
        grid=(t // tm,),
        in_specs=in_specs,
        out_specs=out_specs,
        out_shape=out_shape,
        input_output_aliases=aliases,
        compiler_params=_cparams(("arbitrary",)),
        name="inproj",
    )(*args)


def _log_sigmoid(x):
    return jnp.minimum(x, 0.0) - jnp.log(1.0 + jnp.exp(-jnp.abs(x)))


def _fcum_body(lg_ref, b_ref, lf_ref, cum_ref, *, nblk):
    r = lax.broadcasted_iota(jnp.int32, (LANES, LANES), 0)
    c = lax.broadcasted_iota(jnp.int32, (LANES, LANES), 1)
    upper = (r <= c).astype(F32)
    carry = jnp.zeros((lg_ref.shape[0], 1), F32)
    for i in range(nblk):
        sl = slice(i * LANES, (i + 1) * LANES)
        lf = _log_sigmoid(lg_ref[:, sl] + b_ref[...])
        lf_ref[:, sl] = lf
        cs = _dot_hi(lf, upper) + carry
        carry = cs[:, LANES - 1:LANES]
        cum_ref[:, sl] = cs


def _fcum(logits, bias):
    ns, nh, length = logits.shape
    blk = pl.BlockSpec((None, nh, length), lambda s: (s, 0, 0))
    return pl.pallas_call(
        functools.partial(_fcum_body, nblk=length // LANES),
        grid=(ns,),
        in_specs=[blk, pl.BlockSpec((nh, 1), lambda s: (0, 0))],
        out_specs=[blk, blk],
        out_shape=[jax.ShapeDtypeStruct(logits.shape, F32)] * 2,
        compiler_params=_cparams(("arbitrary",)),
        name="fcum",
    )(logits, bias)


def _fox_prompt_body(q_ref, k_ref, v_ref, fk_ref, o_ref, m_sc, l_sc, acc_sc, s_sc, *, tq, hd, n_heads):
    qi = pl.program_id(1)
    lane = lax.broadcasted_iota(jnp.int32, (1, LANES), 1)
    lo = lane < hd
    reps = tq // LANES
    causal = lax.broadcasted_iota(jnp.int32, (tq, tq), 1) <= lax.broadcasted_iota(jnp.int32, (tq, tq), 0)
    for p in range(n_heads * hd // LANES):
        qp = q_ref[:, p * LANES:(p + 1) * LANES]
        qm = (jnp.where(lo, qp, jnp.zeros_like(qp)), jnp.where(lo, jnp.zeros_like(qp), qp))
        m_sc[...] = jnp.full(m_sc.shape, NEG_BIG, F32)
        l_sc[...] = jnp.zeros(l_sc.shape, F32)
        acc_sc[...] = jnp.zeros(acc_sc.shape, F32)

        def scores(j, p=p, qm=qm):
            start = pl.multiple_of(j * tq, tq)
            kb = k_ref[pl.ds(start, tq), p * LANES:(p + 1) * LANES]
            fkb = fk_ref[:, pl.ds(start, tq)] * LOG2E
            return [_dot_nt(qm[hh], kb) - fkb[2 * p + hh:2 * p + hh + 1, :] for hh in range(2)]

        def consume(j, s, masked, p=p):
            vb = v_ref[pl.ds(pl.multiple_of(j * tq, tq), tq), p * LANES:(p + 1) * LANES]
            if masked:
                s = [jnp.where(causal, sh, NEG_BIG) for sh in s]
            m_old = [m_sc[hh] for hh in range(2)]
            m_new = [jnp.maximum(m_old[hh], jnp.max(s[hh], axis=-1, keepdims=True)) for hh in range(2)]
            alpha = [jnp.exp2(m_old[hh] - m_new[hh]) for hh in range(2)]
            e = [jnp.exp2((s[hh] - pltpu.repeat(m_new[hh], reps, 1)).astype(BF16)) for hh in range(2)]
            pv = [_dot(e[hh], jnp.concatenate([vb, jnp.ones_like(vb)], axis=1)) for hh in range(2)]
            for hh in range(2):
                l_sc[hh] = alpha[hh] * l_sc[hh] + pv[hh][:, LANES:]
                m_sc[hh] = m_new[hh]
            acc_sc[...] = (jnp.where(lo, alpha[0], alpha[1]) * acc_sc[...]
                           + jnp.where(lo, pv[0][:, :LANES], pv[1][:, :LANES]))

        s0 = scores(0)
        s_sc[0], s_sc[1] = s0[0], s0[1]

        def body(j, carry):
            s_cur = [s_sc[0], s_sc[1]]
            s_nxt = scores(j + 1)
            consume(j, s_cur, False)
            s_sc[0], s_sc[1] = s_nxt[0], s_nxt[1]
            return carry

        lax.fori_loop(0, qi, body, 0)
        consume(qi, [s_sc[0], s_sc[1]], True)
        inv = 1.0 / jnp.where(lo, l_sc[0], l_sc[1])
        o_ref[:, p * LANES:(p + 1) * LANES] = (acc_sc[...] * inv).astype(BF16)


def _fox_prompt(q, k, v, fcum, nb, lp, n_heads, hd, tq):
    fw = n_heads * hd
    nq = lp // tq
    return pl.pallas_call(
        functools.partial(_fox_prompt_body, tq=tq, hd=hd, n_heads=n_heads),
        grid=(nb, nq),
        in_specs=[pl.BlockSpec((tq, fw), lambda b, i: (b * nq + i, 0)),
                  pl.BlockSpec((lp, fw), lambda b, i: (b, 0)),
                  pl.BlockSpec((lp, fw), lambda b, i: (b, 0)),
                  pl.BlockSpec((None, n_heads, lp), lambda b, i: (b, 0, 0))],
        out_specs=pl.BlockSpec((tq, fw), lambda b, i: (b * nq + i, 0)),
        out_shape=jax.ShapeDtypeStruct((nb * lp, fw), BF16),
        scratch_shapes=[pltpu.VMEM((2, tq, LANES), F32), pltpu.VMEM((2, tq, LANES), F32),
                        pltpu.VMEM((tq, LANES), F32), pltpu.VMEM((2, tq, tq), F32)],
        compiler_params=_cparams(("arbitrary", "arbitrary")),
        name="fox_prompt",
    )(q, k, v, fcum)


def _fox_sample_body(pt_ref, qbd_ref, cn_ref, kn_ref, vn_ref, *rest, g_pages, n_q, n_heads, hd, page):
    kt_refs = rest[:g_pages]
    vt_refs = rest[g_pages:2 * g_pages]
    lf_ref, o_ref, m_sc, l_sc, acc_sc, car_sc = rest[2 * g_pages:]
    g = pl.program_id(1)
    n_pages = pl.num_programs(1) * g_pages
    lfs = [lf_ref[pt_ref[pl.program_id(0), n_pages - 1 - (g * g_pages + i)]] for i in range(g_pages)]
    nr = n_q * n_heads
    fw = n_heads * hd
    qbd = qbd_ref[...]
    rr = lax.broadcasted_iota(jnp.int32, (nr, 1), 0)
    qidx = rr // n_heads
    lane = lax.broadcasted_iota(jnp.int32, (1, LANES), 1)
    col_new = jnp.concatenate([cn_ref[...]] * n_q, axis=0)
    row_new = jnp.sum(jnp.where(lane == qidx, col_new, 0.0), axis=-1, keepdims=True)

    def update(slots, scores, pv_ofs):
        m_old = [m_sc[i] for i in slots]
        m_new = [jnp.maximum(mo, jnp.max(s, axis=-1, keepdims=True)) for mo, s in zip(m_old, scores)]
        alpha = [jnp.exp2(mo - mn) for mo, mn in zip(m_old, m_new)]
        e = [jnp.exp2(s - mn) for s, mn in zip(scores, m_new)]
        pv = [f(ee.astype(BF16)) for f, ee in zip(pv_ofs, e)]
        for n, i in enumerate(slots):
            l_sc[i] = alpha[n] * l_sc[i] + jnp.sum(e[n], axis=-1, keepdims=True)
            m_sc[i] = m_new[n]
            acc_sc[i] = alpha[n] * acc_sc[i] + pv[n]

    @pl.when(g == 0)
    def _():
        m_sc[...] = jnp.full(m_sc.shape, NEG_BIG, F32)
        l_sc[...] = jnp.zeros(l_sc.shape, F32)
        acc_sc[...] = jnp.zeros(acc_sc.shape, F32)
        car_sc[...] = jnp.zeros(car_sc.shape, F32)
        s = _dot_nt(qbd, kn_ref[...]) + (row_new - col_new) * LOG2E
        s = jnp.where((lane <= qidx) & (lane < n_q), s, NEG_BIG)
        update([0], [s], [lambda e: _dot(e, vn_ref[...])])

    r = lax.broadcasted_iota(jnp.int32, (page, page), 0)
    c = lax.broadcasted_iota(jnp.int32, (page, page), 1)
    later = (r > c).astype(F32)
    suffix = _dot_hi(jnp.concatenate(lfs, axis=0), later)
    carry = car_sc[...]
    scores, pv_ofs = [], []
    for i in range(g_pages):
        bias = jnp.concatenate([suffix[i * n_heads:(i + 1) * n_heads] + carry] * n_q, axis=0) + row_new
        kt = kt_refs[i][...].reshape(fw, page).astype(BF16)
        scores.append(_dot(qbd, kt) + bias * LOG2E)
        vt = vt_refs[i][...].reshape(fw, page).astype(BF16)
        pv_ofs.append(lambda e, vt=vt: _dot_nt(e, vt))
        carry = carry + jnp.sum(lfs[i], axis=-1, keepdims=True)
    car_sc[...] = carry
    update(list(range(g_pages)), scores, pv_ofs)

    @pl.when(g == pl.num_programs(1) - 1)
    def _():
        ms = [m_sc[i] for i in range(g_pages)]
        m_all = functools.reduce(jnp.maximum, ms)
        wts = [jnp.exp2(mi - m_all) for mi in ms]
        l_all = sum(w * l_sc[i] for i, w in enumerate(wts))
        o = sum(w * acc_sc[i] for i, w in enumerate(wts)) / l_all
        lane_w = lax.broadcasted_iota(jnp.int32, (1, fw), 1)
        o = jnp.where(lane_w // hd == rr % n_heads, o, 0.0)
        o_ref[...] = jnp.sum(o.reshape(n_q, n_heads, fw), axis=1).astype(BF16)


def _fox_sample(page_table, qbd, cum_new, k_new, v_new, kt, vt, lft, layer, g_pages, n_q):
    nb, n_pages = page_table.shape
    _, _, n_heads, hd, page = kt.shape
    fw = n_heads * hd
    nr = n_q * n_heads
    n_groups = n_pages // g_pages

    def page_map(i):
        return lambda b, g, pt: (layer, pt[b, n_pages - 1 - (g * g_pages + i)], 0, 0, 0)

    per_b = lambda shape: pl.BlockSpec((None,) + shape, lambda b, g, pt: (b, 0, 0))
    in_specs = ([per_b((nr, fw)), per_b((n_heads, LANES)), per_b((LANES, fw)), per_b((LANES, fw))]
                + [pl.BlockSpec((None, None, n_heads, hd, page), page_map(i)) for i in range(g_pages)]
                + [pl.BlockSpec((None, None, n_heads, hd, page), page_map(i)) for i in range(g_pages)]
                + [_resident((None,) + lft.shape[1:], lambda b, g, pt: (layer, 0, 0, 0))])
    return pl.pallas_call(
        functools.partial(_fox_sample_body, g_pages=g_pages, n_q=n_q, n_heads=n_heads, hd=hd, page=page),
        grid_spec=pltpu.PrefetchScalarGridSpec(
            num_scalar_prefetch=1, grid=(nb, n_groups), in_specs=in_specs,
            out_specs=per_b((n_q, fw)),
            scratch_shapes=[pltpu.VMEM((g_pages, nr, 1), F32), pltpu.VMEM((g_pages, nr, 1), F32),
                            pltpu.VMEM((g_pages, nr, fw), F32), pltpu.VMEM((n_heads, 1), F32)]),
        out_shape=jax.ShapeDtypeStruct((nb, n_q, fw), BF16),
        compiler_params=_cparams(("arbitrary", "arbitrary")),
        name="fox_sample",
    )(page_table, qbd, cum_new, k_new, v_new, *([kt] * g_pages), *([vt] * g_pages), lft)


def _gelu_tanh(x):
    return 0.5 * x * (1.0 + jnp.tanh(math.sqrt(2.0 / math.pi) * (x + 0.044715 * x * x * x)))


def _s5_tables(a_re, a_im, b_re, b_im, c_re, c_im, d_skip, log_dt, n_last):
    ck = S5_CHUNK
    g, p, gs = b_re.shape
    gt = LANES // gs
    nt = g // gt
    dt = jnp.exp(log_dt)[:, None]
    mag_l, ang = a_re * dt, a_im * dt

    def apow_fn(n):
        m = jnp.exp(mag_l * n)
        return m * jnp.cos(ang * n), m * jnp.sin(ang * n)

    ab_re, ab_im = apow_fn(1.0)
    den = a_re * a_re + a_im * a_im
    cf_re = ((ab_re - 1.0) * a_re + ab_im * a_im) / den
    cf_im = (ab_im * a_re - (ab_re - 1.0) * a_im) / den
    bb_re = cf_re[..., None] * b_re - cf_im[..., None] * b_im
    bb_im = cf_re[..., None] * b_im + cf_im[..., None] * b_re
    pw_re, pw_im = jax.vmap(apow_fn)(jnp.arange(ck + 1, dtype=F32))
    cab_re = c_re[None] * pw_re[:, :, None, :] - c_im[None] * pw_im[:, :, None, :]
    cab_im = c_re[None] * pw_im[:, :, None, :] + c_im[None] * pw_re[:, :, None, :]
    w = (jnp.einsum('lgcp,gpd->lgcd', cab_re[:ck], bb_re, precision=HI)
         - jnp.einsum('lgcp,gpd->lgcd', cab_im[:ck], bb_im, precision=HI))
    w = w.at[0].add(jax.vmap(jnp.diag)(d_skip))
    wc = jnp.transpose(w, (1, 3, 0, 2)).reshape(nt, gt * gs, ck * gs)
    e_pow = ck - 1 - jnp.arange(ck)
    f_re = pw_re[e_pow][..., None] * bb_re[None] - pw_im[e_pow][..., None] * bb_im[None]
    f_im = pw_re[e_pow][..., None] * bb_im[None] + pw_im[e_pow][..., None] * bb_re[None]
    f_rows = lambda f: jnp.transpose(f, (1, 0, 3, 2)).reshape(nt, gt, ck, gs, p)
    fc = jnp.transpose(jnp.concatenate([f_rows(f_re), f_rows(f_im)], axis=-1), (0, 2, 1, 3, 4)
                       ).reshape(nt, ck, gt * gs, 2 * p)
    e_rows = lambda m: jnp.transpose(m, (1, 0, 3, 2)).reshape(nt, gt, ck, p, gs)
    ec = jnp.transpose(jnp.stack([e_rows(cab_re[1:]), e_rows(-cab_im[1:])], axis=1), (0, 3, 1, 2, 4, 5)
                       ).reshape(nt, ck, 2 * gt * p, gs)

    tile_vec = lambda v: v.reshape(nt, gt * p)
    nlev = 16
    lev = [apow_fn(float(ck * 2 ** i)) for i in range(nlev)]
    apow = jnp.stack([jnp.stack([tile_vec(r), tile_vec(i)], axis=1) for r, i in lev], axis=1)
    al_re, al_im = apow_fn(float(n_last))
    alast = jnp.stack([tile_vec(al_re), tile_vec(al_im)], axis=1)
    return wc.astype(BF16), fc.astype(BF16), ec.astype(BF16), apow, alast


def _s5_expand_body(wc_ref, fc_ref, ec_ref, w_ref, f_ref, e_ref, *, ck, gs, p):
    gt = LANES // gs
    iota = lambda shape, axis: lax.broadcasted_iota(jnp.int32, shape, axis)
    bf = lambda m: m.astype(F32).astype(BF16)
    r, q = iota((ck * gs, ck * LANES), 0), iota((ck * gs, ck * LANES), 1)
    spread_w = bf((r // gs == q // LANES) & (r % gs == q % gs))
    r, q = iota((LANES, ck * LANES), 0), iota((LANES, ck * LANES), 1)
    w_ref[...] = jnp.where(r // gs == (q % LANES) // gs, _dot(wc_ref[...], spread_w), 0.0).astype(BF16)
    half = gt * p
    r, q = iota((2 * p, 2 * half), 0), iota((2 * p, 2 * half), 1)
    spread_f = bf((r // p == q // half) & (r % p == q % p))
    r, q = iota((LANES, 2 * half), 0), iota((LANES, 2 * half), 1)
    keep_f = r // gs == (q % half) // p
    r, q = iota((gs, LANES), 0), iota((gs, LANES), 1)
    spread_e = bf(r == q % gs)
    r, q = iota((2 * half, LANES), 0), iota((2 * half, LANES), 1)
    keep_e = (r % half) // p == q // gs
    for s in range(ck):
        f_ref[s // 2, (s % 2) * LANES:(s % 2 + 1) * LANES, :] = jnp.where(
            keep_f, _dot(fc_ref[s], spread_f), 0.0).astype(BF16)
        e_ref[s // 2, :, (s % 2) * LANES:(s % 2 + 1) * LANES] = jnp.where(
            keep_e, _dot(ec_ref[s], spread_e), 0.0).astype(BF16)


def _s5_expand(wc, fc, ec, gs, p):
    nt = wc.shape[0]
    ck = S5_CHUNK
    sw = 2 * (LANES // gs) * p
    blk = lambda a: pl.BlockSpec((None,) + a.shape[1:], lambda j: (j,) + (0,) * (a.ndim - 1))
    shapes = [jax.ShapeDtypeStruct((nt, LANES, ck * LANES), BF16),
              jax.ShapeDtypeStruct((nt, ck // 2, 2 * LANES, sw), BF16),
              jax.ShapeDtypeStruct((nt, ck // 2, sw, 2 * LANES), BF16)]
    return pl.pallas_call(
        functools.partial(_s5_expand_body, ck=ck, gs=gs, p=p),
        grid=(nt,),
        in_specs=[blk(wc), blk(fc), blk(ec)],
        out_specs=[blk(s) for s in shapes],
        out_shape=shapes,
        compiler_params=_cparams(("arbitrary",)),
        name="s5_expand",
    )(wc, fc, ec)


def _s5_local_body(u_ref, w_ref, f_ref, y_ref, z_ref, acc_ref, *, ck, n):
    accz = None
    zero = jnp.zeros((LANES, LANES), BF16)
    for s in range(0, ck, 2):
        us = jnp.concatenate([u_ref[pl.ds(s, n, stride=ck), :].astype(BF16),
                              u_ref[pl.ds(s + 1, n, stride=ck), :].astype(BF16)], axis=1)
        wide = (ck - s) * LANES
        w_two = jnp.concatenate([w_ref[:, :wide],
                                 jnp.concatenate([zero, w_ref[:, :wide - LANES]], axis=1)], axis=0)
        y = _dot(us, w_two)
        if s == 0:
            acc_ref[...] = y
        else:
            acc_ref[:, s * LANES:] += y
        z = _dot(us, f_ref[s // 2])
        accz = z if accz is None else accz + z
    for t in range(ck):
        y_ref[pl.ds(t, n, stride=ck), :] = acc_ref[:, t * LANES:(t + 1) * LANES]
    z_ref[...] = accz


def _s5_local(u, wcat, fmat, rows_per_step):
    r, width = u.shape
    ck = S5_CHUNK
    nt = width // LANES
    sw = fmat.shape[3]
    n = rows_per_step // ck
    return pl.pallas_call(
        functools.partial(_s5_local_body, ck=ck, n=n),
        grid=(nt, r // rows_per_step),
        in_specs=[pl.BlockSpec((rows_per_step, LANES), lambda j, i: (i, j)),
                  pl.BlockSpec((None, LANES, ck * LANES), lambda j, i: (j, 0, 0)),
                  pl.BlockSpec((None, ck // 2, 2 * LANES, sw), lambda j, i: (j, 0, 0, 0))],
        out_specs=[pl.BlockSpec((rows_per_step, LANES), lambda j, i: (i, j)),
                   pl.BlockSpec((None, n, sw), lambda j, i: (j, i, 0))],
        out_shape=[jax.ShapeDtypeStruct((r, width), F32), jax.ShapeDtypeStruct((nt, r // ck, sw), F32)],
        scratch_shapes=[pltpu.VMEM((n, ck * LANES), F32)],
        compiler_params=_cparams(("arbitrary", "arbitrary")),
        name="s5_local",
    )(u, wcat, fmat)


def _s5_scan_body(z_ref, ap_ref, xs_ref, xf_ref, *, nc, half, last):
    xr = z_ref[:, :half]
    xi = z_ref[:, half:]
    row = lax.broadcasted_iota(jnp.int32, (nc, 1), 0)
    lev, s = 0, 1
    while s < nc:
        ar = ap_ref[lev, 0:1, :]
        ai = ap_ref[lev, 1:2, :]
        keep = row >= s
        sr = jnp.where(keep, pltpu.roll(xr, s, 0), 0.0)
        si = jnp.where(keep, pltpu.roll(xi, s, 0), 0.0)
        xr, xi = xr + ar * sr - ai * si, xi + ar * si + ai * sr
        lev, s = lev + 1, s * 2
    xf_ref[:, :half] = xr[last:last + 1, :]
    xf_ref[:, half:] = xi[last:last + 1, :]
    xs_ref[:, :half] = jnp.where(row >= 1, pltpu.roll(xr, 1, 0), 0.0)
    xs_ref[:, half:] = jnp.where(row >= 1, pltpu.roll(xi, 1, 0), 0.0)


def _s5_scan(z, apow, nb, n_real):
    nt, n, sw = z.shape
    nc = n // nb
    nlev = apow.shape[1]
    return pl.pallas_call(
        functools.partial(_s5_scan_body, nc=nc, half=sw // 2, last=n_real - 1),
        grid=(nt, nb),
        in_specs=[pl.BlockSpec((None, nc, sw), lambda j, b: (j, b, 0)),
                  pl.BlockSpec((None, nlev, 2, sw // 2), lambda j, b: (j, 0, 0, 0))],
        out_specs=[pl.BlockSpec((None, nc, sw), lambda j, b: (j, b, 0)),
                   pl.BlockSpec((None, None, 1, sw), lambda j, b: (j, b, 0, 0))],
        out_shape=[jax.ShapeDtypeStruct((nt, n, sw), F32), jax.ShapeDtypeStruct((nt, nb, 1, sw), F32)],
        compiler_params=_cparams(("arbitrary", "arbitrary")),
        name="s5_scan",
    )(z, apow)


def _s5_out_body(y_ref, xs_ref, e_ref, o_ref, act_ref, *, ck, n, first):
    xs = xs_ref[...].astype(BF16)
    for t in range(0, ck, 2):
        pair = [y_ref[pl.ds(t + i, n, stride=ck), :] for i in range(2)]
        if t >= first:
            both = _dot(xs, e_ref[(t - first) // 2])
            pair = [pair[i] + both[:, i * LANES:(i + 1) * LANES] for i in range(2)]
        for i in range(2):
            act_ref[pl.ds(t + i, n, stride=ck), :] = _gelu_tanh(pair[i])
    o_ref[...] = act_ref[...].astype(BF16)


def _s5_out(ylocal, xstart, emat, rows_per_step, first):
    r, width = ylocal.shape
    ck = S5_CHUNK
    nt = width // LANES
    sw = xstart.shape[2]
    n = rows_per_step // ck
    return pl.pallas_call(
        functools.partial(_s5_out_body, ck=ck, n=n, first=first),
        grid=(nt, r // rows_per_step),
        in_specs=[pl.BlockSpec((rows_per_step, LANES), lambda j, i: (i, j)),
                  pl.BlockSpec((None, n, sw), lambda j, i: (j, i, 0)),
                  pl.BlockSpec((None, ck // 2, sw, 2 * LANES), lambda j, i: (j, 0, 0, 0))],
        out_specs=pl.BlockSpec((rows_per_step, LANES), lambda j, i: (i, j)),
        out_shape=jax.ShapeDtypeStruct((r, width), BF16),
        scratch_shapes=[pltpu.VMEM((rows_per_step, LANES), F32)],
        compiler_params=_cparams(("arbitrary", "arbitrary")),
        name="s5_out",
    )(ylocal, xstart, emat)


def _s5_step_body(z_ref, x0_ref, al_ref, xf_ref, *, half):
    ar, ai = al_ref[0:1, :], al_ref[1:2, :]
    xr, xi = x0_ref[:, :half], x0_ref[:, half:]
    xf_ref[:, :half] = z_ref[:, :half] + ar * xr - ai * xi
    xf_ref[:, half:] = z_ref[:, half:] + ar * xi + ai * xr


def _s5_step(z, x0, alast):
    nt, n, sw = z.shape
    blk = pl.BlockSpec((None, n, sw), lambda j: (j, 0, 0))
    return pl.pallas_call(
        functools.partial(_s5_step_body, half=sw // 2),
        grid=(nt,),
        in_specs=[blk, blk, pl.BlockSpec((None, 2, sw // 2), lambda j: (j, 0, 0))],
        out_specs=blk,
        out_shape=jax.ShapeDtypeStruct(z.shape, F32),
        compiler_params=_cparams(("arbitrary",)),
        name="s5_step",
    )(z, x0, alast)


def _gdn_body(x_ref, z_ref, sm_ref, ci_ref, s0_ref, cw_ref, na_ref, db_ref, nw_ref, o_ref, sf_ref, s_sc, car_sc,
              *, c, sb, n_valid, n_heads, hd, kconv, a_lane, b_lane):
    ci = pl.program_id(1)
    gw = n_heads * hd

    hist = ci_ref.shape[1]

    @pl.when(ci == 0)
    def _():
        s_sc[...] = s0_ref[...]
        car_sc[:, :hist, :] = ci_ref[...]

    r = lax.broadcasted_iota(jnp.int32, (c, c), 0)
    cc = lax.broadcasted_iota(jnp.int32, (c, c), 1)
    incl = r >= cc
    strict = r > cc
    eye = (r == cc).astype(F32)
    row = ci * c + lax.broadcasted_iota(jnp.int32, (c, 1), 0)
    chains = []
    for b in range(sb):
        car_sc[b, hist:, :] = x_ref[b]
        y = None
        for j in range(kconv):
            term = car_sc[b, pl.ds(hist - (kconv - 1) + j, c), :] * cw_ref[j:j + 1, :]
            y = term if y is None else y + term
        car_sc[b, :hist, :] = car_sc[b, c:, :]
        qkv = y * _sigmoid(y)

        sm = sm_ref[b]
        sp_in = sm + db_ref[...]
        softplus = jnp.maximum(sp_in, 0.0) + jnp.log(1.0 + jnp.exp(-jnp.abs(sp_in)))
        gfull = na_ref[...] * softplus
        beta_full = _sigmoid(sm)
        if n_valid is not None:
            live = row < n_valid
            gfull = jnp.where(live, gfull, 0.0)
            beta_full = jnp.where(live, beta_full, 0.0)
            qkv = jnp.where(live, qkv, 0.0)
        cum_full = _dot_hi(incl.astype(F32), gfull)
        cum_t = cum_full.T
        for h in range(n_heads):
            q = qkv[:, h * hd:(h + 1) * hd]
            k = qkv[:, gw + h * hd:gw + (h + 1) * hd]
            v = qkv[:, 2 * gw + h * hd:2 * gw + (h + 1) * hd]
            q = q * lax.rsqrt(jnp.sum(q * q, axis=-1, keepdims=True) + 1e-6) * (hd ** -0.5)
            k = k * lax.rsqrt(jnp.sum(k * k, axis=-1, keepdims=True) + 1e-6)
            cum = cum_full[:, a_lane + h:a_lane + h + 1]
            cum_r = cum_t[a_lane + h:a_lane + h + 1, :]
            beta = beta_full[:, b_lane + h:b_lane + h + 1]
            decay = jnp.where(incl, jnp.exp(jnp.where(incl, cum - cum_r, 0.0)), 0.0)
            kb = k * beta
            k16 = k.astype(BF16)
            ecum = jnp.exp(cum)
            g_last = cum[c - 1:c, :]
            chains.append(dict(
                b=b, h=h, k16=k16, decay=decay, qe16=(q * ecum).astype(BF16), q16=q.astype(BF16),
                rhs=jnp.concatenate([v * beta, kb * ecum], axis=-1), kb16=kb.astype(BF16),
                kd=k * jnp.exp(g_last - cum), s_scale=jnp.exp(g_last)))
    for ch in chains:
        ch['lower'] = jnp.where(strict, _dot_nt(ch['kb16'], ch['k16']) * ch['decay'], 0.0)
        ch['aqk16'] = jnp.where(incl, _dot_nt(ch['q16'], ch['k16']) * ch['decay'], 0.0).astype(BF16)
    for ch in chains:
        ch['t_inv'] = eye - ch['lower']
        l_hi, l_lo = _split(ch['lower'])
        ch['pw'] = _dot3s(l_hi, l_lo, l_hi, l_lo)
    span = 2
    while span < c:
        span *= 2
        for ch in chains:
            p_hi, p_lo = _split(ch['pw'])
            t_hi, t_lo = _split(ch['t_inv'])
            if span < c:
                both = _dot3s(p_hi, p_lo, jnp.concatenate([t_hi, p_hi], axis=1),
                              jnp.concatenate([t_lo, p_lo], axis=1))
                ch['t_inv'] = ch['t_inv'] + both[:, :c]
                ch['pw'] = both[:, c:]
            else:
                ch['t_inv'] = ch['t_inv'] + _dot3s(p_hi, p_lo, t_hi, t_lo)
    for ch in chains:
        ch['sol'] = _dot3(ch['t_inv'], ch['rhs'])
    for ch in chains:
        s_h = s_sc[ch['b'], ch['h']]
        s16 = s_h.astype(BF16)
        v_new = ch['sol'][:, :hd] - _dot(ch['sol'][:, hd:].astype(BF16), s16)
        ch['vn16'] = v_new.astype(BF16)
        ch['o'] = _dot(ch['qe16'], s16)
        ch['s_dec'] = s_h * ch['s_scale']
    for ch in chains:
        b, h = ch['b'], ch['h']
        o = ch['o'] + _dot(ch['aqk16'], ch['vn16'])
        s_sc[b, h] = ch['s_dec'] + _dot(ch['kd'].T.astype(BF16), ch['vn16'])
        o = o * lax.rsqrt(jnp.mean(o * o, axis=-1, keepdims=True) + RMS_EPS) * nw_ref[...]
        zz = z_ref[b, :, h * hd:(h + 1) * hd]
        o_ref[b, :, h * hd:(h + 1) * hd] = (o * (zz * _sigmoid(zz))).astype(BF16)

    @pl.when(ci == pl.num_programs(1) - 1)
    def _():
        sf_ref[...] = s_sc[...]


def _gdn(x, z, sm, conv_init, s0, conv_w, neg_a, dt_b, norm_w, sb, n_valid, a_lane, b_lane):
    ns, length, w3 = x.shape
    n_heads, hd = s0.shape[1], s0.shape[2]
    gw = n_heads * hd
    c = GDN_CHUNK
    kconv = conv_w.shape[0]
    hist = conv_init.shape[1]
    rowblk = lambda width: pl.BlockSpec((sb, c, width), lambda s, i: (s, i, 0))
    const = lambda shape: pl.BlockSpec(shape, lambda s, i: (0,) * len(shape))
    return pl.pallas_call(
        functools.partial(_gdn_body, c=c, sb=sb, n_valid=n_valid, n_heads=n_heads, hd=hd, kconv=kconv,
                          a_lane=a_lane, b_lane=b_lane),
        grid=(ns // sb, length // c),
        in_specs=[rowblk(w3), rowblk(gw), rowblk(LANES),
                  pl.BlockSpec((sb, hist, w3), lambda s, i: (s, 0, 0)),
                  pl.BlockSpec((sb, n_heads, hd, hd), lambda s, i: (s, 0, 0, 0)),
                  const((kconv, w3)), const((1, LANES)), const((1, LANES)), const((1, hd))],
        out_specs=[rowblk(gw), pl.BlockSpec((sb, n_heads, hd, hd), lambda s, i: (s, 0, 0, 0))],
        out_shape=[jax.ShapeDtypeStruct((ns, length, gw), BF16), jax.ShapeDtypeStruct(s0.shape, F32)],
        scratch_shapes=[pltpu.VMEM((sb, n_heads, hd, hd), F32), pltpu.VMEM((sb, hist + c, w3), F32)],
        compiler_params=_cparams(("arbitrary", "arbitrary")),
        name="gdn",
    )(x, z, sm, conv_init, s0, conv_w, neg_a, dt_b, norm_w)


def _merge_body(x_ref, h_ref, of_ref, z5_ref, og_ref, wf_ref, w5_ref, wg_ref, wgate_ref, wo_ref, o_ref, *, d):
    fox_br = _dot(of_ref[...], wf_ref[...])
    glu = _dot(z5_ref[...], w5_ref[...])
    s5_br = glu[:, :d] * _sigmoid(glu[:, d:])
    gdn_br = _dot(og_ref[...], wg_ref[...])
    h = h_ref[...]
    merged = (_sigmoid(_dot(h, wgate_ref[:, :d])) * fox_br
              + _sigmoid(_dot(h, wgate_ref[:, d:2 * d])) * s5_br
              + _sigmoid(_dot(h, wgate_ref[:, 2 * d:])) * gdn_br)
    o_ref[...] = x_ref[...] + _dot(merged.astype(BF16), wo_ref[...])


def _merge(x, h, o_fox, z5, o_gdn, wf, w5, wg, wgate, wo, layer):
    t, d = x.shape
    tm = _row_tile(t)
    row = lambda a: pl.BlockSpec((tm, a.shape[1]), lambda i: (i, 0))
    res = lambda a: _resident((None,) + a.shape[1:], lambda i: (layer, 0, 0))
    return pl.pallas_call(
        functools.partial(_merge_body, d=d),
        grid=(t // tm,),
        in_specs=[row(x), row(h), row(o_fox), row(z5), row(o_gdn), res(wf), res(w5), res(wg), res(wgate), res(wo)],
        out_specs=pl.BlockSpec((tm, d), lambda i: (i, 0)),
        out_shape=jax.ShapeDtypeStruct((t, d), F32),
        compiler_params=_cparams(("arbitrary",)),
        name="merge",
    )(x, h, o_fox, z5, o_gdn, wf, w5, wg, wgate, wo)


def _final_norm_body(x_ref, w_ref, o_ref):
    x = x_ref[...]
    ms = jnp.mean(x * x, axis=-1, keepdims=True)
    o_ref[...] = x * lax.rsqrt(ms + RMS_EPS) * w_ref[...]


def _final_norm(x, w):
    t, d = x.shape
    tm = _row_tile(t)
    return pl.pallas_call(
        _final_norm_body,
        grid=(t // tm,),
        in_specs=[pl.BlockSpec((tm, d), lambda i: (i, 0)), pl.BlockSpec((1, d), lambda i: (0, 0))],
        out_specs=pl.BlockSpec((tm, d), lambda i: (i, 0)),
        out_shape=jax.ShapeDtypeStruct((t, d), F32),
        compiler_params=_cparams(("arbitrary",)),
        name="final_norm",
    )(x, w)


def kernel(x_prompt, x_sample, cache_fox_k, cache_fox_v, cache_fox_logf, state_s5_re, state_s5_im, state_gdn, state_gdn_conv, page_table, meta_tokens, norm_ffn1, w_ffn1_in, w_ffn1_out, norm_mix, w_in, fox_b_f, w_fox_br, s5_A_re, s5_A_im, s5_B_re, s5_B_im, s5_C_re, s5_C_im, s5_D, s5_log_dt, w_s5_glu, gdn_conv_w, gdn_A_log, gdn_dt_bias, gdn_norm, w_gdn_br, w_out, norm_ffn2, w_ffn2_in, w_ffn2_out, norm_final):
    nb, seq, d = x_prompt.shape
    ds_b, ds_t, _ = x_sample.shape
    depth = w_in.shape[0]
    n_meta = meta_tokens.shape[0]
    fh, fhd = cache_fox_k.shape[3], cache_fox_k.shape[4]
    fw = fh * fhd
    page = cache_fox_k.shape[2]
    sg, sp, sgs = s5_B_re.shape[1:]
    sw = sg * sgs
    gh, ghd = state_gdn.shape[2], state_gdn.shape[3]
    gw = gh * ghd
    kconv = gdn_conv_w.shape[1]
    ck = S5_CHUNK
    gc = GDN_CHUNK
    assert fw % LANES == 0 and 2 * fhd == LANES and ghd == LANES and sw % LANES == 0 and LANES % sgs == 0
    assert ds_t <= ck and (ck - ds_t) % 2 == 0 and ds_t >= kconv - 1 and ds_t <= gc
    assert (n_meta + seq) % ck == 0 and n_meta + seq >= kconv - 1
    assert page == LANES and fh + 2 * gh <= LANES

    lreal = n_meta + seq
    pad = (-lreal) % LANES
    lp = lreal + pad
    tp = nb * lp
    ts = ds_b * ds_t
    tq = max(c for c in (128, 256, 384, 512) if lp % c == 0)
    n_pages = page_table.shape[1]
    g_pages = max(c for c in (1, 2, 4, 8) if n_pages % c == 0)
    hist = 8
    nt = sw // LANES
    gt = LANES // sgs
    ssw = 2 * gt * sp

    sizes = (fw, fw, fw, fh, sw, 3 * gw, gh, gh, gw, 3 * d)
    offs = [0]
    for s in sizes:
        offs.append(offs[-1] + s)
    col = lambda i: w_in[:, :, offs[i]:offs[i + 1]]
    small = jnp.concatenate([col(3), col(6), col(7),
                             jnp.zeros((depth, d, LANES - fh - 2 * gh), F32)], axis=2)
    a_lane, b_lane = fh, fh + gh
    w_proj = jnp.concatenate([col(0), col(1), col(2), col(4), col(5), col(8), small], axis=2).astype(BF16)
    w_gate = col(9).astype(BF16)
    w_kvt = jnp.transpose(w_in[:, :, offs[1]:offs[3]], (0, 2, 1)).astype(BF16)
    q_scale = fhd ** -0.5 * LOG2E
    wi1, wo1 = w_ffn1_in.astype(BF16), w_ffn1_out.astype(BF16)
    wi2, wo2 = w_ffn2_in.astype(BF16), w_ffn2_out.astype(BF16)
    wf16, w516, wg16, wo16 = (w.astype(BF16) for w in (w_fox_br, w_s5_glu, w_gdn_br, w_out))
    nrm = lambda w: w.reshape(depth, 1, d)
    n1, nm, n2 = nrm(norm_ffn1), nrm(norm_mix), nrm(norm_ffn2)
    lane_vec = lambda v, at: jnp.zeros((depth, 1, LANES), F32).at[:, 0, at:at + v.shape[1]].set(v)
    neg_a = lane_vec(-jnp.exp(gdn_A_log), a_lane)
    dt_b = lane_vec(gdn_dt_bias, a_lane)
    s5_c = jax.vmap(lambda *prm: _s5_tables(*prm, ds_t))(
        s5_A_re, s5_A_im, s5_B_re, s5_B_im, s5_C_re, s5_C_im, s5_D, s5_log_dt)
    s5_tab = [tuple(_s5_expand(s5_c[0][l], s5_c[1][l], s5_c[2][l], sgs, sp)) + (s5_c[3][l], s5_c[4][l])
              for l in range(depth)]

    kt_cache = jnp.transpose(cache_fox_k, (0, 1, 3, 4, 2))
    vt_cache = jnp.transpose(cache_fox_v, (0, 1, 3, 4, 2))
    lf_cache = jnp.transpose(cache_fox_logf, (0, 1, 3, 2))

    meta = jnp.broadcast_to(meta_tokens[None].astype(F32), (nb, n_meta, d))
    xp = jnp.concatenate([meta, x_prompt, jnp.zeros((nb, pad, d), F32)], axis=1).reshape(tp, d)
    xs = x_sample.reshape(ts, d)

    zeros_conv = jnp.zeros((nb, hist, 3 * gw), F32)
    zeros_gdn = jnp.zeros((nb, gh, ghd, ghd), F32)
    sb_p = 2 if nb % 2 == 0 else 1
    sb_s = 2 if ds_b % 2 == 0 else 1
    head_eye = jnp.eye(fh, dtype=BF16).reshape(1, 1, fh, fh, 1)
    to_tile = lambda st: jnp.transpose(st.reshape(ds_b, nt, gt * sp), (1, 0, 2))
    from_tile = lambda a, n: jnp.transpose(a, (1, 0, 2)).reshape(n, sg, sp)
    outs = [[] for _ in range(12)]
    kv_stack = None

    for l in range(depth):
        xp = _ffn(xp, n1, wi1, wo1, l)
        xs = _ffn(xs, n1, wi1, wo1, l)
        hp, qp, kp, vp, up, gqp, zp, smp, k_stack, v_stack = _inproj(
            xp, nm, w_proj, l, fw, sw, gw, q_scale, w_kvt, lp, kv_stack, (depth, nb, fw, lp))
        kv_stack = (k_stack, v_stack)
        hs, qs, ks, vs, kvs, us, gqs, zs, sms = _inproj(xs, nm, w_proj, l, fw, sw, gw, q_scale)
        bias_f = fox_b_f[l].reshape(fh, 1)

        lg_p = jnp.transpose(smp[:, :fh].reshape(nb, lp, fh), (0, 2, 1))
        lf_p, cum_p = _fcum(lg_p, bias_f)
        o_fox_p = _fox_prompt(qp, kp, vp, cum_p, nb, lp, fh, fhd, tq)
        lg_s = jnp.transpose(sms[:, :fh].reshape(ds_b, ds_t, fh), (0, 2, 1))
        lg_s = jnp.pad(lg_s, ((0, 0), (0, 0), (0, LANES - ds_t)))
        lf_s, cum_s = _fcum(lg_s, bias_f)
        qbd = (qs.reshape(ds_b, ds_t, 1, fh, fhd) * head_eye).reshape(ds_b, ds_t * fh, fw)
        rows_pad = ((0, 0), (0, LANES - ds_t), (0, 0))
        k_new = jnp.pad(ks.reshape(ds_b, ds_t, fw), rows_pad)
        v_new = jnp.pad(vs.reshape(ds_b, ds_t, fw), rows_pad)
        o_fox_s = _fox_sample(page_table, qbd, cum_s, k_new, v_new, kt_cache, vt_cache, lf_cache, l, g_pages, ds_t)

        wcat, fmat, emat, apow, alast = s5_tab[l]
        y_loc, z_loc = _s5_local(up, wcat, fmat, lp)
        x_start, x_fin = _s5_scan(z_loc, apow, nb, lreal // ck)
        z5_p = _s5_out(y_loc, x_start, emat, lp, 0)
        us_c = jnp.pad(us.reshape(ds_b, ds_t, sw), ((0, 0), (ck - ds_t, 0), (0, 0))).reshape(ds_b * ck, sw)
        x0 = jnp.concatenate([to_tile(state_s5_re[l]), to_tile(state_s5_im[l])], axis=2)
        y_loc_s, z_loc_s = _s5_local(us_c, wcat, fmat, ds_b * ck)
        z5_s = _s5_out(y_loc_s, x0, emat, ds_b * ck, ck - ds_t)
        z5_s = z5_s.reshape(ds_b, ck, sw)[:, ck - ds_t:].reshape(ts, sw)
        xf_s = _s5_step(z_loc_s, x0, alast)

        cw = gdn_conv_w[l]
        gnorm = gdn_norm[l].reshape(1, ghd)
        o_gdn_p, s_p = _gdn(gqp.reshape(nb, lp, 3 * gw), zp.reshape(nb, lp, gw), smp.reshape(nb, lp, LANES),
                            zeros_conv, zeros_gdn, cw, neg_a[l], dt_b[l], gnorm, sb_p, lreal if pad else None,
                            a_lane, b_lane)
        seq_pad = lambda a: jnp.pad(a.reshape(ds_b, ds_t, a.shape[1]), ((0, 0), (0, gc - ds_t), (0, 0)))
        conv_s = jnp.pad(state_gdn_conv[l], ((0, 0), (hist - (kconv - 1), 0), (0, 0)))
        o_gdn_s, s_s = _gdn(seq_pad(gqs), seq_pad(zs), seq_pad(sms), conv_s, state_gdn[l], cw, neg_a[l], dt_b[l],
                            gnorm, sb_s, ds_t, a_lane, b_lane)

        xp = _merge(xp, hp, o_fox_p, z5_p, o_gdn_p.reshape(tp, gw), wf16, w516, wg16, w_gate, wo16, l)
        xs = _merge(xs, hs, o_fox_s.reshape(ts, fw), z5_s, o_gdn_s[:, :ds_t].reshape(ts, gw),
                    wf16, w516, wg16, w_gate, wo16, l)
        xp = _ffn(xp, n2, wi2, wo2, l)
        xs = _ffn(xs, n2, wi2, wo2, l)

        kv_s = kvs.reshape(ds_b, ds_t, 2, fh, fhd)
        xf_p = x_fin.reshape(nt, nb, ssw)
        gq_s = jnp.concatenate([state_gdn_conv[l], gqs.reshape(ds_b, ds_t, 3 * gw)], axis=1)
        new = (jnp.transpose(lf_p[:, :, :lreal], (0, 2, 1)),
               kv_s[:, :, 0], kv_s[:, :, 1], jnp.transpose(lf_s[:, :, :ds_t], (0, 2, 1)),
               from_tile(xf_p[:, :, :ssw // 2], nb), from_tile(xf_p[:, :, ssw // 2:], nb),
               from_tile(xf_s[:, :, :ssw // 2], ds_b), from_tile(xf_s[:, :, ssw // 2:], ds_b),
               s_p, s_s, gqp.reshape(nb, lp, 3 * gw)[:, lreal - (kconv - 1):lreal], gq_s[:, -(kconv - 1):])
        for i, a in enumerate(new):
            outs[i].append(a)

    y_prompt = _final_norm(xp, norm_final.reshape(1, d)).reshape(nb, lp, d)[:, n_meta:lreal]
    y_sample = _final_norm(xs, norm_final.reshape(1, d)).reshape(ds_b, ds_t, d)
    k_prompt, v_prompt = (jnp.transpose(a.reshape(depth, nb, fh, fhd, lp)[..., :lreal], (0, 1, 4, 2, 3))
                          for a in kv_stack)
    return (y_prompt, y_sample, k_prompt, v_prompt) + tuple(jnp.stack(o) for o in outs)
```

```python
import functools
import math

import jax
import jax.numpy as jnp
from jax import lax
from jax.experimental import pallas as pl
from jax.experimental.pallas import tpu as pltpu

F32 = jnp.float32
BF16 = jnp.bfloat16
RMS_EPS = 1e-6
NEG_BIG = -1e30
LANES = 128
VMEM_LIMIT = 56 * 1024 * 1024
S5_CHUNK = 16
GDN_CHUNK = 128
HI = lax.Precision.HIGHEST
LOG2E = math.log2(math.e)


def _cparams(sem):
    return pltpu.CompilerParams(dimension_semantics=sem, vmem_limit_bytes=VMEM_LIMIT)


def _resident(shape, index_map):
    return pl.BlockSpec(shape, index_map, pipeline_mode=pl.Buffered(1))


def _dot(a, b):
    return jnp.dot(a, b, preferred_element_type=F32)


def _dot_hi(a, b):
    return jnp.dot(a, b, preferred_element_type=F32, precision=HI)


def _dot_nt(a, b):
    return lax.dot_general(a, b, (((1,), (1,)), ((), ())), preferred_element_type=F32)


def _split(x):
    hi = x.astype(BF16)
    return hi, (x - hi.astype(F32)).astype(BF16)


def _dot3s(a_hi, a_lo, b_hi, b_lo):
    return _dot(jnp.concatenate([a_hi, a_lo, a_hi], axis=1), jnp.concatenate([b_hi, b_hi, b_lo], axis=0))


def _dot3(a, b):
    return _dot3s(*_split(a), *_split(b))


def _rms_bf16(x, w):
    ms = jnp.mean(x * x, axis=-1, keepdims=True)
    return (x * lax.rsqrt(ms + RMS_EPS) * w).astype(BF16)


def _sigmoid(x):
    return 1.0 / (1.0 + jnp.exp(-x))


def _row_tile(t, cap=1024):
    best = 8
    for tm in range(8, min(t, cap) + 1, 8):
        if t % tm == 0:
            best = tm
    return best


def _ffn_body(x_ref, nw_ref, wi_ref, wo_ref, o_ref, act_ref, *, d_ff, fc):
    x = x_ref[...]
    h = _rms_bf16(x, nw_ref[...])
    for c in range(d_ff // fc):
        g = _dot(h, wi_ref[:, c * fc:(c + 1) * fc])
        u = _dot(h, wi_ref[:, d_ff + c * fc:d_ff + (c + 1) * fc])
        act_ref[:, c * fc:(c + 1) * fc] = (g * _sigmoid(g) * u).astype(BF16)
    o_ref[...] = x + 0.5 * _dot(act_ref[...], wo_ref[...])


def _ffn(x, nw, wi, wo, layer):
    t, d = x.shape
    tm = _row_tile(t)
    d_ff = wo.shape[1]
    fc = 256 if d_ff % 256 == 0 else d_ff
    return pl.pallas_call(
        functools.partial(_ffn_body, d_ff=d_ff, fc=fc),
        grid=(t // tm,),
        in_specs=[pl.BlockSpec((tm, d), lambda i: (i, 0)),
                  _resident((None, 1, d), lambda i: (layer, 0, 0)),
                  _resident((None, d, 2 * d_ff), lambda i: (layer, 0, 0)),
                  _resident((None, d_ff, d), lambda i: (layer, 0, 0))],
        out_specs=pl.BlockSpec((tm, d), lambda i: (i, 0)),
        out_shape=jax.ShapeDtypeStruct((t, d), F32),
        scratch_shapes=[pltpu.VMEM((tm, d_ff), BF16)],
        compiler_params=_cparams(("arbitrary",)),
        name="ffn",
    )(x, nw, wi, wo)


def _inproj_body(x_ref, nw_ref, w_ref, *rest, fw, sw, gw, q_scale, kv_t, n_alias):
    if kv_t:
        wkvt_ref = rest[0]
        h_ref, q_ref, k_ref, v_ref, u_ref, gq_ref, z_ref, sm_ref, kt_ref, vt_ref = rest[1 + n_alias:]
    else:
        h_ref, q_ref, k_ref, v_ref, kv_ref, u_ref, gq_ref, z_ref, sm_ref = rest
    h = _rms_bf16(x_ref[...], nw_ref[...])
    h_ref[...] = h
    o = 0
    q = _dot(h, w_ref[:, o:o + fw]); o += fw
    q_ref[...] = (q * q_scale).astype(BF16)
    kv = _dot(h, w_ref[:, o:o + 2 * fw]); o += 2 * fw
    if kv_t:
        kvt = _dot_nt(wkvt_ref[...], h)
        kt_ref[...] = kvt[:fw]
        vt_ref[...] = kvt[fw:]
    else:
        kv_ref[...] = kv
    k_ref[...] = kv[:, :fw].astype(BF16)
    v_ref[...] = kv[:, fw:].astype(BF16)
    u_ref[...] = _dot(h, w_ref[:, o:o + sw]); o += sw
    gq_ref[...] = _dot(h, w_ref[:, o:o + 3 * gw]); o += 3 * gw
    z_ref[...] = _dot(h, w_ref[:, o:o + gw]); o += gw
    sm_ref[...] = _dot(h, w_ref[:, o:o + LANES])


def _inproj(x, nw, w, layer, fw, sw, gw, q_scale, w_kvt=None, seq_len=None, kv_stack=None, stack_shape=None):
    t, d = x.shape
    n = w.shape[2]
    kv_t = w_kvt is not None
    tm = max(c for c in (128, 256, 384, 512) if seq_len % c == 0) if kv_t else _row_tile(t)
    row = lambda width: pl.BlockSpec((tm, width), lambda i: (i, 0))
    in_specs = [row(d), _resident((None, 1, d), lambda i: (layer, 0, 0)),
                _resident((None, d, n), lambda i: (layer, 0, 0))]
    args = [x, nw, w]
    aliases = {}
    if kv_t:
        widths = (d, fw, fw, fw, sw, 3 * gw, gw, LANES)
        dts = (BF16, BF16, BF16, BF16, F32, F32, F32, F32)
        per_seq = seq_len // tm
        in_specs.append(_resident((None, 2 * fw, d), lambda i: (layer, 0, 0)))
        args.append(w_kvt)
        stack_spec = pl.BlockSpec((None, None, fw, tm), lambda i: (layer, i // per_seq, 0, i % per_seq))
        out_specs = [row(wd) for wd in widths] + [stack_spec, stack_spec]
        out_shape = ([jax.ShapeDtypeStruct((t, wd), dt) for wd, dt in zip(widths, dts)]
                     + [jax.ShapeDtypeStruct(stack_shape, F32)] * 2)
        if kv_stack is not None:
            in_specs += [pl.BlockSpec(memory_space=pl.ANY)] * 2
            args += list(kv_stack)
            aliases = {4: len(widths), 5: len(widths) + 1}
    else:
        widths = (d, fw, fw, fw, 2 * fw, sw, 3 * gw, gw, LANES)
        dts = (BF16, BF16, BF16, BF16, F32, F32, F32, F32, F32)
        out_specs = [row(wd) for wd in widths]
        out_shape = [jax.ShapeDtypeStruct((t, wd), dt) for wd, dt in zip(widths, dts)]
    return pl.pallas_call(
        functools.partial(_inproj_body, fw=fw, sw=sw, gw=gw, q_scale=q_scale, kv_t=kv_t, n_alias=len(aliases)),
        grid=(t // tm,),
        in_specs=in_specs,
        out_specs=out_specs,
        out_shape=out_shape,
        input_output_aliases=aliases,
        compiler_params=_cparams(("arbitrary",)),
        name="inproj",
    )(*args)


def _log_sigmoid(x):
    return jnp.minimum(x, 0.0) - jnp.log(1.0 + jnp.exp(-jnp.abs(x)))


def _fcum_body(lg_ref, b_ref, lf_ref, cum_ref, *, nblk):
    r = lax.broadcasted_iota(jnp.int32, (LANES, LANES), 0)
    c = lax.broadcasted_iota(jnp.int32, (LANES, LANES), 1)
    upper = (r <= c).astype(F32)
    carry = jnp.zeros((lg_ref.shape[0], 1), F32)
    for i in range(nblk):
        sl = slice(i * LANES, (i + 1) * LANES)
        lf = _log_sigmoid(lg_ref[:, sl] + b_ref[...])
        lf_ref[:, sl] = lf
        cs = _dot_hi(lf, upper) + carry
        carry = cs[:, LANES - 1:LANES]
        cum_ref[:, sl] = cs


def _fcum(logits, bias):
    ns, nh, length = logits.shape
    blk = pl.BlockSpec((None, nh, length), lambda s: (s, 0, 0))
    return pl.pallas_call(
        functools.partial(_fcum_body, nblk=length // LANES),
        grid=(ns,),
        in_specs=[blk, pl.BlockSpec((nh, 1), lambda s: (0, 0))],
        out_specs=[blk, blk],
        out_shape=[jax.ShapeDtypeStruct(logits.shape, F32)] * 2,
        compiler_params=_cparams(("arbitrary",)),
        name="fcum",
    )(logits, bias)


def _fox_prompt_body(q_ref, k_ref, v_ref, fk_ref, o_ref, m_sc, l_sc, acc_sc, s_sc, *, tq, hd, n_heads):
    qi = pl.program_id(1)
    lane = lax.broadcasted_iota(jnp.int32, (1, LANES), 1)
    lo = lane < hd
    reps = tq // LANES
    causal = lax.broadcasted_iota(jnp.int32, (tq, tq), 1) <= lax.broadcasted_iota(jnp.int32, (tq, tq), 0)
    for p in range(n_heads * hd // LANES):
        qp = q_ref[:, p * LANES:(p + 1) * LANES]
        qm = (jnp.where(lo, qp, jnp.zeros_like(qp)), jnp.where(lo, jnp.zeros_like(qp), qp))
        m_sc[...] = jnp.full(m_sc.shape, NEG_BIG, F32)
        l_sc[...] = jnp.zeros(l_sc.shape, F32)
        acc_sc[...] = jnp.zeros(acc_sc.shape, F32)

        def scores(j, p=p, qm=qm):
            start = pl.multiple_of(j * tq, tq)
            kb = k_ref[pl.ds(start, tq), p * LANES:(p + 1) * LANES]
            fkb = fk_ref[:, pl.ds(start, tq)] * LOG2E
            return [_dot_nt(qm[hh], kb) - fkb[2 * p + hh:2 * p + hh + 1, :] for hh in range(2)]

        def consume(j, s, masked, p=p):
            vb = v_ref[pl.ds(pl.multiple_of(j * tq, tq), tq), p * LANES:(p + 1) * LANES]
            if masked:
                s = [jnp.where(causal, sh, NEG_BIG) for sh in s]
            m_old = [m_sc[hh] for hh in range(2)]
            m_new = [jnp.maximum(m_old[hh], jnp.max(s[hh], axis=-1, keepdims=True)) for hh in range(2)]
            alpha = [jnp.exp2(m_old[hh] - m_new[hh]) for hh in range(2)]
            e = [jnp.exp2((s[hh] - pltpu.repeat(m_new[hh], reps, 1)).astype(BF16)) for hh in range(2)]
            pv = [_dot(e[hh], jnp.concatenate([vb, jnp.ones_like(vb)], axis=1)) for hh in range(2)]
            for hh in range(2):
                l_sc[hh] = alpha[hh] * l_sc[hh] + pv[hh][:, LANES:]
                m_sc[hh] = m_new[hh]
            acc_sc[...] = (jnp.where(lo, alpha[0], alpha[1]) * acc_sc[...]
                           + jnp.where(lo, pv[0][:, :LANES], pv[1][:, :LANES]))

        s0 = scores(0)
        s_sc[0], s_sc[1] = s0[0], s0[1]

        def body(j, carry):
            s_cur = [s_sc[0], s_sc[1]]
            s_nxt = scores(j + 1)
            consume(j, s_cur, False)
            s_sc[0], s_sc[1] = s_nxt[0], s_nxt[1]
            return carry

        lax.fori_loop(0, qi, body, 0)
        consume(qi, [s_sc[0], s_sc[1]], True)
        inv = 1.0 / jnp.where(lo, l_sc[0], l_sc[1])
        o_ref[:, p * LANES:(p + 1) * LANES] = (acc_sc[...] * inv).astype(BF16)


def _fox_prompt(q, k, v, fcum, nb, lp, n_heads, hd, tq):
    fw = n_heads * hd
    nq = lp // tq
    return pl.pallas_call(
        functools.partial(_fox_prompt_body, tq=tq, hd=hd, n_heads=n_heads),
        grid=(nb, nq),
        in_specs=[pl.BlockSpec((tq, fw), lambda b, i: (b * nq + i, 0)),
                  pl.BlockSpec((lp, fw), lambda b, i: (b, 0)),
                  pl.BlockSpec((lp, fw), lambda b, i: (b, 0)),
                  pl.BlockSpec((None, n_heads, lp), lambda b, i: (b, 0, 0))],
        out_specs=pl.BlockSpec((tq, fw), lambda b, i: (b * nq + i, 0)),
        out_shape=jax.ShapeDtypeStruct((nb * lp, fw), BF16),
        scratch_shapes=[pltpu.VMEM((2, tq, LANES), F32), pltpu.VMEM((2, tq, LANES), F32),
                        pltpu.VMEM((tq, LANES), F32), pltpu.VMEM((2, tq, tq), F32)],
        compiler_params=_cparams(("arbitrary", "arbitrary")),
        name="fox_prompt",
    )(q, k, v, fcum)


def _fox_sample_body(pt_ref, qbd_ref, cn_ref, kn_ref, vn_ref, *rest, g_pages, n_q, n_heads, hd, page):
    kt_refs = rest[:g_pages]
    vt_refs = rest[g_pages:2 * g_pages]
    lf_ref, o_ref, m_sc, l_sc, acc_sc, car_sc = rest[2 * g_pages:]
    g = pl.program_id(1)
    n_pages = pl.num_programs(1) * g_pages
    lfs = [lf_ref[pt_ref[pl.program_id(0), n_pages - 1 - (g * g_pages + i)]] for i in range(g_pages)]
    nr = n_q * n_heads
    fw = n_heads * hd
    qbd = qbd_ref[...]
    rr = lax.broadcasted_iota(jnp.int32, (nr, 1), 0)
    qidx = rr // n_heads
    lane = lax.broadcasted_iota(jnp.int32, (1, LANES), 1)
    col_new = jnp.concatenate([cn_ref[...]] * n_q, axis=0)
    row_new = jnp.sum(jnp.where(lane == qidx, col_new, 0.0), axis=-1, keepdims=True)

    def update(slots, scores, pv_ofs):
        m_old = [m_sc[i] for i in slots]
        m_new = [jnp.maximum(mo, jnp.max(s, axis=-1, keepdims=True)) for mo, s in zip(m_old, scores)]
        alpha = [jnp.exp2(mo - mn) for mo, mn in zip(m_old, m_new)]
        e = [jnp.exp2(s - mn) for s, mn in zip(scores, m_new)]
        pv = [f(ee.astype(BF16)) for f, ee in zip(pv_ofs, e)]
        for n, i in enumerate(slots):
            l_sc[i] = alpha[n] * l_sc[i] + jnp.sum(e[n], axis=-1, keepdims=True)
            m_sc[i] = m_new[n]
            acc_sc[i] = alpha[n] * acc_sc[i] + pv[n]

    @pl.when(g == 0)
    def _():
        m_sc[...] = jnp.full(m_sc.shape, NEG_BIG, F32)
        l_sc[...] = jnp.zeros(l_sc.shape, F32)
        acc_sc[...] = jnp.zeros(acc_sc.shape, F32)
        car_sc[...] = jnp.zeros(car_sc.shape, F32)
        s = _dot_nt(qbd, kn_ref[...]) + (row_new - col_new) * LOG2E
        s = jnp.where((lane <= qidx) & (lane < n_q), s, NEG_BIG)
        update([0], [s], [lambda e: _dot(e, vn_ref[...])])

    r = lax.broadcasted_iota(jnp.int32, (page, page), 0)
    c = lax.broadcasted_iota(jnp.int32, (page, page), 1)
    later = (r > c).astype(F32)
    suffix = _dot_hi(jnp.concatenate(lfs, axis=0), later)
    carry = car_sc[...]
    scores, pv_ofs = [], []
    for i in range(g_pages):
        bias = jnp.concatenate([suffix[i * n_heads:(i + 1) * n_heads] + carry] * n_q, axis=0) + row_new
        kt = kt_refs[i][...].reshape(fw, page).astype(BF16)
        scores.append(_dot(qbd, kt) + bias * LOG2E)
        vt = vt_refs[i][...].reshape(fw, page).astype(BF16)
        pv_ofs.append(lambda e, vt=vt: _dot_nt(e, vt))
        carry = carry + jnp.sum(lfs[i], axis=-1, keepdims=True)
    car_sc[...] = carry
    update(list(range(g_pages)), scores, pv_ofs)

    @pl.when(g == pl.num_programs(1) - 1)
    def _():
        ms = [m_sc[i] for i in range(g_pages)]
        m_all = functools.reduce(jnp.maximum, ms)
        wts = [jnp.exp2(mi - m_all) for mi in ms]
        l_all = sum(w * l_sc[i] for i, w in enumerate(wts))
        o = sum(w * acc_sc[i] for i, w in enumerate(wts)) / l_all
        lane_w = lax.broadcasted_iota(jnp.int32, (1, fw), 1)
        o = jnp.where(lane_w // hd == rr % n_heads, o, 0.0)
        o_ref[...] = jnp.sum(o.reshape(n_q, n_heads, fw), axis=1).astype(BF16)


def _fox_sample(page_table, qbd, cum_new, k_new, v_new, kt, vt, lft, layer, g_pages, n_q):
    nb, n_pages = page_table.shape
    _, _, n_heads, hd, page = kt.shape
    fw = n_heads * hd
    nr = n_q * n_heads
    n_groups = n_pages // g_pages

    def page_map(i):
        return lambda b, g, pt: (layer, pt[b, n_pages - 1 - (g * g_pages + i)], 0, 0, 0)

    per_b = lambda shape: pl.BlockSpec((None,) + shape, lambda b, g, pt: (b, 0, 0))
    in_specs = ([per_b((nr, fw)), per_b((n_heads, LANES)), per_b((LANES, fw)), per_b((LANES, fw))]
                + [pl.BlockSpec((None, None, n_heads, hd, page), page_map(i)) for i in range(g_pages)]
                + [pl.BlockSpec((None, None, n_heads, hd, page), page_map(i)) for i in range(g_pages)]
                + [_resident((None,) + lft.shape[1:], lambda b, g, pt: (layer, 0, 0, 0))])
    return pl.pallas_call(
        functools.partial(_fox_sample_body, g_pages=g_pages, n_q=n_q, n_heads=n_heads, hd=hd, page=page),
        grid_spec=pltpu.PrefetchScalarGridSpec(
            num_scalar_prefetch=1, grid=(nb, n_groups), in_specs=in_specs,
            out_specs=per_b((n_q, fw)),
            scratch_shapes=[pltpu.VMEM((g_pages, nr, 1), F32), pltpu.VMEM((g_pages, nr, 1), F32),
                            pltpu.VMEM((g_pages, nr, fw), F32), pltpu.VMEM((n_heads, 1), F32)]),
        out_shape=jax.ShapeDtypeStruct((nb, n_q, fw), BF16),
        compiler_params=_cparams(("arbitrary", "arbitrary")),
        name="fox_sample",
    )(page_table, qbd, cum_new, k_new, v_new, *([kt] * g_pages), *([vt] * g_pages), lft)


def _gelu_tanh(x):
    return 0.5 * x * (1.0 + jnp.tanh(math.sqrt(2.0 / math.pi) * (x + 0.044715 * x * x * x)))


def _s5_tables(a_re, a_im, b_re, b_im, c_re, c_im, d_skip, log_dt, n_last):
    ck = S5_CHUNK
    g, p, gs = b_re.shape
    gt = LANES // gs
    nt = g // gt
    dt = jnp.exp(log_dt)[:, None]
    mag_l, ang = a_re * dt, a_im * dt

    def apow_fn(n):
        m = jnp.exp(mag_l * n)
        return m * jnp.cos(ang * n), m * jnp.sin(ang * n)

    ab_re, ab_im = apow_fn(1.0)
    den = a_re * a_re + a_im * a_im
    cf_re = ((ab_re - 1.0) * a_re + ab_im * a_im) / den
    cf_im = (ab_im * a_re - (ab_re - 1.0) * a_im) / den
    bb_re = cf_re[..., None] * b_re - cf_im[..., None] * b_im
    bb_im = cf_re[..., None] * b_im + cf_im[..., None] * b_re
    pw_re, pw_im = jax.vmap(apow_fn)(jnp.arange(ck + 1, dtype=F32))
    cab_re = c_re[None] * pw_re[:, :, None, :] - c_im[None] * pw_im[:, :, None, :]
    cab_im = c_re[None] * pw_im[:, :, None, :] + c_im[None] * pw_re[:, :, None, :]
    w = (jnp.einsum('lgcp,gpd->lgcd', cab_re[:ck], bb_re, precision=HI)
         - jnp.einsum('lgcp,gpd->lgcd', cab_im[:ck], bb_im, precision=HI))
    w = w.at[0].add(jax.vmap(jnp.diag)(d_skip))
    wc = jnp.transpose(w, (1, 3, 0, 2)).reshape(nt, gt * gs, ck * gs)
    e_pow = ck - 1 - jnp.arange(ck)
    f_re = pw_re[e_pow][..., None] * bb_re[None] - pw_im[e_pow][..., None] * bb_im[None]
    f_im = pw_re[e_pow][..., None] * bb_im[None] + pw_im[e_pow][..., None] * bb_re[None]
    f_rows = lambda f: jnp.transpose(f, (1, 0, 3, 2)).reshape(nt, gt, ck, gs, p)
    fc = jnp.transpose(jnp.concatenate([f_rows(f_re), f_rows(f_im)], axis=-1), (0, 2, 1, 3, 4)
                       ).reshape(nt, ck, gt * gs, 2 * p)
    e_rows = lambda m: jnp.transpose(m, (1, 0, 3, 2)).reshape(nt, gt, ck, p, gs)
    ec = jnp.transpose(jnp.stack([e_rows(cab_re[1:]), e_rows(-cab_im[1:])], axis=1), (0, 3, 1, 2, 4, 5)
                       ).reshape(nt, ck, 2 * gt * p, gs)

    tile_vec = lambda v: v.reshape(nt, gt * p)
    nlev = 16
    lev = [apow_fn(float(ck * 2 ** i)) for i in range(nlev)]
    apow = jnp.stack([jnp.stack([tile_vec(r), tile_vec(i)], axis=1) for r, i in lev], axis=1)
    al_re, al_im = apow_fn(float(n_last))
    alast = jnp.stack([tile_vec(al_re), tile_vec(al_im)], axis=1)
    wcat, fmat, emat = _s5_expand(wc.astype(BF16), fc.astype(BF16), ec.astype(BF16), gs, p)
    return wcat, fmat, emat, apow, alast


def _s5_expand_body(wc_ref, fc_ref, ec_ref, w_ref, f_ref, e_ref, *, ck, gs, p):
    gt = LANES // gs
    iota = lambda shape, axis: lax.broadcasted_iota(jnp.int32, shape, axis)
    bf = lambda m: m.astype(F32).astype(BF16)
    r, q = iota((ck * gs, ck * LANES), 0), iota((ck * gs, ck * LANES), 1)
    spread_w = bf((r // gs == q // LANES) & (r % gs == q % gs))
    r, q = iota((LANES, ck * LANES), 0), iota((LANES, ck * LANES), 1)
    w_ref[...] = jnp.where(r // gs == (q % LANES) // gs, _dot(wc_ref[...], spread_w), 0.0).astype(BF16)
    half = gt * p
    r, q = iota((2 * p, 2 * half), 0), iota((2 * p, 2 * half), 1)
    spread_f = bf((r // p == q // half) & (r % p == q % p))
    r, q = iota((LANES, 2 * half), 0), iota((LANES, 2 * half), 1)
    keep_f = r // gs == (q % half) // p
    r, q = iota((gs, LANES), 0), iota((gs, LANES), 1)
    spread_e = bf(r == q % gs)
    r, q = iota((2 * half, LANES), 0), iota((2 * half, LANES), 1)
    keep_e = (r % half) // p == q // gs
    for s in range(ck):
        f_ref[s // 2, (s % 2) * LANES:(s % 2 + 1) * LANES, :] = jnp.where(
            keep_f, _dot(fc_ref[s], spread_f), 0.0).astype(BF16)
        e_ref[s // 2, :, (s % 2) * LANES:(s % 2 + 1) * LANES] = jnp.where(
            keep_e, _dot(ec_ref[s], spread_e), 0.0).astype(BF16)


def _s5_expand(wc, fc, ec, gs, p):
    nt = wc.shape[0]
    ck = S5_CHUNK
    sw = 2 * (LANES // gs) * p
    blk = lambda a: pl.BlockSpec((None,) + a.shape[1:], lambda j: (j,) + (0,) * (a.ndim - 1))
    shapes = [jax.ShapeDtypeStruct((nt, LANES, ck * LANES), BF16),
              jax.ShapeDtypeStruct((nt, ck // 2, 2 * LANES, sw), BF16),
              jax.ShapeDtypeStruct((nt, ck // 2, sw, 2 * LANES), BF16)]
    return pl.pallas_call(
        functools.partial(_s5_expand_body, ck=ck, gs=gs, p=p),
        grid=(nt,),
        in_specs=[blk(wc), blk(fc), blk(ec)],
        out_specs=[blk(s) for s in shapes],
        out_shape=shapes,
        compiler_params=_cparams(("arbitrary",)),
        name="s5_expand",
    )(wc, fc, ec)


def _s5_local_body(u_ref, w_ref, f_ref, y_ref, z_ref, acc_ref, *, ck, n):
    accz = None
    zero = jnp.zeros((LANES, LANES), BF16)
    for s in range(0, ck, 2):
        us = jnp.concatenate([u_ref[pl.ds(s, n, stride=ck), :].astype(BF16),
                              u_ref[pl.ds(s + 1, n, stride=ck), :].astype(BF16)], axis=1)
        wide = (ck - s) * LANES
        w_two = jnp.concatenate([w_ref[:, :wide],
                                 jnp.concatenate([zero, w_ref[:, :wide - LANES]], axis=1)], axis=0)
        y = _dot(us, w_two)
        if s == 0:
            acc_ref[...] = y
        else:
            acc_ref[:, s * LANES:] += y
        z = _dot(us, f_ref[s // 2])
        accz = z if accz is None else accz + z
    for t in range(ck):
        y_ref[pl.ds(t, n, stride=ck), :] = acc_ref[:, t * LANES:(t + 1) * LANES]
    z_ref[...] = accz


def _s5_local(u, wcat, fmat, rows_per_step):
    r, width = u.shape
    ck = S5_CHUNK
    nt = width // LANES
    sw = fmat.shape[3]
    n = rows_per_step // ck
    return pl.pallas_call(
        functools.partial(_s5_local_body, ck=ck, n=n),
        grid=(nt, r // rows_per_step),
        in_specs=[pl.BlockSpec((rows_per_step, LANES), lambda j, i: (i, j)),
                  pl.BlockSpec((None, LANES, ck * LANES), lambda j, i: (j, 0, 0)),
                  pl.BlockSpec((None, ck // 2, 2 * LANES, sw), lambda j, i: (j, 0, 0, 0))],
        out_specs=[pl.BlockSpec((rows_per_step, LANES), lambda j, i: (i, j)),
                   pl.BlockSpec((None, n, sw), lambda j, i: (j, i, 0))],
        out_shape=[jax.ShapeDtypeStruct((r, width), F32), jax.ShapeDtypeStruct((nt, r // ck, sw), F32)],
        scratch_shapes=[pltpu.VMEM((n, ck * LANES), F32)],
        compiler_params=_cparams(("arbitrary", "arbitrary")),
        name="s5_local",
    )(u, wcat, fmat)


def _s5_scan_body(z_ref, ap_ref, xs_ref, xf_ref, *, nc, half, last):
    xr = z_ref[:, :half]
    xi = z_ref[:, half:]
    row = lax.broadcasted_iota(jnp.int32, (nc, 1), 0)
    lev, s = 0, 1
    while s < nc:
        ar = ap_ref[lev, 0:1, :]
        ai = ap_ref[lev, 1:2, :]
        keep = row >= s
        sr = jnp.where(keep, pltpu.roll(xr, s, 0), 0.0)
        si = jnp.where(keep, pltpu.roll(xi, s, 0), 0.0)
        xr, xi = xr + ar * sr - ai * si, xi + ar * si + ai * sr
        lev, s = lev + 1, s * 2
    xf_ref[:, :half] = xr[last:last + 1, :]
    xf_ref[:, half:] = xi[last:last + 1, :]
    xs_ref[:, :half] = jnp.where(row >= 1, pltpu.roll(xr, 1, 0), 0.0)
    xs_ref[:, half:] = jnp.where(row >= 1, pltpu.roll(xi, 1, 0), 0.0)


def _s5_scan(z, apow, nb, n_real):
    nt, n, sw = z.shape
    nc = n // nb
    nlev = apow.shape[1]
    return pl.pallas_call(
        functools.partial(_s5_scan_body, nc=nc, half=sw // 2, last=n_real - 1),
        grid=(nt, nb),
        in_specs=[pl.BlockSpec((None, nc, sw), lambda j, b: (j, b, 0)),
                  pl.BlockSpec((None, nlev, 2, sw // 2), lambda j, b: (j, 0, 0, 0))],
        out_specs=[pl.BlockSpec((None, nc, sw), lambda j, b: (j, b, 0)),
                   pl.BlockSpec((None, None, 1, sw), lambda j, b: (j, b, 0, 0))],
        out_shape=[jax.ShapeDtypeStruct((nt, n, sw), F32), jax.ShapeDtypeStruct((nt, nb, 1, sw), F32)],
        compiler_params=_cparams(("arbitrary", "arbitrary")),
        name="s5_scan",
    )(z, apow)


def _s5_out_body(y_ref, xs_ref, e_ref, o_ref, act_ref, *, ck, n, first):
    xs = xs_ref[...].astype(BF16)
    for t in range(0, ck, 2):
        pair = [y_ref[pl.ds(t + i, n, stride=ck), :] for i in range(2)]
        if t >= first:
            both = _dot(xs, e_ref[(t - first) // 2])
            pair = [pair[i] + both[:, i * LANES:(i + 1) * LANES] for i in range(2)]
        for i in range(2):
            act_ref[pl.ds(t + i, n, stride=ck), :] = _gelu_tanh(pair[i])
    o_ref[...] = act_ref[...].astype(BF16)


def _s5_out(ylocal, xstart, emat, rows_per_step, first):
    r, width = ylocal.shape
    ck = S5_CHUNK
    nt = width // LANES
    sw = xstart.shape[2]
    n = rows_per_step // ck
    return pl.pallas_call(
        functools.partial(_s5_out_body, ck=ck, n=n, first=first),
        grid=(nt, r // rows_per_step),
        in_specs=[pl.BlockSpec((rows_per_step, LANES), lambda j, i: (i, j)),
                  pl.BlockSpec((None, n, sw), lambda j, i: (j, i, 0)),
                  pl.BlockSpec((None, ck // 2, sw, 2 * LANES), lambda j, i: (j, 0, 0, 0))],
        out_specs=pl.BlockSpec((rows_per_step, LANES), lambda j, i: (i, j)),
        out_shape=jax.ShapeDtypeStruct((r, width), BF16),
        scratch_shapes=[pltpu.VMEM((rows_per_step, LANES), F32)],
        compiler_params=_cparams(("arbitrary", "arbitrary")),
        name="s5_out",
    )(ylocal, xstart, emat)


def _s5_step_body(z_ref, x0_ref, al_ref, xf_ref, *, half):
    ar, ai = al_ref[0:1, :], al_ref[1:2, :]
    xr, xi = x0_ref[:, :half], x0_ref[:, half:]
    xf_ref[:, :half] = z_ref[:, :half] + ar * xr - ai * xi
    xf_ref[:, half:] = z_ref[:, half:] + ar * xi + ai * xr


def _s5_step(z, x0, alast):
    nt, n, sw = z.shape
    blk = pl.BlockSpec((None, n, sw), lambda j: (j, 0, 0))
    return pl.pallas_call(
        functools.partial(_s5_step_body, half=sw // 2),
        grid=(nt,),
        in_specs=[blk, blk, pl.BlockSpec((None, 2, sw // 2), lambda j: (j, 0, 0))],
        out_specs=blk,
        out_shape=jax.ShapeDtypeStruct(z.shape, F32),
        compiler_params=_cparams(("arbitrary",)),
        name="s5_step",
    )(z, x0, alast)


def _gdn_body(x_ref, z_ref, sm_ref, ci_ref, s0_ref, cw_ref, na_ref, db_ref, nw_ref, o_ref, sf_ref, s_sc, car_sc,
              *, c, sb, n_valid, n_heads, hd, kconv, a_lane, b_lane):
    ci = pl.program_id(1)
    gw = n_heads * hd

    hist = ci_ref.shape[1]

    @pl.when(ci == 0)
    def _():
        s_sc[...] = s0_ref[...]
        car_sc[:, :hist, :] = ci_ref[...]

    r = lax.broadcasted_iota(jnp.int32, (c, c), 0)
    cc = lax.broadcasted_iota(jnp.int32, (c, c), 1)
    incl = r >= cc
    strict = r > cc
    eye = (r == cc).astype(F32)
    row = ci * c + lax.broadcasted_iota(jnp.int32, (c, 1), 0)
    chains = []
    for b in range(sb):
        car_sc[b, hist:, :] = x_ref[b]
        y = None
        for j in range(kconv):
            term = car_sc[b, pl.ds(hist - (kconv - 1) + j, c), :] * cw_ref[j:j + 1, :]
            y = term if y is None else y + term
        car_sc[b, :hist, :] = car_sc[b, c:, :]
        qkv = y * _sigmoid(y)

        sm = sm_ref[b]
        sp_in = sm + db_ref[...]
        softplus = jnp.maximum(sp_in, 0.0) + jnp.log(1.0 + jnp.exp(-jnp.abs(sp_in)))
        gfull = na_ref[...] * softplus
        beta_full = _sigmoid(sm)
        if n_valid is not None:
            live = row < n_valid
            gfull = jnp.where(live, gfull, 0.0)
            beta_full = jnp.where(live, beta_full, 0.0)
            qkv = jnp.where(live, qkv, 0.0)
        cum_full = _dot_hi(incl.astype(F32), gfull)
        cum_t = cum_full.T
        for h in range(n_heads):
            q = qkv[:, h * hd:(h + 1) * hd]
            k = qkv[:, gw + h * hd:gw + (h + 1) * hd]
            v = qkv[:, 2 * gw + h * hd:2 * gw + (h + 1) * hd]
            q = q * lax.rsqrt(jnp.sum(q * q, axis=-1, keepdims=True) + 1e-6) * (hd ** -0.5)
            k = k * lax.rsqrt(jnp.sum(k * k, axis=-1, keepdims=True) + 1e-6)
            cum = cum_full[:, a_lane + h:a_lane + h + 1]
            cum_r = cum_t[a_lane + h:a_lane + h + 1, :]
            beta = beta_full[:, b_lane + h:b_lane + h + 1]
            decay = jnp.where(incl, jnp.exp(jnp.where(incl, cum - cum_r, 0.0)), 0.0)
            kb = k * beta
            k16 = k.astype(BF16)
            ecum = jnp.exp(cum)
            g_last = cum[c - 1:c, :]
            chains.append(dict(
                b=b, h=h, k16=k16, decay=decay, qe16=(q * ecum).astype(BF16), q16=q.astype(BF16),
                rhs=jnp.concatenate([v * beta, kb * ecum], axis=-1), kb16=kb.astype(BF16),
                kd=k * jnp.exp(g_last - cum), s_scale=jnp.exp(g_last)))
    for ch in chains:
        ch['lower'] = jnp.where(strict, _dot_nt(ch['kb16'], ch['k16']) * ch['decay'], 0.0)
        ch['aqk16'] = jnp.where(incl, _dot_nt(ch['q16'], ch['k16']) * ch['decay'], 0.0).astype(BF16)
    for ch in chains:
        ch['t_inv'] = eye - ch['lower']
        l_hi, l_lo = _split(ch['lower'])
        ch['pw'] = _dot3s(l_hi, l_lo, l_hi, l_lo)
    span = 2
    while span < c:
        span *= 2
        for ch in chains:
            p_hi, p_lo = _split(ch['pw'])
            t_hi, t_lo = _split(ch['t_inv'])
            if span < c:
                both = _dot3s(p_hi, p_lo, jnp.concatenate([t_hi, p_hi], axis=1),
                              jnp.concatenate([t_lo, p_lo], axis=1))
                ch['t_inv'] = ch['t_inv'] + both[:, :c]
                ch['pw'] = both[:, c:]
            else:
                ch['t_inv'] = ch['t_inv'] + _dot3s(p_hi, p_lo, t_hi, t_lo)
    for ch in chains:
        ch['sol'] = _dot3(ch['t_inv'], ch['rhs'])
    for ch in chains:
        s_h = s_sc[ch['b'], ch['h']]
        s16 = s_h.astype(BF16)
        v_new = ch['sol'][:, :hd] - _dot(ch['sol'][:, hd:].astype(BF16), s16)
        ch['vn16'] = v_new.astype(BF16)
        ch['o'] = _dot(ch['qe16'], s16)
        ch['s_dec'] = s_h * ch['s_scale']
    for ch in chains:
        b, h = ch['b'], ch['h']
        o = ch['o'] + _dot(ch['aqk16'], ch['vn16'])
        s_sc[b, h] = ch['s_dec'] + _dot(ch['kd'].T.astype(BF16), ch['vn16'])
        o = o * lax.rsqrt(jnp.mean(o * o, axis=-1, keepdims=True) + RMS_EPS) * nw_ref[...]
        zz = z_ref[b, :, h * hd:(h + 1) * hd]
        o_ref[b, :, h * hd:(h + 1) * hd] = (o * (zz * _sigmoid(zz))).astype(BF16)

    @pl.when(ci == pl.num_programs(1) - 1)
    def _():
        sf_ref[...] = s_sc[...]


def _gdn(x, z, sm, conv_init, s0, conv_w, neg_a, dt_b, norm_w, sb, n_valid, a_lane, b_lane):
    ns, length, w3 = x.shape
    n_heads, hd = s0.shape[1], s0.shape[2]
    gw = n_heads * hd
    c = GDN_CHUNK
    kconv = conv_w.shape[0]
    hist = conv_init.shape[1]
    rowblk = lambda width: pl.BlockSpec((sb, c, width), lambda s, i: (s, i, 0))
    const = lambda shape: pl.BlockSpec(shape, lambda s, i: (0,) * len(shape))
    return pl.pallas_call(
        functools.partial(_gdn_body, c=c, sb=sb, n_valid=n_valid, n_heads=n_heads, hd=hd, kconv=kconv,
                          a_lane=a_lane, b_lane=b_lane),
        grid=(ns // sb, length // c),
        in_specs=[rowblk(w3), rowblk(gw), rowblk(LANES),
                  pl.BlockSpec((sb, hist, w3), lambda s, i: (s, 0, 0)),
                  pl.BlockSpec((sb, n_heads, hd, hd), lambda s, i: (s, 0, 0, 0)),
                  const((kconv, w3)), const((1, LANES)), const((1, LANES)), const((1, hd))],
        out_specs=[rowblk(gw), pl.BlockSpec((sb, n_heads, hd, hd), lambda s, i: (s, 0, 0, 0))],
        out_shape=[jax.ShapeDtypeStruct((ns, length, gw), BF16), jax.ShapeDtypeStruct(s0.shape, F32)],
        scratch_shapes=[pltpu.VMEM((sb, n_heads, hd, hd), F32), pltpu.VMEM((sb, hist + c, w3), F32)],
        compiler_params=_cparams(("arbitrary", "arbitrary")),
        name="gdn",
    )(x, z, sm, conv_init, s0, conv_w, neg_a, dt_b, norm_w)


def _merge_body(x_ref, h_ref, of_ref, z5_ref, og_ref, wf_ref, w5_ref, wg_ref, wgate_ref, wo_ref, o_ref, *, d):
    fox_br = _dot(of_ref[...], wf_ref[...])
    glu = _dot(z5_ref[...], w5_ref[...])
    s5_br = glu[:, :d] * _sigmoid(glu[:, d:])
    gdn_br = _dot(og_ref[...], wg_ref[...])
    h = h_ref[...]
    merged = (_sigmoid(_dot(h, wgate_ref[:, :d])) * fox_br
              + _sigmoid(_dot(h, wgate_ref[:, d:2 * d])) * s5_br
              + _sigmoid(_dot(h, wgate_ref[:, 2 * d:])) * gdn_br)
    o_ref[...] = x_ref[...] + _dot(merged.astype(BF16), wo_ref[...])


def _merge(x, h, o_fox, z5, o_gdn, wf, w5, wg, wgate, wo, layer):
    t, d = x.shape
    tm = _row_tile(t)
    row = lambda a: pl.BlockSpec((tm, a.shape[1]), lambda i: (i, 0))
    res = lambda a: _resident((None,) + a.shape[1:], lambda i: (layer, 0, 0))
    return pl.pallas_call(
        functools.partial(_merge_body, d=d),
        grid=(t // tm,),
        in_specs=[row(x), row(h), row(o_fox), row(z5), row(o_gdn), res(wf), res(w5), res(wg), res(wgate), res(wo)],
        out_specs=pl.BlockSpec((tm, d), lambda i: (i, 0)),
        out_shape=jax.ShapeDtypeStruct((t, d), F32),
        compiler_params=_cparams(("arbitrary",)),
        name="merge",
    )(x, h, o_fox, z5, o_gdn, wf, w5, wg, wgate, wo)


def _final_norm_body(x_ref, w_ref, o_ref):
    x = x_ref[...]
    ms = jnp.mean(x * x, axis=-1, keepdims=True)
    o_ref[...] = x * lax.rsqrt(ms + RMS_EPS) * w_ref[...]


def _final_norm(x, w):
    t, d = x.shape
    tm = _row_tile(t)
    return pl.pallas_call(
        _final_norm_body,
        grid=(t // tm,),
        in_specs=[pl.BlockSpec((tm, d), lambda i: (i, 0)), pl.BlockSpec((1, d), lambda i: (0, 0))],
        out_specs=pl.BlockSpec((tm, d), lambda i: (i, 0)),
        out_shape=jax.ShapeDtypeStruct((t, d), F32),
        compiler_params=_cparams(("arbitrary",)),
        name="final_norm",
    )(x, w)


def kernel(x_prompt, x_sample, cache_fox_k, cache_fox_v, cache_fox_logf, state_s5_re, state_s5_im, state_gdn, state_gdn_conv, page_table, meta_tokens, norm_ffn1, w_ffn1_in, w_ffn1_out, norm_mix, w_in, fox_b_f, w_fox_br, s5_A_re, s5_A_im, s5_B_re, s5_B_im, s5_C_re, s5_C_im, s5_D, s5_log_dt, w_s5_glu, gdn_conv_w, gdn_A_log, gdn_dt_bias, gdn_norm, w_gdn_br, w_out, norm_ffn2, w_ffn2_in, w_ffn2_out, norm_final):
    nb, seq, d = x_prompt.shape
    ds_b, ds_t, _ = x_sample.shape
    depth = w_in.shape[0]
    n_meta = meta_tokens.shape[0]
    fh, fhd = cache_fox_k.shape[3], cache_fox_k.shape[4]
    fw = fh * fhd
    page = cache_fox_k.shape[2]
    sg, sp, sgs = s5_B_re.shape[1:]
    sw = sg * sgs
    gh, ghd = state_gdn.shape[2], state_gdn.shape[3]
    gw = gh * ghd
    kconv = gdn_conv_w.shape[1]
    ck = S5_CHUNK
    gc = GDN_CHUNK
    assert fw % LANES == 0 and 2 * fhd == LANES and ghd == LANES and sw % LANES == 0 and LANES % sgs == 0
    assert ds_t <= ck and (ck - ds_t) % 2 == 0 and ds_t >= kconv - 1 and ds_t <= gc
    assert (n_meta + seq) % ck == 0 and n_meta + seq >= kconv - 1
    assert page == LANES and fh + 2 * gh <= LANES

    lreal = n_meta + seq
    pad = (-lreal) % LANES
    lp = lreal + pad
    tp = nb * lp
    ts = ds_b * ds_t
    tq = max(c for c in (128, 256, 384, 512) if lp % c == 0)
    n_pages = page_table.shape[1]
    g_pages = max(c for c in (1, 2, 4, 8) if n_pages % c == 0)
    hist = 8
    nt = sw // LANES
    gt = LANES // sgs
    ssw = 2 * gt * sp

    sizes = (fw, fw, fw, fh, sw, 3 * gw, gh, gh, gw, 3 * d)
    offs = [0]
    for s in sizes:
        offs.append(offs[-1] + s)
    col = lambda i: w_in[:, :, offs[i]:offs[i + 1]]
    small = jnp.concatenate([col(3), col(6), col(7),
                             jnp.zeros((depth, d, LANES - fh - 2 * gh), F32)], axis=2)
    a_lane, b_lane = fh, fh + gh
    w_proj = jnp.concatenate([col(0), col(1), col(2), col(4), col(5), col(8), small], axis=2).astype(BF16)
    w_gate = col(9).astype(BF16)
    w_kvt = jnp.transpose(w_in[:, :, offs[1]:offs[3]], (0, 2, 1)).astype(BF16)
    q_scale = fhd ** -0.5 * LOG2E
    wi1, wo1 = w_ffn1_in.astype(BF16), w_ffn1_out.astype(BF16)
    wi2, wo2 = w_ffn2_in.astype(BF16), w_ffn2_out.astype(BF16)
    wf16, w516, wg16, wo16 = (w.astype(BF16) for w in (w_fox_br, w_s5_glu, w_gdn_br, w_out))
    nrm = lambda w: w.reshape(depth, 1, d)
    n1, nm, n2 = nrm(norm_ffn1), nrm(norm_mix), nrm(norm_ffn2)
    lane_vec = lambda v, at: jnp.zeros((depth, 1, LANES), F32).at[:, 0, at:at + v.shape[1]].set(v)
    neg_a = lane_vec(-jnp.exp(gdn_A_log), a_lane)
    dt_b = lane_vec(gdn_dt_bias, a_lane)
    s5_tab = [_s5_tables(s5_A_re[l], s5_A_im[l], s5_B_re[l], s5_B_im[l], s5_C_re[l], s5_C_im[l], s5_D[l],
                         s5_log_dt[l], ds_t) for l in range(depth)]

    kt_cache = jnp.transpose(cache_fox_k, (0, 1, 3, 4, 2))
    vt_cache = jnp.transpose(cache_fox_v, (0, 1, 3, 4, 2))
    lf_cache = jnp.transpose(cache_fox_logf, (0, 1, 3, 2))

    meta = jnp.broadcast_to(meta_tokens[None].astype(F32), (nb, n_meta, d))
    xp = jnp.concatenate([meta, x_prompt, jnp.zeros((nb, pad, d), F32)], axis=1).reshape(tp, d)
    xs = x_sample.reshape(ts, d)

    zeros_conv = jnp.zeros((nb, hist, 3 * gw), F32)
    zeros_gdn = jnp.zeros((nb, gh, ghd, ghd), F32)
    sb_p = max(c for c in (1, 2, 4) if nb % c == 0)
    sb_s = max(c for c in (1, 2, 4) if ds_b % c == 0)
    head_eye = jnp.eye(fh, dtype=BF16).reshape(1, 1, fh, fh, 1)
    to_tile = lambda st: jnp.transpose(st.reshape(ds_b, nt, gt * sp), (1, 0, 2))
    from_tile = lambda a, n: jnp.transpose(a, (1, 0, 2)).reshape(n, sg, sp)
    outs = [[] for _ in range(12)]
    kv_stack = None

    for l in range(depth):
        xp = _ffn(xp, n1, wi1, wo1, l)
        xs = _ffn(xs, n1, wi1, wo1, l)
        hp, qp, kp, vp, up, gqp, zp, smp, k_stack, v_stack = _inproj(
            xp, nm, w_proj, l, fw, sw, gw, q_scale, w_kvt, lp, kv_stack, (depth, nb, fw, lp))
        kv_stack = (k_stack, v_stack)
        hs, qs, ks, vs, kvs, us, gqs, zs, sms = _inproj(xs, nm, w_proj, l, fw, sw, gw, q_scale)
        bias_f = fox_b_f[l].reshape(fh, 1)

        lg_p = jnp.transpose(smp[:, :fh].reshape(nb, lp, fh), (0, 2, 1))
        lf_p, cum_p = _fcum(lg_p, bias_f)
        o_fox_p = _fox_prompt(qp, kp, vp, cum_p, nb, lp, fh, fhd, tq)
        lg_s = jnp.transpose(sms[:, :fh].reshape(ds_b, ds_t, fh), (0, 2, 1))
        lg_s = jnp.pad(lg_s, ((0, 0), (0, 0), (0, LANES - ds_t)))
        lf_s, cum_s = _fcum(lg_s, bias_f)
        qbd = (qs.reshape(ds_b, ds_t, 1, fh, fhd) * head_eye).reshape(ds_b, ds_t * fh, fw)
        rows_pad = ((0, 0), (0, LANES - ds_t), (0, 0))
        k_new = jnp.pad(ks.reshape(ds_b, ds_t, fw), rows_pad)
        v_new = jnp.pad(vs.reshape(ds_b, ds_t, fw), rows_pad)
        o_fox_s = _fox_sample(page_table, qbd, cum_s, k_new, v_new, kt_cache, vt_cache, lf_cache, l, g_pages, ds_t)

        wcat, fmat, emat, apow, alast = s5_tab[l]
        y_loc, z_loc = _s5_local(up, wcat, fmat, lp)
        x_start, x_fin = _s5_scan(z_loc, apow, nb, lreal // ck)
        z5_p = _s5_out(y_loc, x_start, emat, lp, 0)
        us_c = jnp.pad(us.reshape(ds_b, ds_t, sw), ((0, 0), (ck - ds_t, 0), (0, 0))).reshape(ds_b * ck, sw)
        x0 = jnp.concatenate([to_tile(state_s5_re[l]), to_tile(state_s5_im[l])], axis=2)
        y_loc_s, z_loc_s = _s5_local(us_c, wcat, fmat, ds_b * ck)
        z5_s = _s5_out(y_loc_s, x0, emat, ds_b * ck, ck - ds_t)
        z5_s = z5_s.reshape(ds_b, ck, sw)[:, ck - ds_t:].reshape(ts, sw)
        xf_s = _s5_step(z_loc_s, x0, alast)

        cw = gdn_conv_w[l]
        gnorm = gdn_norm[l].reshape(1, ghd)
        o_gdn_p, s_p = _gdn(gqp.reshape(nb, lp, 3 * gw), zp.reshape(nb, lp, gw), smp.reshape(nb, lp, LANES),
                            zeros_conv, zeros_gdn, cw, neg_a[l], dt_b[l], gnorm, sb_p, lreal if pad else None,
                            a_lane, b_lane)
        seq_pad = lambda a: jnp.pad(a.reshape(ds_b, ds_t, a.shape[1]), ((0, 0), (0, gc - ds_t), (0, 0)))
        conv_s = jnp.pad(state_gdn_conv[l], ((0, 0), (hist - (kconv - 1), 0), (0, 0)))
        o_gdn_s, s_s = _gdn(seq_pad(gqs), seq_pad(zs), seq_pad(sms), conv_s, state_gdn[l], cw, neg_a[l], dt_b[l],
                            gnorm, sb_s, ds_t, a_lane, b_lane)

        xp = _merge(xp, hp, o_fox_p, z5_p, o_gdn_p.reshape(tp, gw), wf16, w516, wg16, w_gate, wo16, l)
        xs = _merge(xs, hs, o_fox_s.reshape(ts, fw), z5_s, o_gdn_s[:, :ds_t].reshape(ts, gw),
                    wf16, w516, wg16, w_gate, wo16, l)
        xp = _ffn(xp, n2, wi2, wo2, l)
        xs = _ffn(xs, n2, wi2, wo2, l)

        kv_s = kvs.reshape(ds_b, ds_t, 2, fh, fhd)
        xf_p = x_fin.reshape(nt, nb, ssw)
        gq_s = jnp.concatenate([state_gdn_conv[l], gqs.reshape(ds_b, ds_t, 3 * gw)], axis=1)
        new = (jnp.transpose(lf_p[:, :, :lreal], (0, 2, 1)),
               kv_s[:, :, 0], kv_s[:, :, 1], jnp.transpose(lf_s[:, :, :ds_t], (0, 2, 1)),
               from_tile(xf_p[:, :, :ssw // 2], nb), from_tile(xf_p[:, :, ssw // 2:], nb),
               from_tile(xf_s[:, :, :ssw // 2], ds_b), from_tile(xf_s[:, :, ssw // 2:], ds_b),
               s_p, s_s, gqp.reshape(nb, lp, 3 * gw)[:, lreal - (kconv - 1):lreal], gq_s[:, -(kconv - 1):])
        for i, a in enumerate(new):
            outs[i].append(a)

    y_prompt = _final_norm(xp, norm_final.reshape(1, d)).reshape(nb, lp, d)[:, n_meta:lreal]
    y_sample = _final_norm(xs, norm_final.reshape(1, d)).reshape(ds_b, ds_t, d)
    k_prompt, v_prompt = (jnp.transpose(a.reshape(depth, nb, fh, fhd, lp)[..., :lreal], (0, 1, 4, 2, 3))
                          for a in kv_stack)
    return (y_prompt, y_sample, k_prompt, v_prompt) + tuple(jnp.stack(o) for o in outs)
```
